```python
import jax
import jax.numpy as jnp
from jax import lax
import numpy as np

D_MODEL = 1024
BATCH = 8
SEQ = 2048
DEPTH = 1
DEC_BATCH = 32
DEC_SEQ = 16
PAST_LEN = 4096

CHUNK = 64
N_PAST_CHUNKS = 8
BAND_CHUNKS = N_PAST_CHUNKS + 1
BAND = BAND_CHUNKS * CHUNK
A_HEADS = 8
A_HEAD_DIM = 64
A_WIDTH = A_HEADS * A_HEAD_DIM
REL_CLIP = 128
G_HEADS = 4
G_DK = 64
G_DV = 128
G_WIDTH = G_HEADS * G_DV
G_GATE_RANK = 16
G_GATE_NORM = 16.0
G_BLOCK = 16
MIX_WIDTH = A_WIDTH + G_WIDTH
N_MEM = 256
X_HEADS = 4
X_HEAD_DIM = D_MODEL // X_HEADS
X_WIDTH = X_HEADS * X_HEAD_DIM
P_HEADS = 8
P_NKEYS = 128
P_EXPERTS = P_NKEYS * P_NKEYS
P_QDIM = 256
P_HALF = P_QDIM // 2
P_TOPK = 16
P_TOKEN_BLOCK = 128
EPS = 1e-6
IN_WIDTH = 3 * A_WIDTH + 2 * G_HEADS * G_DK + G_WIDTH + G_GATE_RANK + G_WIDTH
SPLITS = (A_WIDTH, 2 * A_WIDTH, 3 * A_WIDTH,
          3 * A_WIDTH + G_HEADS * G_DK,
          3 * A_WIDTH + 2 * G_HEADS * G_DK,
          3 * A_WIDTH + 2 * G_HEADS * G_DK + G_WIDTH,
          3 * A_WIDTH + 2 * G_HEADS * G_DK + G_WIDTH + G_GATE_RANK)

kernel_name = 'hymba_style_chunkband_gla_peer_stream_step'


def rmsnorm(x, g):
    xf = x.astype(jnp.float32)
    y = xf * lax.rsqrt(jnp.mean(xf * xf, axis=-1, keepdims=True) + EPS)
    return (y * g.astype(jnp.float32)).astype(x.dtype)


def mixer_inputs(x, g_mix, w_in, g_q, g_k, w_gate2, b_gate2):
    b, t, _ = x.shape
    z = rmsnorm(x, g_mix) @ w_in
    qa, ka, va, qg, kg, vg, gl, r = jnp.split(z, SPLITS, axis=-1)
    qa = rmsnorm(qa.reshape(b, t, A_HEADS, A_HEAD_DIM), g_q)
    ka = rmsnorm(ka.reshape(b, t, A_HEADS, A_HEAD_DIM), g_k)
    va = va.reshape(b, t, A_HEADS, A_HEAD_DIM)
    qg = qg.reshape(b, t, G_HEADS, G_DK) * (G_DK ** -0.5)
    kg = kg.reshape(b, t, G_HEADS, G_DK)
    vg = vg.reshape(b, t, G_HEADS, G_DV)
    log_a = jax.nn.log_sigmoid((gl @ w_gate2 + b_gate2).astype(jnp.float32)) / G_GATE_NORM
    log_a = log_a.reshape(b, t, G_HEADS, G_DK)
    return qa, ka, va, qg, kg, vg, log_a, r


def rel_bias_lookup(rel_bias, rel):
    return rel_bias[:, jnp.clip(rel, -REL_CLIP, REL_CLIP) + REL_CLIP].astype(jnp.float32)


def chunk_band_attention_prompt(q, k, v, rel_bias):
    b, s, h, dh = q.shape
    nc = s // CHUNK
    qc = q.reshape(b, nc, CHUNK, h, dh)
    padw = ((0, 0), (N_PAST_CHUNKS, 0), (0, 0), (0, 0), (0, 0))
    kc = jnp.pad(k.reshape(b, nc, CHUNK, h, dh), padw)
    vc = jnp.pad(v.reshape(b, nc, CHUNK, h, dh), padw)
    k_band = jnp.concatenate([kc[:, i:i + nc] for i in range(BAND_CHUNKS)], axis=2)
    v_band = jnp.concatenate([vc[:, i:i + nc] for i in range(BAND_CHUNKS)], axis=2)
    qi = jnp.arange(CHUNK)
    kj = jnp.arange(BAND)
    bias = rel_bias_lookup(rel_bias, N_PAST_CHUNKS * CHUNK + qi[:, None] - kj[None, :])
    key_pos = (jnp.arange(nc)[:, None] - N_PAST_CHUNKS) * CHUNK + kj[None, :]
    valid = (key_pos >= 0)[None, :, None, None, :]
    scores = jnp.einsum('bnqhd,bnkhd->bnhqk', qc, k_band).astype(jnp.float32) * (dh ** -0.5) + bias[None, None]
    scores = jnp.where(valid, scores, -1e30)
    p = jax.nn.softmax(scores, axis=-1).astype(v.dtype)
    o = jnp.einsum('bnhqk,bnkhd->bnqhd', p, v_band)
    return o.reshape(b, s, h * dh)


def chunk_band_attention_sample(q, k_new, v_new, k_cache, v_cache, rel_bias):
    b, t, h, dh = q.shape
    r = k_cache.shape[1]
    q_pos = PAST_LEN + jnp.arange(t)
    k_pos = jnp.concatenate([PAST_LEN - r + jnp.arange(r), q_pos])
    k_all = jnp.concatenate([k_cache.astype(k_new.dtype), k_new], axis=1)
    v_all = jnp.concatenate([v_cache.astype(v_new.dtype), v_new], axis=1)
    bias = rel_bias_lookup(rel_bias, q_pos[:, None] - k_pos[None, :])
    qch = q_pos[:, None] // CHUNK
    kch = k_pos[None, :] // CHUNK
    valid = (kch <= qch) & (kch >= qch - N_PAST_CHUNKS)
    scores = jnp.einsum('bqhd,bkhd->bhqk', q, k_all).astype(jnp.float32) * (dh ** -0.5) + bias[None]
    scores = jnp.where(valid[None, None], scores, -1e30)
    p = jax.nn.softmax(scores, axis=-1).astype(v_all.dtype)
    o = jnp.einsum('bhqk,bkhd->bqhd', p, v_all)
    return o.reshape(b, t, h * dh)


def gla_scan(q, k, v, log_a, s0):
    b, t, h, dk = q.shape
    pad = (-t) % G_BLOCK
    padw = ((0, 0), (0, pad), (0, 0), (0, 0))
    q, k, v, log_a = [jnp.pad(a.astype(jnp.float32), padw) for a in (q, k, v, log_a)]
    nb = (t + pad) // G_BLOCK

    def blocks(a):
        return jnp.moveaxis(a.reshape(b, nb, G_BLOCK, h, a.shape[-1]), 1, 0)

    causal = jnp.tril(jnp.ones((G_BLOCK, G_BLOCK), dtype=bool))[None, :, :, None, None]

    def step(state, inp):
        qb, kb, vb, lab = inp
        cum = jnp.cumsum(lab, axis=1)
        o_inter = jnp.einsum('blhk,bhkv->blhv', qb * jnp.exp(cum), state)
        diff = cum[:, :, None] - cum[:, None, :]
        decay = jnp.exp(jnp.where(causal, diff, -jnp.inf))
        att = jnp.einsum('bqhk,bshk,bqshk->bhqs', qb, kb, decay)
        o_intra = jnp.einsum('bhqs,bshv->bqhv', att, vb)
        last = cum[:, -1]
        k_dec = kb * jnp.exp(last[:, None] - cum)
        state = jnp.exp(last)[..., None] * state + jnp.einsum('bshk,bshv->bhkv', k_dec, vb)
        return state, o_inter + o_intra

    s_final, o = lax.scan(step, s0.astype(jnp.float32), (blocks(q), blocks(k), blocks(v), blocks(log_a)))
    o = jnp.moveaxis(o, 0, 1).reshape(b, nb * G_BLOCK, h, -1)[:, :t]
    return o, s_final


def mixer_output(oa, og, r, g_gla_out, w_out):
    b, t = oa.shape[:2]
    og = rmsnorm(og.astype(r.dtype), g_gla_out).reshape(b, t, G_WIDTH) * jax.nn.silu(r)
    return jnp.concatenate([oa.astype(r.dtype), og], axis=-1) @ w_out


def mem_kv(mem, g_mem, w_ck, w_cv, g_ck):
    m = rmsnorm(mem, g_mem)
    b, n, _ = m.shape
    k = rmsnorm((m @ w_ck).reshape(b, n, X_HEADS, X_HEAD_DIM), g_ck)
    v = (m @ w_cv).reshape(b, n, X_HEADS, X_HEAD_DIM)
    return k, v


def cross_attend(h, mem_k, mem_v, w_cq, g_cq, w_co):
    b, t, _ = h.shape
    q = rmsnorm((h @ w_cq).reshape(b, t, X_HEADS, X_HEAD_DIM), g_cq)
    s = jnp.einsum('bqhd,bkhd->bhqk', q, mem_k.astype(q.dtype)).astype(jnp.float32) * (X_HEAD_DIM ** -0.5)
    p = jax.nn.softmax(s, axis=-1).astype(q.dtype)
    o = jnp.einsum('bhqk,bkhd->bqhd', p, mem_v.astype(q.dtype)).reshape(b, t, X_WIDTH)
    return o @ w_co


def peer_ffn(h, w_pq, sub_keys1, sub_keys2, expert_u, expert_v):
    lead = h.shape[:-1]
    xt = h.reshape(-1, D_MODEL)
    n = xt.shape[0]
    qry = (xt @ w_pq).reshape(n, P_HEADS, 2, P_HALF).astype(jnp.float32)
    s1 = jnp.einsum('nhd,kd->nhk', qry[:, :, 0], sub_keys1.astype(jnp.float32))
    s2 = jnp.einsum('nhd,kd->nhk', qry[:, :, 1], sub_keys2.astype(jnp.float32))
    v1, i1 = lax.top_k(s1, P_TOPK)
    v2, i2 = lax.top_k(s2, P_TOPK)
    cand = (v1[..., :, None] + v2[..., None, :]).reshape(n, P_HEADS, P_TOPK * P_TOPK)
    cand_idx = (i1[..., :, None] * P_NKEYS + i2[..., None, :]).reshape(n, P_HEADS, P_TOPK * P_TOPK)
    top_s, top_pos = lax.top_k(cand, P_TOPK)
    idx = jnp.take_along_axis(cand_idx, top_pos, axis=-1).reshape(n, P_HEADS * P_TOPK)
    gate = jax.nn.softmax(top_s, axis=-1).reshape(n, P_HEADS * P_TOPK).astype(xt.dtype)
    pad = (-n) % P_TOKEN_BLOCK
    xb = jnp.pad(xt, ((0, pad), (0, 0))).reshape(-1, P_TOKEN_BLOCK, D_MODEL)
    ib = jnp.pad(idx, ((0, pad), (0, 0))).reshape(-1, P_TOKEN_BLOCK, P_HEADS * P_TOPK)
    gb = jnp.pad(gate, ((0, pad), (0, 0))).reshape(-1, P_TOKEN_BLOCK, P_HEADS * P_TOPK)

    def block(args):
        xblk, iblk, gblk = args
        act = jax.nn.gelu(jnp.einsum('td,tkd->tk', xblk, expert_u[iblk]), approximate=False) * gblk
        return jnp.einsum('tk,tkd->td', act, expert_v[iblk])

    out = lax.map(block, (xb, ib, gb)).reshape(-1, D_MODEL)[:n]
    return out.reshape(*lead, D_MODEL)


def layer_tail(x, mem_k, mem_v, g_cross, w_cq, g_cq, w_co, g_ffn, w_pq, sub_keys1, sub_keys2, expert_u, expert_v):
    x = x + cross_attend(rmsnorm(x, g_cross), mem_k, mem_v, w_cq, g_cq, w_co)
    return x + peer_ffn(rmsnorm(x, g_ffn), w_pq, sub_keys1, sub_keys2, expert_u, expert_v)


def setup_inputs(seed: int = 0) -> dict:
    key = jax.random.key(seed)
    ks = iter(jax.random.split(key, 40))
    nrm = lambda shape, scale: jax.random.normal(next(ks), shape, jnp.float32) * scale
    gain = lambda shape: 1.0 + 0.02 * jax.random.normal(next(ks), shape, jnp.float32)
    r_cache = min(BAND, PAST_LEN)
    return {
        'x_prompt': nrm((BATCH, SEQ, D_MODEL), 1.0),
        'x_sample': nrm((DEC_BATCH, DEC_SEQ, D_MODEL), 1.0),
        'cache_att_k': nrm((DEPTH, DEC_BATCH, r_cache, A_HEADS, A_HEAD_DIM), 1.0),
        'cache_att_v': nrm((DEPTH, DEC_BATCH, r_cache, A_HEADS, A_HEAD_DIM), 1.0),
        'state_gla': nrm((DEPTH, DEC_BATCH, G_HEADS, G_DK, G_DV), 1.0),
        'cache_mem_k': nrm((DEPTH, DEC_BATCH, N_MEM, X_HEADS, X_HEAD_DIM), 1.0),
        'cache_mem_v': nrm((DEPTH, DEC_BATCH, N_MEM, X_HEADS, X_HEAD_DIM), 1.0),
        'mem_prompt': nrm((BATCH, N_MEM, D_MODEL), 1.0),
        'g_mix': gain((DEPTH, D_MODEL)),
        'w_in': nrm((DEPTH, D_MODEL, IN_WIDTH), D_MODEL ** -0.5),
        'g_q': gain((DEPTH, A_HEAD_DIM)),
        'g_k': gain((DEPTH, A_HEAD_DIM)),
        'rel_bias': nrm((DEPTH, A_HEADS, 2 * REL_CLIP + 1), 0.5),
        'w_gate2': nrm((DEPTH, G_GATE_RANK, G_HEADS * G_DK), G_GATE_RANK ** -0.5),
        'b_gate2': nrm((DEPTH, G_HEADS * G_DK), 0.1),
        'g_gla_out': gain((DEPTH, G_DV)),
        'w_out': nrm((DEPTH, MIX_WIDTH, D_MODEL), MIX_WIDTH ** -0.5),
        'g_cross': gain((DEPTH, D_MODEL)),
        'g_mem': gain((DEPTH, D_MODEL)),
        'w_cq': nrm((DEPTH, D_MODEL, X_WIDTH), D_MODEL ** -0.5),
        'w_ck': nrm((DEPTH, D_MODEL, X_WIDTH), D_MODEL ** -0.5),
        'w_cv': nrm((DEPTH, D_MODEL, X_WIDTH), D_MODEL ** -0.5),
        'g_cq': gain((DEPTH, X_HEAD_DIM)),
        'g_ck': gain((DEPTH, X_HEAD_DIM)),
        'w_co': nrm((DEPTH, X_WIDTH, D_MODEL), X_WIDTH ** -0.5),
        'g_ffn': gain((DEPTH, D_MODEL)),
        'w_pq': nrm((DEPTH, D_MODEL, P_HEADS * P_QDIM), D_MODEL ** -0.5),
        'sub_keys1': nrm((DEPTH, P_NKEYS, P_HALF), P_HALF ** -0.5),
        'sub_keys2': nrm((DEPTH, P_NKEYS, P_HALF), P_HALF ** -0.5),
        'expert_u': nrm((DEPTH, P_EXPERTS, D_MODEL), D_MODEL ** -0.5),
        'expert_v': nrm((DEPTH, P_EXPERTS, D_MODEL), P_HEADS ** -0.5),
    }


def reference(x_prompt, x_sample, cache_att_k, cache_att_v, state_gla, cache_mem_k, cache_mem_v, mem_prompt,
              g_mix, w_in, g_q, g_k, rel_bias, w_gate2, b_gate2, g_gla_out, w_out,
              g_cross, g_mem, w_cq, w_ck, w_cv, g_cq, g_ck, w_co,
              g_ffn, w_pq, sub_keys1, sub_keys2, expert_u, expert_v):
    xp = x_prompt
    xs = x_sample
    keep = min(BAND, xp.shape[1])
    p_k, p_v, p_g, p_mk, p_mv, s_k, s_v, s_g = [], [], [], [], [], [], [], []
    for l in range(DEPTH):
        qa, ka, va, qg, kg, vg, la, r = mixer_inputs(xp, g_mix[l], w_in[l], g_q[l], g_k[l], w_gate2[l], b_gate2[l])
        oa = chunk_band_attention_prompt(qa, ka, va, rel_bias[l])
        s0 = jnp.zeros((xp.shape[0], G_HEADS, G_DK, G_DV), jnp.float32)
        og, gp = gla_scan(qg, kg, vg, la, s0)
        xp = xp + mixer_output(oa, og, r, g_gla_out[l], w_out[l])
        mk, mv = mem_kv(mem_prompt, g_mem[l], w_ck[l], w_cv[l], g_ck[l])
        xp = layer_tail(xp, mk, mv, g_cross[l], w_cq[l], g_cq[l], w_co[l],
                        g_ffn[l], w_pq[l], sub_keys1[l], sub_keys2[l], expert_u[l], expert_v[l])
        p_k.append(ka[:, -keep:])
        p_v.append(va[:, -keep:])
        p_g.append(gp.astype(xp.dtype))
        p_mk.append(mk)
        p_mv.append(mv)
        qa, ka, va, qg, kg, vg, la, r = mixer_inputs(xs, g_mix[l], w_in[l], g_q[l], g_k[l], w_gate2[l], b_gate2[l])
        oa = chunk_band_attention_sample(qa, ka, va, cache_att_k[l], cache_att_v[l], rel_bias[l])
        og, gs = gla_scan(qg, kg, vg, la, state_gla[l])
        xs = xs + mixer_output(oa, og, r, g_gla_out[l], w_out[l])
        xs = layer_tail(xs, cache_mem_k[l], cache_mem_v[l], g_cross[l], w_cq[l], g_cq[l], w_co[l],
                        g_ffn[l], w_pq[l], sub_keys1[l], sub_keys2[l], expert_u[l], expert_v[l])
        s_k.append(ka)
        s_v.append(va)
        s_g.append(gs.astype(state_gla.dtype))
    return (xp, xs, jnp.stack(p_k), jnp.stack(p_v), jnp.stack(p_g), jnp.stack(p_mk), jnp.stack(p_mv),
            jnp.stack(s_k), jnp.stack(s_v), jnp.stack(s_g))
```

```python
import functools
import math

import jax
import jax.numpy as jnp
from jax import lax
from jax.experimental import pallas as pl
from jax.experimental.pallas import tpu as pltpu

F32 = jnp.float32
BF16 = jnp.bfloat16
HIGHEST = lax.Precision.HIGHEST

D_MODEL = 1024
CHUNK = 64
N_PAST_CHUNKS = 8
BAND = (N_PAST_CHUNKS + 1) * CHUNK
A_HEADS = 8
A_HEAD_DIM = 64
A_WIDTH = A_HEADS * A_HEAD_DIM
REL_CLIP = 128
G_HEADS = 4
G_DK = 64
G_DV = 128
G_KW = G_HEADS * G_DK
G_WIDTH = G_HEADS * G_DV
G_GATE_RANK = 16
G_GATE_NORM = 16.0
G_BLOCK = 16
N_MEM = 256
X_HEADS = 4
X_HEAD_DIM = D_MODEL // X_HEADS
P_HEADS = 8
P_NKEYS = 128
P_QDIM = 256
P_HALF = P_QDIM // 2
P_TOPK = 16
EPS = 1e-6
NEG = -1e30

LANES = 128
VMEM_LIMIT = 56 * 1024 * 1024

TOKEN_TILE = 256
GLA_TILE = 256
PEER_TOKEN_TILE = 512
PEER_EXPERT_TILE = 256


def _params(*sem):
    return pltpu.CompilerParams(dimension_semantics=sem, vmem_limit_bytes=VMEM_LIMIT)


def _full(shape):
    return pl.BlockSpec(shape, lambda *_: (0,) * len(shape))


def _rms(x, g):
    return x * lax.rsqrt(jnp.mean(x * x, axis=-1, keepdims=True) + EPS) * g


def _seg_mean_sq(z, seg_avg):
    sq = z * z
    hi = sq.astype(BF16)
    lo = (sq - hi.astype(F32)).astype(BF16)
    return (jnp.dot(hi, seg_avg, preferred_element_type=F32)
            + jnp.dot(lo, seg_avg, preferred_element_type=F32))


def _seg_avg_matrix(width, seg):
    r = jnp.arange(width) // seg
    return jnp.where(r[:, None] == r[None, :], 1.0 / seg, 0.0).astype(BF16)


def _mixer_in_kernel(x_ref, gmix_ref, w_ref, gq_ref, gk_ref, avg_ref, wg2_ref, bg2_ref,
                     qa_ref, kaf_ref, kab_ref, vaf_ref, vab_ref, qg_ref, kg_ref, vg_ref, la_ref, sr_ref):
    h = _rms(x_ref[...], gmix_ref[...]).astype(BF16)
    z = jnp.dot(h, w_ref[...], preferred_element_type=F32)
    o = 0
    qa = z[:, o:o + A_WIDTH]; o += A_WIDTH
    ka = z[:, o:o + A_WIDTH]; o += A_WIDTH
    va = z[:, o:o + A_WIDTH]; o += A_WIDTH
    qg = z[:, o:o + G_KW]; o += G_KW
    kg = z[:, o:o + G_KW]; o += G_KW
    vg = z[:, o:o + G_WIDTH]; o += G_WIDTH
    r = z[:, o:o + G_WIDTH]; o += G_WIDTH
    gl = z[:, o:o + LANES]
    avg = avg_ref[...]
    qa = qa * lax.rsqrt(_seg_mean_sq(qa, avg) + EPS) * gq_ref[...]
    ka = ka * lax.rsqrt(_seg_mean_sq(ka, avg) + EPS) * gk_ref[...]
    qa_ref[...] = (qa * (A_HEAD_DIM ** -0.5)).astype(BF16)
    kaf_ref[...] = ka
    kab_ref[...] = ka.astype(BF16)
    vaf_ref[...] = va
    vab_ref[...] = va.astype(BF16)
    qg_ref[...] = qg * (G_DK ** -0.5)
    kg_ref[...] = kg
    vg_ref[...] = vg
    pre = jnp.dot(gl, wg2_ref[...], precision=HIGHEST, preferred_element_type=F32) + bg2_ref[...]
    la_ref[...] = (jnp.minimum(pre, 0.0) - jnp.log1p(jnp.exp(-jnp.abs(pre)))) / G_GATE_NORM
    sr_ref[...] = r * jax.nn.sigmoid(r)


def _mixer_in(x, g_mix, w_in, g_q, g_k, w_gate2, b_gate2):
    n = x.shape[0]
    c_gl = 3 * A_WIDTH + 2 * G_KW + G_WIDTH
    w_cat = jnp.concatenate(
        [w_in[:, :c_gl], w_in[:, c_gl + G_GATE_RANK:],
         jnp.pad(w_in[:, c_gl:c_gl + G_GATE_RANK], ((0, 0), (0, LANES - G_GATE_RANK)))], axis=1).astype(BF16)
    wg2 = jnp.pad(w_gate2, ((0, LANES - G_GATE_RANK), (0, 0)))
    wcols = w_cat.shape[1]
    tm = TOKEN_TILE
    row = lambda w: pl.BlockSpec((tm, w), lambda i: (i, 0))
    outs = [(A_WIDTH, BF16), (A_WIDTH, F32), (A_WIDTH, BF16), (A_WIDTH, F32), (A_WIDTH, BF16),
            (G_KW, F32), (G_KW, F32), (G_WIDTH, F32), (G_KW, F32), (G_WIDTH, F32)]
    return pl.pallas_call(
        _mixer_in_kernel,
        grid=(n // tm,),
        in_specs=[row(D_MODEL), _full((1, D_MODEL)), _full((D_MODEL, wcols)), _full((1, A_WIDTH)),
                  _full((1, A_WIDTH)), _full((A_WIDTH, A_WIDTH)), _full((LANES, G_KW)), _full((1, G_KW))],
        out_specs=[row(w) for w, _ in outs],
        out_shape=[jax.ShapeDtypeStruct((n, w), dt) for w, dt in outs],
        compiler_params=_params("parallel"),
        name="mixer_in",
    )(x, g_mix.reshape(1, -1), w_cat, jnp.tile(g_q, A_HEADS).reshape(1, -1), jnp.tile(g_k, A_HEADS).reshape(1, -1),
      _seg_avg_matrix(A_WIDTH, A_HEAD_DIM), wg2, b_gate2.reshape(1, -1))


def _attend(q, segments):
    tq = q.shape[0]
    lane = lax.broadcasted_iota(jnp.int32, (tq, LANES), 1)
    outs = []
    for pair in range(A_HEADS // 2):
        sl = slice(pair * LANES, (pair + 1) * LANES)
        q2 = q[:, sl]
        halves = []
        for half in range(2):
            head = 2 * pair + half
            mine = (lane < A_HEAD_DIM) if half == 0 else (lane >= A_HEAD_DIM)
            qm = jnp.where(mine, q2, jnp.zeros_like(q2))
            scores = [lax.dot_general(qm, k[:, sl], (((1,), (1,)), ((), ())), preferred_element_type=F32) + bias(head)
                      for k, _, bias in segments]
            m = functools.reduce(jnp.maximum, [jnp.max(s, axis=-1, keepdims=True) for s in scores])
            es = [jnp.exp(s - m) for s in scores]
            inv = 1.0 / functools.reduce(jnp.add, [jnp.sum(e, axis=-1, keepdims=True) for e in es])
            o = functools.reduce(jnp.add, [
                jnp.dot((e * inv).astype(BF16), v[:, sl], preferred_element_type=F32)
                for e, (_, v, _) in zip(es, segments)])
            halves.append(o)
        outs.append(jnp.where(lane < A_HEAD_DIM, halves[0], halves[1]))
    return jnp.concatenate(outs, axis=1)


def _attn_prompt_kernel(q_ref, k_ref, v_ref, bias_ref, o_ref):
    c = pl.program_id(1)
    start = pl.multiple_of(c * CHUNK, CHUNK)
    kb = k_ref[0, pl.ds(start, BAND), :]
    vb = v_ref[0, pl.ds(start, BAND), :]
    key_pos = (c - N_PAST_CHUNKS) * CHUNK + lax.broadcasted_iota(jnp.int32, (CHUNK, BAND), 1)
    valid = key_pos >= 0
    bias = lambda head: jnp.where(valid, bias_ref[head], NEG)
    o_ref[...] = _attend(q_ref[...], [(kb, vb, bias)]).astype(o_ref.dtype)


def _band_bias(rel_bias, q_pos, k_pos):
    rel = jnp.clip(q_pos[:, None] - k_pos[None, :], -REL_CLIP, REL_CLIP) + REL_CLIP
    return rel_bias[:, rel].astype(F32)


def _attn_prompt(qa, ka, va, rel_bias, batch, seq):
    nc = seq // CHUNK
    padw = ((0, 0), (N_PAST_CHUNKS * CHUNK, 0), (0, 0))
    kp = jnp.pad(ka.reshape(batch, seq, A_WIDTH), padw)
    vp = jnp.pad(va.reshape(batch, seq, A_WIDTH), padw)
    bias = _band_bias(rel_bias, N_PAST_CHUNKS * CHUNK + jnp.arange(CHUNK), jnp.arange(BAND))
    plen = seq + N_PAST_CHUNKS * CHUNK
    return pl.pallas_call(
        _attn_prompt_kernel,
        grid=(batch, nc),
        in_specs=[pl.BlockSpec((CHUNK, A_WIDTH), lambda b, c: (b * nc + c, 0)),
                  pl.BlockSpec((1, plen, A_WIDTH), lambda b, c: (b, 0, 0)),
                  pl.BlockSpec((1, plen, A_WIDTH), lambda b, c: (b, 0, 0)),
                  _full((A_HEADS, CHUNK, BAND))],
        out_specs=pl.BlockSpec((CHUNK, A_WIDTH), lambda b, c: (b * nc + c, 0)),
        out_shape=jax.ShapeDtypeStruct((batch * seq, A_WIDTH), BF16),
        compiler_params=_params("parallel", "arbitrary"),
        name="attn_prompt",
    )(qa, kp, vp, bias)


def _attn_sample_kernel(q_ref, kn_ref, vn_ref, kc_ref, vc_ref, bc_ref, bn_ref, o_ref):
    segs = [(kc_ref[0].astype(BF16), vc_ref[0].astype(BF16), lambda head: bc_ref[head]),
            (kn_ref[...], vn_ref[...], lambda head: bn_ref[head])]
    o_ref[...] = _attend(q_ref[...], segs).astype(o_ref.dtype)


def _attn_sample(qa, ka, va, cache_k, cache_v, rel_bias, batch, t, past_len):
    r = cache_k.shape[1]
    q_pos = past_len + jnp.arange(t)
    c_pos = past_len - r + jnp.arange(r)
    valid_c = ((c_pos[None, :] // CHUNK) <= (q_pos[:, None] // CHUNK)) & \
              ((c_pos[None, :] // CHUNK) >= (q_pos[:, None] // CHUNK) - N_PAST_CHUNKS)
    valid_n = ((q_pos[None, :] // CHUNK) <= (q_pos[:, None] // CHUNK)) & \
              ((q_pos[None, :] // CHUNK) >= (q_pos[:, None] // CHUNK) - N_PAST_CHUNKS)
    bias_c = jnp.where(valid_c[None], _band_bias(rel_bias, q_pos, c_pos), NEG)
    bias_n = jnp.where(valid_n[None], _band_bias(rel_bias, q_pos, q_pos), NEG)
    tok = pl.BlockSpec((t, A_WIDTH), lambda b: (b, 0))
    cache = pl.BlockSpec((1, r, A_WIDTH), lambda b: (b, 0, 0))
    return pl.pallas_call(
        _attn_sample_kernel,
        grid=(batch,),
        in_specs=[tok, tok, tok, cache, cache, _full((A_HEADS, t, r)), _full((A_HEADS, t, t))],
        out_specs=tok,
        out_shape=jax.ShapeDtypeStruct((batch * t, A_WIDTH), BF16),
        compiler_params=_params("parallel"),
        name="attn_sample",
    )(qa, ka, va, cache_k.reshape(batch, r, A_WIDTH), cache_v.reshape(batch, r, A_WIDTH), bias_c, bias_n)


def _gla_kernel(q_ref, k_ref, v_ref, la_ref, s0_ref, tri_ref, expand_ref, bdmask_ref, o_ref, sfin_ref, state_ref):
    tb = q_ref.shape[0]
    nb = tb // G_BLOCK

    @pl.when(pl.program_id(1) == 0)
    def _():
        state_ref[...] = s0_ref[0]

    q = q_ref[...]
    k = k_ref[...]
    v = v_ref[...]
    cum = jnp.dot(tri_ref[...], la_ref[...], precision=HIGHEST, preferred_element_type=F32)
    q3 = q.reshape(nb, G_BLOCK, G_KW)
    k3 = k.reshape(nb, G_BLOCK, G_KW)
    v3 = v.reshape(nb, G_BLOCK, G_WIDTH)
    cum3 = cum.reshape(nb, G_BLOCK, G_KW)
    row = lax.broadcasted_iota(jnp.int32, (nb, G_BLOCK, G_KW), 1)
    expand = expand_ref[...]

    intra = jnp.zeros((nb, G_BLOCK, G_WIDTH), F32)
    for s in range(G_BLOCK):
        k_s = jnp.broadcast_to(k3[:, s:s + 1, :], (nb, G_BLOCK, G_KW))
        c_s = jnp.broadcast_to(cum3[:, s:s + 1, :], (nb, G_BLOCK, G_KW))
        v_s = jnp.broadcast_to(v3[:, s:s + 1, :], (nb, G_BLOCK, G_WIDTH))
        prod = q3 * k_s * jnp.exp(jnp.where(row >= s, cum3 - c_s, -jnp.inf))
        att = jnp.dot(prod.reshape(tb, G_KW).astype(BF16), expand, preferred_element_type=F32)
        intra = intra + att.reshape(nb, G_BLOCK, G_WIDTH) * v_s
    o_ref[...] = intra.reshape(tb, G_WIDTH)

    last3 = jnp.broadcast_to(cum3[:, G_BLOCK - 1:G_BLOCK, :], (nb, G_BLOCK, G_KW))
    q_dec = (q3 * jnp.exp(cum3)).reshape(tb, G_KW).astype(BF16)
    k_dec = (k3 * jnp.exp(last3 - cum3)).reshape(tb, G_KW).astype(BF16)
    carry = jnp.exp(last3).reshape(tb, G_KW)
    vb = v.astype(BF16)
    bdmask = bdmask_ref[...]
    for j in range(nb):
        rows = slice(j * G_BLOCK, (j + 1) * G_BLOCK)
        state = state_ref[...]
        o_ref[rows, :] += lax.dot_general(q_dec[rows], state.astype(BF16), (((1,), (1,)), ((), ())),
                                          preferred_element_type=F32)
        upd = lax.dot_general(vb[rows], k_dec[rows], (((0,), (0,)), ((), ())), preferred_element_type=F32)
        state_ref[...] = carry[j * G_BLOCK:j * G_BLOCK + 1, :] * state + upd * bdmask

    @pl.when(pl.program_id(1) == pl.num_programs(1) - 1)
    def _():
        sfin_ref[0] = state_ref[...]


def _gla(qg, kg, vg, la, s0, batch, t, tile):
    nt = t // tile
    hsel = jnp.arange(G_KW) // G_DK
    vsel = jnp.arange(G_WIDTH) // G_DV
    same = hsel[:, None] == vsel[None, :]
    expand = same.astype(BF16)
    bdmask = same.T.astype(F32)
    s0_t = jnp.swapaxes(s0, 2, 3).reshape(batch, G_WIDTH, G_DK)
    s0_bd = jnp.where(same.T[None], jnp.tile(s0_t, (1, 1, G_HEADS)), 0.0)
    rr = jnp.arange(tile)
    tri = ((rr[:, None] // G_BLOCK == rr[None, :] // G_BLOCK) & (rr[None, :] <= rr[:, None])).astype(F32)
    tok = lambda w: pl.BlockSpec((tile, w), lambda b, i: (b * nt + i, 0))
    st = pl.BlockSpec((1, G_WIDTH, G_KW), lambda b, i: (b, 0, 0))
    o, s_bd = pl.pallas_call(
        _gla_kernel,
        grid=(batch, nt),
        in_specs=[tok(G_KW), tok(G_KW), tok(G_WIDTH), tok(G_KW), st,
                  _full((tile, tile)), _full((G_KW, G_WIDTH)), _full((G_WIDTH, G_KW))],
        out_specs=[tok(G_WIDTH), st],
        out_shape=[jax.ShapeDtypeStruct((batch * t, G_WIDTH), F32),
                   jax.ShapeDtypeStruct((batch, G_WIDTH, G_KW), F32)],
        scratch_shapes=[pltpu.VMEM((G_WIDTH, G_KW), F32)],
        compiler_params=_params("parallel", "arbitrary"),
        name="gla",
    )(qg, kg, vg, la, s0_bd, tri, expand, bdmask)
    s_fin = jnp.stack([s_bd[:, h * G_DV:(h + 1) * G_DV, h * G_DK:(h + 1) * G_DK] for h in range(G_HEADS)], axis=1)
    return o, jnp.swapaxes(s_fin, 2, 3)


def _mixer_out_kernel(x_ref, oa_ref, og_ref, sr_ref, gg_ref, avg_ref, wa_ref, wg_ref, y_ref):
    og = og_ref[...]
    og = og * lax.rsqrt(_seg_mean_sq(og, avg_ref[...]) + EPS) * gg_ref[...] * sr_ref[...]
    y = jnp.dot(oa_ref[...], wa_ref[...], preferred_element_type=F32)
    y = y + jnp.dot(og.astype(BF16), wg_ref[...], preferred_element_type=F32)
    y_ref[...] = x_ref[...] + y


def _mixer_out(x, oa, og, sr, g_gla_out, w_out):
    n = x.shape[0]
    tm = TOKEN_TILE
    row = lambda w: pl.BlockSpec((tm, w), lambda i: (i, 0))
    w_out = w_out.astype(BF16)
    return pl.pallas_call(
        _mixer_out_kernel,
        grid=(n // tm,),
        in_specs=[row(D_MODEL), row(A_WIDTH), row(G_WIDTH), row(G_WIDTH), _full((1, G_WIDTH)),
                  _full((G_WIDTH, G_WIDTH)), _full((A_WIDTH, D_MODEL)), _full((G_WIDTH, D_MODEL))],
        out_specs=row(D_MODEL),
        out_shape=jax.ShapeDtypeStruct((n, D_MODEL), F32),
        compiler_params=_params("parallel"),
        name="mixer_out",
    )(x, oa, og, sr, jnp.tile(g_gla_out, G_HEADS).reshape(1, -1), _seg_avg_matrix(G_WIDTH, G_DV),
      w_out[:A_WIDTH], w_out[A_WIDTH:])


def _mem_kv_kernel(m_ref, gm_ref, wk_ref, wv_ref, gk_ref, avg_ref, k_ref, v_ref):
    m = _rms(m_ref[...], gm_ref[...]).astype(BF16)
    k = jnp.dot(m, wk_ref[...], preferred_element_type=F32)
    k_ref[...] = k * lax.rsqrt(_seg_mean_sq(k, avg_ref[...]) + EPS) * gk_ref[...]
    v_ref[...] = jnp.dot(m, wv_ref[...], preferred_element_type=F32)


def _mem_kv(mem, g_mem, w_ck, w_cv, g_ck):
    n = mem.shape[0]
    tm = TOKEN_TILE
    row = pl.BlockSpec((tm, D_MODEL), lambda i: (i, 0))
    sq = _full((D_MODEL, D_MODEL))
    vec = _full((1, D_MODEL))
    return pl.pallas_call(
        _mem_kv_kernel,
        grid=(n // tm,),
        in_specs=[row, vec, sq, sq, vec, sq],
        out_specs=[row, row],
        out_shape=[jax.ShapeDtypeStruct((n, D_MODEL), F32)] * 2,
        compiler_params=_params("parallel"),
        name="mem_kv",
    )(mem, g_mem.reshape(1, -1), w_ck.astype(BF16), w_cv.astype(BF16), jnp.tile(g_ck, X_HEADS).reshape(1, -1),
      _seg_avg_matrix(D_MODEL, X_HEAD_DIM))


def _cross_kernel(x_ref, gc_ref, wq_ref, gq_ref, avg_ref, mk_ref, mv_ref, wo_ref, y_ref):
    x = x_ref[...]
    h = _rms(x, gc_ref[...]).astype(BF16)
    q = jnp.dot(h, wq_ref[...], preferred_element_type=F32)
    q = q * lax.rsqrt(_seg_mean_sq(q, avg_ref[...]) + EPS) * gq_ref[...] * (X_HEAD_DIM ** -0.5)
    q = q.astype(BF16)
    mk = mk_ref[0].astype(BF16)
    mv = mv_ref[0].astype(BF16)
    outs = []
    for head in range(X_HEADS):
        sl = slice(head * X_HEAD_DIM, (head + 1) * X_HEAD_DIM)
        s = lax.dot_general(q[:, sl], mk[:, sl], (((1,), (1,)), ((), ())), preferred_element_type=F32)
        e = jnp.exp(s - jnp.max(s, axis=-1, keepdims=True))
        p = (e * (1.0 / jnp.sum(e, axis=-1, keepdims=True))).astype(BF16)
        outs.append(jnp.dot(p, mv[:, sl], preferred_element_type=F32))
    o = jnp.concatenate(outs, axis=1).astype(BF16)
    y_ref[...] = x + jnp.dot(o, wo_ref[...], preferred_element_type=F32)


def _cross(x, mem_k, mem_v, g_cross, w_cq, g_cq, w_co, batch, t, tile):
    nt = t // tile
    row = pl.BlockSpec((tile, D_MODEL), lambda b, i: (b * nt + i, 0))
    mem = pl.BlockSpec((1, N_MEM, D_MODEL), lambda b, i: (b, 0, 0))
    sq = _full((D_MODEL, D_MODEL))
    vec = _full((1, D_MODEL))
    return pl.pallas_call(
        _cross_kernel,
        grid=(batch, nt),
        in_specs=[row, vec, sq, vec, sq, mem, mem, sq],
        out_specs=row,
        out_shape=jax.ShapeDtypeStruct((batch * t, D_MODEL), F32),
        compiler_params=_params("parallel", "parallel"),
        name="cross",
    )(x, g_cross.reshape(1, -1), w_cq.astype(BF16), jnp.tile(g_cq, X_HEADS).reshape(1, -1),
      _seg_avg_matrix(D_MODEL, X_HEAD_DIM), mem_k, mem_v, w_co.astype(BF16))


def _top_values(work, count):
    rows = lax.broadcasted_iota(jnp.int32, work.shape, 0).astype(F32)
    nrows = float(work.shape[0])
    out = []
    for _ in range(count):
        mx = jnp.max(work, axis=0, keepdims=True)
        first = jnp.min(jnp.where(work == mx, rows, nrows), axis=0, keepdims=True)
        work = jnp.where(rows == first, -jnp.inf, work)
        out.append(mx)
    return jnp.concatenate(out, axis=0)


def _route_kernel(x_ref, gf_ref, wpq_ref, sk1_ref, sk2_ref, h_ref, s1_ref, s2_ref, st_ref):
    hf = _rms(x_ref[...], gf_ref[...])
    h_ref[...] = hf.astype(BF16)
    qry = jnp.dot(hf, wpq_ref[...], precision=HIGHEST, preferred_element_type=F32)
    nt = (((1,), (1,)), ((), ()))
    for head in range(P_HEADS):
        q1 = qry[:, head * P_QDIM:head * P_QDIM + P_HALF]
        q2 = qry[:, head * P_QDIM + P_HALF:(head + 1) * P_QDIM]
        s1 = lax.dot_general(sk1_ref[...], q1, nt, precision=HIGHEST, preferred_element_type=F32)
        s2 = lax.dot_general(sk2_ref[...], q2, nt, precision=HIGHEST, preferred_element_type=F32)
        s1_ref[head] = s1
        s2_ref[head] = s2
        v1 = _top_values(s1, P_TOPK + 1)
        v2 = _top_values(s2, P_TOPK + 1)
        cand = jnp.concatenate([v1[a:a + 1, :] + v2[:P_TOPK, :] for a in range(P_TOPK)], axis=0)
        top = _top_values(cand, P_TOPK + 1)
        m1 = v1[0:1, :]
        m2 = v2[0:1, :]
        z = jnp.sum(jnp.exp(top[:P_TOPK, :] - (m1 + m2)), axis=0, keepdims=True)
        runner_up = jnp.maximum(top[P_TOPK:P_TOPK + 1, :],
                                jnp.maximum(v1[P_TOPK:P_TOPK + 1, :] + m2, m1 + v2[P_TOPK:P_TOPK + 1, :]))
        tau = 0.5 * (top[P_TOPK - 1:P_TOPK, :] + runner_up)
        st_ref[head] = jnp.concatenate([tau, m1, m2, 1.0 / z, jnp.zeros((4, top.shape[1]), F32)], axis=0)


def _route(x, g_ffn, w_pq, sub_keys1, sub_keys2):
    n = x.shape[0]
    tm = TOKEN_TILE
    keys = pl.BlockSpec((P_HEADS, P_NKEYS, tm), lambda i: (0, 0, i))
    return pl.pallas_call(
        _route_kernel,
        grid=(n // tm,),
        in_specs=[pl.BlockSpec((tm, D_MODEL), lambda i: (i, 0)), _full((1, D_MODEL)),
                  _full((D_MODEL, P_HEADS * P_QDIM)), _full((P_NKEYS, P_HALF)), _full((P_NKEYS, P_HALF))],
        out_specs=[pl.BlockSpec((tm, D_MODEL), lambda i: (i, 0)), keys, keys,
                   pl.BlockSpec((P_HEADS, 8, tm), lambda i: (0, 0, i))],
        out_shape=[jax.ShapeDtypeStruct((n, D_MODEL), BF16),
                   jax.ShapeDtypeStruct((P_HEADS, P_NKEYS, n), F32),
                   jax.ShapeDtypeStruct((P_HEADS, P_NKEYS, n), F32),
                   jax.ShapeDtypeStruct((P_HEADS, 8, n), F32)],
        compiler_params=_params("parallel"),
        name="peer_route",
    )(x, g_ffn.reshape(1, -1), w_pq, sub_keys1, sub_keys2)


def _peer_kernel(h_ref, x_ref, u_ref, vt_ref, s1_ref, s2_ref, st_ref, y_ref, acc_ref, thr_ref, c1_ref, e2_ref):
    j = pl.program_id(1)

    @pl.when(j == 0)
    def _():
        acc_ref[...] = jnp.zeros_like(acc_ref)
        for head in range(P_HEADS):
            st = st_ref[head]
            tau, m1, m2, inv_z = st[0:1, :], st[1:2, :], st[2:3, :], st[3:4, :]
            thr_ref[head] = tau - s1_ref[head]
            c1_ref[head] = jnp.exp(s1_ref[head] - m1) * inv_z
            e2_ref[head] = jnp.exp(s2_ref[head] - m2)

    a = lax.dot_general(u_ref[...], h_ref[...], (((1,), (1,)), ((), ())), preferred_element_type=F32)
    act = 0.5 * a * (1.0 + lax.erf(a * (1.0 / math.sqrt(2.0))))
    t = a.shape[1]
    gates = []
    for half in range(PEER_EXPERT_TILE // P_NKEYS):
        i1 = j * (PEER_EXPERT_TILE // P_NKEYS) + half
        g = jnp.zeros((P_NKEYS, t), F32)
        for head in range(P_HEADS):
            thr = thr_ref[head, pl.ds(i1, 1), :]
            c1 = c1_ref[head, pl.ds(i1, 1), :]
            g = g + jnp.where(s2_ref[head] >= thr, e2_ref[head], 0.0) * c1
        gates.append(g)
    ga = (act * jnp.concatenate(gates, axis=0)).astype(BF16)
    acc_ref[...] += jnp.dot(vt_ref[...], ga, preferred_element_type=F32)

    @pl.when(j == pl.num_programs(1) - 1)
    def _():
        y_ref[...] = x_ref[...] + jnp.transpose(acc_ref[...])


def _peer(h, x, s1t, s2t, stats, expert_u, expert_v):
    n = x.shape[0]
    tt = PEER_TOKEN_TILE
    et = PEER_EXPERT_TILE
    ne = expert_u.shape[0] // et
    u = expert_u.astype(BF16)
    vt = jnp.transpose(expert_v).astype(BF16)
    keys = pl.BlockSpec((P_HEADS, P_NKEYS, tt), lambda i, j: (0, 0, i))
    return pl.pallas_call(
        _peer_kernel,
        grid=(n // tt, ne),
        in_specs=[pl.BlockSpec((tt, D_MODEL), lambda i, j: (i, 0)),
                  pl.BlockSpec((tt, D_MODEL), lambda i, j: (i, 0)),
                  pl.BlockSpec((et, D_MODEL), lambda i, j: (j, 0)),
                  pl.BlockSpec((D_MODEL, et), lambda i, j: (0, j)),
                  keys, keys, pl.BlockSpec((P_HEADS, 8, tt), lambda i, j: (0, 0, i))],
        out_specs=pl.BlockSpec((tt, D_MODEL), lambda i, j: (i, 0)),
        out_shape=jax.ShapeDtypeStruct((n, D_MODEL), F32),
        scratch_shapes=[pltpu.VMEM((D_MODEL, tt), F32)] + [pltpu.VMEM((P_HEADS, P_NKEYS, tt), F32)] * 3,
        compiler_params=_params("parallel", "arbitrary"),
        name="peer_experts",
    )(h, x, u, vt, s1t, s2t, stats)


def _layer(xp, xs, cache_k, cache_v, state, cmem_k, cmem_v, mem_prompt, past_len,
           g_mix, w_in, g_q, g_k, rel_bias, w_gate2, b_gate2, g_gla_out, w_out,
           g_cross, g_mem, w_cq, w_ck, w_cv, g_cq, g_ck, w_co, g_ffn, w_pq, sub_keys1, sub_keys2, expert_u, expert_v):
    bp, sp, _ = xp.shape
    bs, ts, _ = xs.shape
    n_p = bp * sp
    x = jnp.concatenate([xp.reshape(n_p, D_MODEL), xs.reshape(bs * ts, D_MODEL)], axis=0)

    qa, ka_f, ka_b, va_f, va_b, qg, kg, vg, la, sr = _mixer_in(x, g_mix, w_in, g_q, g_k, w_gate2, b_gate2)
    oa_p = _attn_prompt(qa[:n_p], ka_b[:n_p], va_b[:n_p], rel_bias, bp, sp)
    oa_s = _attn_sample(qa[n_p:], ka_b[n_p:], va_b[n_p:], cache_k, cache_v, rel_bias, bs, ts, past_len)
    og_p, gla_p = _gla(qg[:n_p], kg[:n_p], vg[:n_p], la[:n_p], jnp.zeros((bp, G_HEADS, G_DK, G_DV), F32),
                       bp, sp, GLA_TILE)
    og_s, gla_s = _gla(qg[n_p:], kg[n_p:], vg[n_p:], la[n_p:], state, bs, ts, ts)
    x = _mixer_out(x, jnp.concatenate([oa_p, oa_s], axis=0), jnp.concatenate([og_p, og_s], axis=0),
                   sr, g_gla_out, w_out)

    mk, mv = _mem_kv(mem_prompt.reshape(bp * N_MEM, D_MODEL), g_mem, w_ck, w_cv, g_ck)
    x_p = _cross(x[:n_p], mk.reshape(bp, N_MEM, D_MODEL), mv.reshape(bp, N_MEM, D_MODEL),
                 g_cross, w_cq, g_cq, w_co, bp, sp, TOKEN_TILE)
    x_s = _cross(x[n_p:], cmem_k.reshape(bs, N_MEM, D_MODEL), cmem_v.reshape(bs, N_MEM, D_MODEL),
                 g_cross, w_cq, g_cq, w_co, bs, ts, ts)
    x = jnp.concatenate([x_p, x_s], axis=0)

    h, s1t, s2t, stats = _route(x, g_ffn, w_pq, sub_keys1, sub_keys2)
    x = _peer(h, x, s1t, s2t, stats, expert_u, expert_v)

    keep = min(BAND, sp)
    heads = lambda a, b, t: a.reshape(b, t, A_HEADS, A_HEAD_DIM)
    return (x[:n_p].reshape(bp, sp, D_MODEL), x[n_p:].reshape(bs, ts, D_MODEL),
            heads(ka_f[:n_p], bp, sp)[:, -keep:], heads(va_f[:n_p], bp, sp)[:, -keep:], gla_p,
            mk.reshape(bp, N_MEM, X_HEADS, X_HEAD_DIM), mv.reshape(bp, N_MEM, X_HEADS, X_HEAD_DIM),
            heads(ka_f[n_p:], bs, ts), heads(va_f[n_p:], bs, ts), gla_s)


def kernel(x_prompt, x_sample, cache_att_k, cache_att_v, state_gla, cache_mem_k, cache_mem_v, mem_prompt, g_mix, w_in, g_q, g_k, rel_bias, w_gate2, b_gate2, g_gla_out, w_out, g_cross, g_mem, w_cq, w_ck, w_cv, g_cq, g_ck, w_co, g_ffn, w_pq, sub_keys1, sub_keys2, expert_u, expert_v):
    depth = w_in.shape[0]
    past_len = 4096
    xp, xs = x_prompt, x_sample
    per_layer = []
    for l in range(depth):
        outs = _layer(xp, xs, cache_att_k[l], cache_att_v[l], state_gla[l], cache_mem_k[l], cache_mem_v[l],
                      mem_prompt, past_len,
                      g_mix[l], w_in[l], g_q[l], g_k[l], rel_bias[l], w_gate2[l], b_gate2[l], g_gla_out[l], w_out[l],
                      g_cross[l], g_mem[l], w_cq[l], w_ck[l], w_cv[l], g_cq[l], g_ck[l], w_co[l],
                      g_ffn[l], w_pq[l], sub_keys1[l], sub_keys2[l], expert_u[l], expert_v[l])
        xp, xs = outs[0], outs[1]
        per_layer.append(outs[2:])
    stacked = [jnp.stack([p[i] for p in per_layer]) for i in range(8)]
    return (xp, xs) + tuple(stacked)
```

```python
import functools
import math

import jax
import jax.numpy as jnp
from jax import lax
from jax.experimental import pallas as pl
from jax.experimental.pallas import tpu as pltpu

F32 = jnp.float32
BF16 = jnp.bfloat16
HIGHEST = lax.Precision.HIGHEST

D_MODEL = 1024
CHUNK = 64
N_PAST_CHUNKS = 8
BAND = (N_PAST_CHUNKS + 1) * CHUNK
A_HEADS = 8
A_HEAD_DIM = 64
A_WIDTH = A_HEADS * A_HEAD_DIM
REL_CLIP = 128
G_HEADS = 4
G_DK = 64
G_DV = 128
G_KW = G_HEADS * G_DK
G_WIDTH = G_HEADS * G_DV
G_GATE_RANK = 16
G_GATE_NORM = 16.0
G_BLOCK = 16
N_MEM = 256
X_HEADS = 4
X_HEAD_DIM = D_MODEL // X_HEADS
P_HEADS = 8
P_NKEYS = 128
P_QDIM = 256
P_HALF = P_QDIM // 2
P_TOPK = 16
EPS = 1e-6
NEG = -1e30

LANES = 128
VMEM_LIMIT = 56 * 1024 * 1024

TOKEN_TILE = 256
GLA_TILE = 256
PEER_TOKEN_TILE = 1024
PEER_EXPERT_TILE = 256


def _params(*sem):
    return pltpu.CompilerParams(dimension_semantics=sem, vmem_limit_bytes=VMEM_LIMIT)


def _full(shape):
    return pl.BlockSpec(shape, lambda *_: (0,) * len(shape))


def _rms(x, g):
    return x * lax.rsqrt(jnp.mean(x * x, axis=-1, keepdims=True) + EPS) * g


def _seg_mean_sq(z, seg_avg):
    sq = z * z
    hi = sq.astype(BF16)
    lo = (sq - hi.astype(F32)).astype(BF16)
    return (jnp.dot(hi, seg_avg, preferred_element_type=F32)
            + jnp.dot(lo, seg_avg, preferred_element_type=F32))


def _seg_avg_matrix(width, seg):
    r = jnp.arange(width) // seg
    return jnp.where(r[:, None] == r[None, :], 1.0 / seg, 0.0).astype(BF16)


def _mixer_in_kernel(x_ref, gmix_ref, w_ref, gq_ref, gk_ref, avg_ref, wg2_ref, bg2_ref,
                     qa_ref, kaf_ref, kab_ref, vaf_ref, vab_ref, qg_ref, kg_ref, vg_ref, la_ref, sr_ref):
    h = _rms(x_ref[...], gmix_ref[...]).astype(BF16)
    z = jnp.dot(h, w_ref[...], preferred_element_type=F32)
    o = 0
    qa = z[:, o:o + A_WIDTH]; o += A_WIDTH
    ka = z[:, o:o + A_WIDTH]; o += A_WIDTH
    va = z[:, o:o + A_WIDTH]; o += A_WIDTH
    qg = z[:, o:o + G_KW]; o += G_KW
    kg = z[:, o:o + G_KW]; o += G_KW
    vg = z[:, o:o + G_WIDTH]; o += G_WIDTH
    r = z[:, o:o + G_WIDTH]; o += G_WIDTH
    gl = z[:, o:o + LANES]
    avg = avg_ref[...]
    qa = qa * lax.rsqrt(_seg_mean_sq(qa, avg) + EPS) * gq_ref[...]
    ka = ka * lax.rsqrt(_seg_mean_sq(ka, avg) + EPS) * gk_ref[...]
    qa_ref[...] = (qa * (A_HEAD_DIM ** -0.5)).astype(BF16)
    kaf_ref[...] = ka
    kab_ref[...] = ka.astype(BF16)
    vaf_ref[...] = va
    vab_ref[...] = va.astype(BF16)
    qg_ref[...] = qg * (G_DK ** -0.5)
    kg_ref[...] = kg
    vg_ref[...] = vg
    pre = jnp.dot(gl, wg2_ref[...], precision=HIGHEST, preferred_element_type=F32) + bg2_ref[...]
    la_ref[...] = (jnp.minimum(pre, 0.0) - jnp.log1p(jnp.exp(-jnp.abs(pre)))) / G_GATE_NORM
    sr_ref[...] = r * jax.nn.sigmoid(r)


def _mixer_in(x, g_mix, w_in, g_q, g_k, w_gate2, b_gate2):
    n = x.shape[0]
    c_gl = 3 * A_WIDTH + 2 * G_KW + G_WIDTH
    w_cat = jnp.concatenate(
        [w_in[:, :c_gl], w_in[:, c_gl + G_GATE_RANK:],
         jnp.pad(w_in[:, c_gl:c_gl + G_GATE_RANK], ((0, 0), (0, LANES - G_GATE_RANK)))], axis=1).astype(BF16)
    wg2 = jnp.pad(w_gate2, ((0, LANES - G_GATE_RANK), (0, 0)))
    wcols = w_cat.shape[1]
    tm = TOKEN_TILE
    row = lambda w: pl.BlockSpec((tm, w), lambda i: (i, 0))
    outs = [(A_WIDTH, BF16), (A_WIDTH, F32), (A_WIDTH, BF16), (A_WIDTH, F32), (A_WIDTH, BF16),
            (G_KW, F32), (G_KW, F32), (G_WIDTH, F32), (G_KW, F32), (G_WIDTH, F32)]
    return pl.pallas_call(
        _mixer_in_kernel,
        grid=(n // tm,),
        in_specs=[row(D_MODEL), _full((1, D_MODEL)), _full((D_MODEL, wcols)), _full((1, A_WIDTH)),
                  _full((1, A_WIDTH)), _full((A_WIDTH, A_WIDTH)), _full((LANES, G_KW)), _full((1, G_KW))],
        out_specs=[row(w) for w, _ in outs],
        out_shape=[jax.ShapeDtypeStruct((n, w), dt) for w, dt in outs],
        compiler_params=_params("parallel"),
        name="mixer_in",
    )(x, g_mix.reshape(1, -1), w_cat, jnp.tile(g_q, A_HEADS).reshape(1, -1), jnp.tile(g_k, A_HEADS).reshape(1, -1),
      _seg_avg_matrix(A_WIDTH, A_HEAD_DIM), wg2, b_gate2.reshape(1, -1))


def _attend(q, segments):
    tq = q.shape[0]
    lane = lax.broadcasted_iota(jnp.int32, (tq, LANES), 1)
    outs = []
    for pair in range(A_HEADS // 2):
        sl = slice(pair * LANES, (pair + 1) * LANES)
        q2 = q[:, sl]
        halves = []
        for half in range(2):
            head = 2 * pair + half
            mine = (lane < A_HEAD_DIM) if half == 0 else (lane >= A_HEAD_DIM)
            qm = jnp.where(mine, q2, jnp.zeros_like(q2))
            scores = [lax.dot_general(qm, k[:, sl], (((1,), (1,)), ((), ())), preferred_element_type=F32) + bias(head)
                      for k, _, bias in segments]
            m = functools.reduce(jnp.maximum, [jnp.max(s, axis=-1, keepdims=True) for s in scores])
            es = [jnp.exp(s - m) for s in scores]
            inv = 1.0 / functools.reduce(jnp.add, [jnp.sum(e, axis=-1, keepdims=True) for e in es])
            o = functools.reduce(jnp.add, [
                jnp.dot((e * inv).astype(BF16), v[:, sl], preferred_element_type=F32)
                for e, (_, v, _) in zip(es, segments)])
            halves.append(o)
        outs.append(jnp.where(lane < A_HEAD_DIM, halves[0], halves[1]))
    return jnp.concatenate(outs, axis=1)


BIAS_WRAP = 640


def _attn_prompt_kernel(q_ref, k_ref, v_ref, tb_ref, o_ref):
    c = pl.program_id(1)
    first = jnp.maximum(c - N_PAST_CHUNKS, 0)
    start = pl.multiple_of(first * CHUNK, CHUNK)
    off = (c - first) * CHUNK
    kb = k_ref[0, pl.ds(start, BAND), :]
    vb = v_ref[0, pl.ds(start, BAND), :]
    valid = lax.broadcasted_iota(jnp.int32, (CHUNK, BAND), 1) < off + CHUNK

    def bias(head):
        table = jnp.broadcast_to(tb_ref[head:head + 1, :], (CHUNK, BIAS_WRAP))
        rolled = pltpu.roll(table, off, 1, stride=1, stride_axis=0)
        return jnp.where(valid, rolled[:, :BAND], NEG)

    o_ref[...] = _attend(q_ref[...], [(kb, vb, bias)]).astype(o_ref.dtype)


def _band_bias(rel_bias, q_pos, k_pos):
    rel = jnp.clip(q_pos[:, None] - k_pos[None, :], -REL_CLIP, REL_CLIP) + REL_CLIP
    return rel_bias[:, rel].astype(F32)


def _wrapped_bias_table(rel_bias):
    u = jnp.arange(BIAS_WRAP)
    dist = jnp.where(u < CHUNK, -u, BIAS_WRAP - u)
    return rel_bias[:, jnp.clip(dist, -REL_CLIP, REL_CLIP) + REL_CLIP].astype(F32)


def _attn_prompt(qa, ka, va, rel_bias, batch, seq):
    nc = seq // CHUNK
    assert seq >= BAND
    kv = pl.BlockSpec((1, seq, A_WIDTH), lambda b, c: (b, 0, 0))
    return pl.pallas_call(
        _attn_prompt_kernel,
        grid=(batch, nc),
        in_specs=[pl.BlockSpec((CHUNK, A_WIDTH), lambda b, c: (b * nc + c, 0)), kv, kv,
                  _full((A_HEADS, BIAS_WRAP))],
        out_specs=pl.BlockSpec((CHUNK, A_WIDTH), lambda b, c: (b * nc + c, 0)),
        out_shape=jax.ShapeDtypeStruct((batch * seq, A_WIDTH), BF16),
        compiler_params=_params("parallel", "arbitrary"),
        name="attn_prompt",
    )(qa, ka.reshape(batch, seq, A_WIDTH), va.reshape(batch, seq, A_WIDTH), _wrapped_bias_table(rel_bias))


def _attn_sample_kernel(q_ref, kn_ref, vn_ref, kc_ref, vc_ref, bc_ref, bn_ref, o_ref):
    segs = [(kc_ref[0].astype(BF16), vc_ref[0].astype(BF16), lambda head: bc_ref[head]),
            (kn_ref[...], vn_ref[...], lambda head: bn_ref[head])]
    o_ref[...] = _attend(q_ref[...], segs).astype(o_ref.dtype)


def _attn_sample(qa, ka, va, cache_k, cache_v, rel_bias, batch, t, past_len):
    r = cache_k.shape[1]
    q_pos = past_len + jnp.arange(t)
    c_pos = past_len - r + jnp.arange(r)
    valid_c = ((c_pos[None, :] // CHUNK) <= (q_pos[:, None] // CHUNK)) & \
              ((c_pos[None, :] // CHUNK) >= (q_pos[:, None] // CHUNK) - N_PAST_CHUNKS)
    valid_n = ((q_pos[None, :] // CHUNK) <= (q_pos[:, None] // CHUNK)) & \
              ((q_pos[None, :] // CHUNK) >= (q_pos[:, None] // CHUNK) - N_PAST_CHUNKS)
    bias_c = jnp.where(valid_c[None], _band_bias(rel_bias, q_pos, c_pos), NEG)
    bias_n = jnp.where(valid_n[None], _band_bias(rel_bias, q_pos, q_pos), NEG)
    tok = pl.BlockSpec((t, A_WIDTH), lambda b: (b, 0))
    cache = pl.BlockSpec((1, r, A_WIDTH), lambda b: (b, 0, 0))
    return pl.pallas_call(
        _attn_sample_kernel,
        grid=(batch,),
        in_specs=[tok, tok, tok, cache, cache, _full((A_HEADS, t, r)), _full((A_HEADS, t, t))],
        out_specs=tok,
        out_shape=jax.ShapeDtypeStruct((batch * t, A_WIDTH), BF16),
        compiler_params=_params("parallel"),
        name="attn_sample",
    )(qa, ka, va, cache_k.reshape(batch, r, A_WIDTH), cache_v.reshape(batch, r, A_WIDTH), bias_c, bias_n)


def _gla_kernel(q_ref, k_ref, v_ref, la_ref, s0_ref, tri_ref, expand_ref, bdmask_ref, o_ref, sfin_ref, state_ref):
    tb = q_ref.shape[0]
    nb = tb // G_BLOCK

    @pl.when(pl.program_id(1) == 0)
    def _():
        state_ref[...] = s0_ref[0]

    q = q_ref[...]
    k = k_ref[...]
    v = v_ref[...]
    cum = jnp.dot(tri_ref[...], la_ref[...], precision=HIGHEST, preferred_element_type=F32)
    q3 = q.reshape(nb, G_BLOCK, G_KW)
    k3 = k.reshape(nb, G_BLOCK, G_KW)
    v3 = v.reshape(nb, G_BLOCK, G_WIDTH)
    cum3 = cum.reshape(nb, G_BLOCK, G_KW)
    row = lax.broadcasted_iota(jnp.int32, (nb, G_BLOCK, G_KW), 1)
    expand = expand_ref[...]

    intra = jnp.zeros((nb, G_BLOCK, G_WIDTH), F32)
    for s in range(G_BLOCK):
        k_s = jnp.broadcast_to(k3[:, s:s + 1, :], (nb, G_BLOCK, G_KW))
        c_s = jnp.broadcast_to(cum3[:, s:s + 1, :], (nb, G_BLOCK, G_KW))
        v_s = jnp.broadcast_to(v3[:, s:s + 1, :], (nb, G_BLOCK, G_WIDTH))
        prod = q3 * k_s * jnp.exp(jnp.where(row >= s, cum3 - c_s, -jnp.inf))
        att = jnp.dot(prod.reshape(tb, G_KW).astype(BF16), expand, preferred_element_type=F32)
        intra = intra + att.reshape(nb, G_BLOCK, G_WIDTH) * v_s
    o_ref[...] = intra.reshape(tb, G_WIDTH)

    last3 = jnp.broadcast_to(cum3[:, G_BLOCK - 1:G_BLOCK, :], (nb, G_BLOCK, G_KW))
    q_dec = (q3 * jnp.exp(cum3)).reshape(tb, G_KW).astype(BF16)
    k_dec = (k3 * jnp.exp(last3 - cum3)).reshape(tb, G_KW).astype(BF16)
    carry = jnp.exp(last3).reshape(tb, G_KW)
    vb = v.astype(BF16)
    bdmask = bdmask_ref[...]
    for j in range(nb):
        rows = slice(j * G_BLOCK, (j + 1) * G_BLOCK)
        state = state_ref[...]
        o_ref[rows, :] += lax.dot_general(q_dec[rows], state.astype(BF16), (((1,), (1,)), ((), ())),
                                          preferred_element_type=F32)
        upd = lax.dot_general(vb[rows], k_dec[rows], (((0,), (0,)), ((), ())), preferred_element_type=F32)
        state_ref[...] = carry[j * G_BLOCK:j * G_BLOCK + 1, :] * state + upd * bdmask

    @pl.when(pl.program_id(1) == pl.num_programs(1) - 1)
    def _():
        sfin_ref[0] = state_ref[...]


def _gla(qg, kg, vg, la, s0, batch, t, tile):
    nt = t // tile
    hsel = jnp.arange(G_KW) // G_DK
    vsel = jnp.arange(G_WIDTH) // G_DV
    same = hsel[:, None] == vsel[None, :]
    expand = same.astype(BF16)
    bdmask = same.T.astype(F32)
    s0_t = jnp.swapaxes(s0, 2, 3).reshape(batch, G_WIDTH, G_DK)
    s0_bd = jnp.where(same.T[None], jnp.tile(s0_t, (1, 1, G_HEADS)), 0.0)
    rr = jnp.arange(tile)
    tri = ((rr[:, None] // G_BLOCK == rr[None, :] // G_BLOCK) & (rr[None, :] <= rr[:, None])).astype(F32)
    tok = lambda w: pl.BlockSpec((tile, w), lambda b, i: (b * nt + i, 0))
    st = pl.BlockSpec((1, G_WIDTH, G_KW), lambda b, i: (b, 0, 0))
    o, s_bd = pl.pallas_call(
        _gla_kernel,
        grid=(batch, nt),
        in_specs=[tok(G_KW), tok(G_KW), tok(G_WIDTH), tok(G_KW), st,
                  _full((tile, tile)), _full((G_KW, G_WIDTH)), _full((G_WIDTH, G_KW))],
        out_specs=[tok(G_WIDTH), st],
        out_shape=[jax.ShapeDtypeStruct((batch * t, G_WIDTH), F32),
                   jax.ShapeDtypeStruct((batch, G_WIDTH, G_KW), F32)],
        scratch_shapes=[pltpu.VMEM((G_WIDTH, G_KW), F32)],
        compiler_params=_params("parallel", "arbitrary"),
        name="gla",
    )(qg, kg, vg, la, s0_bd, tri, expand, bdmask)
    s_fin = jnp.stack([s_bd[:, h * G_DV:(h + 1) * G_DV, h * G_DK:(h + 1) * G_DK] for h in range(G_HEADS)], axis=1)
    return o, jnp.swapaxes(s_fin, 2, 3)


def _mixer_out_kernel(x_ref, oa_ref, og_ref, sr_ref, gg_ref, avg_ref, wa_ref, wg_ref, y_ref):
    og = og_ref[...]
    og = og * lax.rsqrt(_seg_mean_sq(og, avg_ref[...]) + EPS) * gg_ref[...] * sr_ref[...]
    y = jnp.dot(oa_ref[...], wa_ref[...], preferred_element_type=F32)
    y = y + jnp.dot(og.astype(BF16), wg_ref[...], preferred_element_type=F32)
    y_ref[...] = x_ref[...] + y


def _mixer_out(x, oa, og, sr, g_gla_out, w_out):
    n = x.shape[0]
    tm = TOKEN_TILE
    row = lambda w: pl.BlockSpec((tm, w), lambda i: (i, 0))
    w_out = w_out.astype(BF16)
    return pl.pallas_call(
        _mixer_out_kernel,
        grid=(n // tm,),
        in_specs=[row(D_MODEL), row(A_WIDTH), row(G_WIDTH), row(G_WIDTH), _full((1, G_WIDTH)),
                  _full((G_WIDTH, G_WIDTH)), _full((A_WIDTH, D_MODEL)), _full((G_WIDTH, D_MODEL))],
        out_specs=row(D_MODEL),
        out_shape=jax.ShapeDtypeStruct((n, D_MODEL), F32),
        compiler_params=_params("parallel"),
        name="mixer_out",
    )(x, oa, og, sr, jnp.tile(g_gla_out, G_HEADS).reshape(1, -1), _seg_avg_matrix(G_WIDTH, G_DV),
      w_out[:A_WIDTH], w_out[A_WIDTH:])


def _mem_kv_kernel(m_ref, gm_ref, wk_ref, wv_ref, gk_ref, avg_ref, k_ref, v_ref):
    m = _rms(m_ref[...], gm_ref[...]).astype(BF16)
    k = jnp.dot(m, wk_ref[...], preferred_element_type=F32)
    k_ref[...] = k * lax.rsqrt(_seg_mean_sq(k, avg_ref[...]) + EPS) * gk_ref[...]
    v_ref[...] = jnp.dot(m, wv_ref[...], preferred_element_type=F32)


def _mem_kv(mem, g_mem, w_ck, w_cv, g_ck):
    n = mem.shape[0]
    tm = TOKEN_TILE
    row = pl.BlockSpec((tm, D_MODEL), lambda i: (i, 0))
    sq = _full((D_MODEL, D_MODEL))
    vec = _full((1, D_MODEL))
    return pl.pallas_call(
        _mem_kv_kernel,
        grid=(n // tm,),
        in_specs=[row, vec, sq, sq, vec, sq],
        out_specs=[row, row],
        out_shape=[jax.ShapeDtypeStruct((n, D_MODEL), F32)] * 2,
        compiler_params=_params("parallel"),
        name="mem_kv",
    )(mem, g_mem.reshape(1, -1), w_ck.astype(BF16), w_cv.astype(BF16), jnp.tile(g_ck, X_HEADS).reshape(1, -1),
      _seg_avg_matrix(D_MODEL, X_HEAD_DIM))


def _cross_kernel(x_ref, gc_ref, wq_ref, gq_ref, avg_ref, mk_ref, mv_ref, wo_ref, y_ref):
    x = x_ref[...]
    h = _rms(x, gc_ref[...]).astype(BF16)
    q = jnp.dot(h, wq_ref[...], preferred_element_type=F32)
    q = q * lax.rsqrt(_seg_mean_sq(q, avg_ref[...]) + EPS) * gq_ref[...] * (X_HEAD_DIM ** -0.5)
    q = q.astype(BF16)
    mk = mk_ref[0].astype(BF16)
    mv = mv_ref[0].astype(BF16)
    outs = []
    for head in range(X_HEADS):
        sl = slice(head * X_HEAD_DIM, (head + 1) * X_HEAD_DIM)
        s = lax.dot_general(q[:, sl], mk[:, sl], (((1,), (1,)), ((), ())), preferred_element_type=F32)
        e = jnp.exp(s - jnp.max(s, axis=-1, keepdims=True))
        p = (e * (1.0 / jnp.sum(e, axis=-1, keepdims=True))).astype(BF16)
        outs.append(jnp.dot(p, mv[:, sl], preferred_element_type=F32))
    o = jnp.concatenate(outs, axis=1).astype(BF16)
    y_ref[...] = x + jnp.dot(o, wo_ref[...], preferred_element_type=F32)


def _cross(x, mem_k, mem_v, g_cross, w_cq, g_cq, w_co, batch, t, tile):
    nt = t // tile
    row = pl.BlockSpec((tile, D_MODEL), lambda b, i: (b * nt + i, 0))
    mem = pl.BlockSpec((1, N_MEM, D_MODEL), lambda b, i: (b, 0, 0))
    sq = _full((D_MODEL, D_MODEL))
    vec = _full((1, D_MODEL))
    return pl.pallas_call(
        _cross_kernel,
        grid=(batch, nt),
        in_specs=[row, vec, sq, vec, sq, mem, mem, sq],
        out_specs=row,
        out_shape=jax.ShapeDtypeStruct((batch * t, D_MODEL), F32),
        compiler_params=_params("parallel", "parallel"),
        name="cross",
    )(x, g_cross.reshape(1, -1), w_cq.astype(BF16), jnp.tile(g_cq, X_HEADS).reshape(1, -1),
      _seg_avg_matrix(D_MODEL, X_HEAD_DIM), mem_k, mem_v, w_co.astype(BF16))


def _top_values(work, count, want_rank=False):
    rows = lax.broadcasted_iota(jnp.int32, work.shape, 0).astype(F32)
    nrows = float(work.shape[0])
    rank = jnp.full(work.shape, float(count), F32)
    out = []
    for r in range(count):
        mx = jnp.max(work, axis=0, keepdims=True)
        first = jnp.min(jnp.where(work == mx, rows, nrows), axis=0, keepdims=True)
        kill = rows == first
        work = jnp.where(kill, -jnp.inf, work)
        if want_rank:
            rank = jnp.where(kill, float(r), rank)
        out.append(mx)
    vals = jnp.concatenate(out, axis=0)
    return (vals, rank) if want_rank else vals


def _route_kernel(x_ref, gf_ref, wpq_ref, sk1_ref, sk2_ref, h_ref, r2_ref, e2_ref, n1_ref, c1_ref):
    hf = _rms(x_ref[...], gf_ref[...])
    h_ref[...] = hf.astype(BF16)
    qry = jnp.dot(hf, wpq_ref[...], precision=HIGHEST, preferred_element_type=F32)
    nt = (((1,), (1,)), ((), ()))
    nk = P_TOPK + 1
    t = qry.shape[0]
    for head in range(P_HEADS):
        q1 = qry[:, head * P_QDIM:head * P_QDIM + P_HALF]
        q2 = qry[:, head * P_QDIM + P_HALF:(head + 1) * P_QDIM]
        s1 = lax.dot_general(sk1_ref[...], q1, nt, precision=HIGHEST, preferred_element_type=F32)
        s2 = lax.dot_general(sk2_ref[...], q2, nt, precision=HIGHEST, preferred_element_type=F32)
        v1 = _top_values(s1, nk)
        v2, rank2 = _top_values(s2, nk, want_rank=True)
        slabs = [v1[a:a + 1, :] + v2[:nk // (a + 1), :] for a in range(nk)]
        pad = (-sum(sl.shape[0] for sl in slabs)) % 8
        cand = jnp.concatenate(slabs + [jnp.full((pad, t), -jnp.inf, F32)], axis=0)
        top = _top_values(cand, nk)
        m1 = v1[0:1, :]
        m2 = v2[0:1, :]
        z = jnp.sum(jnp.exp(top[:P_TOPK, :] - (m1 + m2)), axis=0, keepdims=True)
        tau = 0.5 * (top[P_TOPK - 1:P_TOPK, :] + top[P_TOPK:nk, :])
        thr = tau - s1
        n1 = jnp.zeros_like(s1)
        for b in range(P_TOPK):
            n1 = n1 + jnp.where(v2[b:b + 1, :] >= thr, 1.0, 0.0)
        r2_ref[head] = rank2.astype(BF16)
        e2_ref[head] = jnp.exp(s2 - m2).astype(BF16)
        n1_ref[head] = n1
        c1_ref[head] = jnp.exp(s1 - m1) * (1.0 / z)


def _route(x, g_ffn, w_pq, sub_keys1, sub_keys2):
    n = x.shape[0]
    tm = TOKEN_TILE
    keys = pl.BlockSpec((P_HEADS, P_NKEYS, tm), lambda i: (0, 0, i))
    tab = lambda dt: jax.ShapeDtypeStruct((P_HEADS, P_NKEYS, n), dt)
    return pl.pallas_call(
        _route_kernel,
        grid=(n // tm,),
        in_specs=[pl.BlockSpec((tm, D_MODEL), lambda i: (i, 0)), _full((1, D_MODEL)),
                  _full((D_MODEL, P_HEADS * P_QDIM)), _full((P_NKEYS, P_HALF)), _full((P_NKEYS, P_HALF))],
        out_specs=[pl.BlockSpec((tm, D_MODEL), lambda i: (i, 0)), keys, keys, keys, keys],
        out_shape=[jax.ShapeDtypeStruct((n, D_MODEL), BF16), tab(BF16), tab(BF16), tab(F32), tab(F32)],
        compiler_params=_params("parallel"),
        name="peer_route",
    )(x, g_ffn.reshape(1, -1), w_pq, sub_keys1, sub_keys2)


def _peer_kernel(h_ref, x_ref, u_ref, vt_ref, r2_ref, e2_ref, n1_ref, c1_ref, y_ref, acc_ref):
    j = pl.program_id(1)

    @pl.when(j == 0)
    def _():
        acc_ref[...] = jnp.zeros_like(acc_ref)

    a = lax.dot_general(u_ref[...], h_ref[...], (((1,), (1,)), ((), ())), preferred_element_type=F32)
    act = (0.5 * a * (1.0 + lax.erf(a * (1.0 / math.sqrt(2.0))))).astype(BF16)
    t = a.shape[1]
    gates = []
    for half in range(PEER_EXPERT_TILE // P_NKEYS):
        i1 = j * (PEER_EXPERT_TILE // P_NKEYS) + half
        g = jnp.zeros((P_NKEYS, t), BF16)
        for head in range(P_HEADS):
            n1 = n1_ref[head, pl.ds(i1, 1), :].astype(BF16)
            c1 = c1_ref[head, pl.ds(i1, 1), :].astype(BF16)
            g = g + jnp.where(r2_ref[head] < n1, e2_ref[head], jnp.zeros((), BF16)) * c1
        gates.append(g)
    ga = act * jnp.concatenate(gates, axis=0)
    acc_ref[...] += jnp.dot(vt_ref[0], ga, preferred_element_type=F32)

    @pl.when(j == pl.num_programs(1) - 1)
    def _():
        y_ref[...] = x_ref[...] + jnp.transpose(acc_ref[...])


def _peer(h, x, r2, e2, n1, c1, u, vt, tt):
    n = x.shape[0]
    et = PEER_EXPERT_TILE
    ne = u.shape[0] // et
    keys = pl.BlockSpec((P_HEADS, P_NKEYS, tt), lambda i, j: (0, 0, i))
    once = pl.Buffered(1)
    return pl.pallas_call(
        _peer_kernel,
        grid=(n // tt, ne),
        in_specs=[pl.BlockSpec((tt, D_MODEL), lambda i, j: (i, 0)),
                  pl.BlockSpec((tt, D_MODEL), lambda i, j: (i, 0), pipeline_mode=once),
                  pl.BlockSpec((et, D_MODEL), lambda i, j: (j, 0)),
                  pl.BlockSpec((1, D_MODEL, et), lambda i, j: (j, 0, 0)),
                  keys, keys, keys, keys],
        out_specs=pl.BlockSpec((tt, D_MODEL), lambda i, j: (i, 0), pipeline_mode=once),
        out_shape=jax.ShapeDtypeStruct((n, D_MODEL), F32),
        scratch_shapes=[pltpu.VMEM((D_MODEL, tt), F32)],
        compiler_params=_params("parallel", "arbitrary"),
        name="peer_experts",
    )(h, x, u, vt, r2, e2, n1, c1)


def _layer(xp, xs, cache_k, cache_v, state, cmem_k, cmem_v, mem_prompt, past_len,
           g_mix, w_in, g_q, g_k, rel_bias, w_gate2, b_gate2, g_gla_out, w_out,
           g_cross, g_mem, w_cq, w_ck, w_cv, g_cq, g_ck, w_co, g_ffn, w_pq, sub_keys1, sub_keys2, expert_u, expert_v):
    bp, sp, _ = xp.shape
    bs, ts, _ = xs.shape
    xp = xp.reshape(bp * sp, D_MODEL)
    xs = xs.reshape(bs * ts, D_MODEL)
    u = expert_u.astype(BF16)
    ne = expert_v.shape[0] // PEER_EXPERT_TILE
    vt = jnp.swapaxes(expert_v.reshape(ne, PEER_EXPERT_TILE, D_MODEL), 1, 2).astype(BF16)
    heads = lambda a, b, t: a.reshape(b, t, A_HEADS, A_HEAD_DIM)

    def tail(x, mem_k, mem_v, batch, t, cross_tile, peer_tile):
        x = _cross(x, mem_k, mem_v, g_cross, w_cq, g_cq, w_co, batch, t, cross_tile)
        h, r2, e2, n1, c1 = _route(x, g_ffn, w_pq, sub_keys1, sub_keys2)
        return _peer(h, x, r2, e2, n1, c1, u, vt, peer_tile)

    qa, ka_f, ka_b, va_f, va_b, qg, kg, vg, la, sr = _mixer_in(xp, g_mix, w_in, g_q, g_k, w_gate2, b_gate2)
    oa = _attn_prompt(qa, ka_b, va_b, rel_bias, bp, sp)
    og, gla_p = _gla(qg, kg, vg, la, jnp.zeros((bp, G_HEADS, G_DK, G_DV), F32), bp, sp, GLA_TILE)
    xp = _mixer_out(xp, oa, og, sr, g_gla_out, w_out)
    mk, mv = _mem_kv(mem_prompt.reshape(bp * N_MEM, D_MODEL), g_mem, w_ck, w_cv, g_ck)
    xp = tail(xp, mk.reshape(bp, N_MEM, D_MODEL), mv.reshape(bp, N_MEM, D_MODEL), bp, sp, TOKEN_TILE,
              PEER_TOKEN_TILE)
    keep = min(BAND, sp)
    outs_p = (heads(ka_f, bp, sp)[:, -keep:], heads(va_f, bp, sp)[:, -keep:], gla_p,
              mk.reshape(bp, N_MEM, X_HEADS, X_HEAD_DIM), mv.reshape(bp, N_MEM, X_HEADS, X_HEAD_DIM))

    qa, ka_f, ka_b, va_f, va_b, qg, kg, vg, la, sr = _mixer_in(xs, g_mix, w_in, g_q, g_k, w_gate2, b_gate2)
    oa = _attn_sample(qa, ka_b, va_b, cache_k, cache_v, rel_bias, bs, ts, past_len)
    og, gla_s = _gla(qg, kg, vg, la, state, bs, ts, ts)
    xs = _mixer_out(xs, oa, og, sr, g_gla_out, w_out)
    xs = tail(xs, cmem_k.reshape(bs, N_MEM, D_MODEL), cmem_v.reshape(bs, N_MEM, D_MODEL), bs, ts, ts, bs * ts)
    outs_s = (heads(ka_f, bs, ts), heads(va_f, bs, ts), gla_s)

    return (xp.reshape(bp, sp, D_MODEL), xs.reshape(bs, ts, D_MODEL)) + outs_p + outs_s


def kernel(x_prompt, x_sample, cache_att_k, cache_att_v, state_gla, cache_mem_k, cache_mem_v, mem_prompt, g_mix, w_in, g_q, g_k, rel_bias, w_gate2, b_gate2, g_gla_out, w_out, g_cross, g_mem, w_cq, w_ck, w_cv, g_cq, g_ck, w_co, g_ffn, w_pq, sub_keys1, sub_keys2, expert_u, expert_v):
    depth = w_in.shape[0]
    past_len = 4096
    xp, xs = x_prompt, x_sample
    per_layer = []
    for l in range(depth):
        outs = _layer(xp, xs, cache_att_k[l], cache_att_v[l], state_gla[l], cache_mem_k[l], cache_mem_v[l],
                      mem_prompt, past_len,
                      g_mix[l], w_in[l], g_q[l], g_k[l], rel_bias[l], w_gate2[l], b_gate2[l], g_gla_out[l], w_out[l],
                      g_cross[l], g_mem[l], w_cq[l], w_ck[l], w_cv[l], g_cq[l], g_ck[l], w_co[l],
                      g_ffn[l], w_pq[l], sub_keys1[l], sub_keys2[l], expert_u[l], expert_v[l])
        xp, xs = outs[0], outs[1]
        per_layer.append(outs[2:])
    stacked = [jnp.stack([p[i] for p in per_layer]) for i in range(8)]
    return (xp, xs) + tuple(stacked)
```

```python
import functools
import math

import jax
import jax.numpy as jnp
from jax import lax
from jax.experimental import pallas as pl
from jax.experimental.pallas import tpu as pltpu

F32 = jnp.float32
BF16 = jnp.bfloat16
HIGHEST = lax.Precision.HIGHEST

D_MODEL = 1024
CHUNK = 64
N_PAST_CHUNKS = 8
BAND = (N_PAST_CHUNKS + 1) * CHUNK
A_HEADS = 8
A_HEAD_DIM = 64
A_WIDTH = A_HEADS * A_HEAD_DIM
REL_CLIP = 128
G_HEADS = 4
G_DK = 64
G_DV = 128
G_KW = G_HEADS * G_DK
G_WIDTH = G_HEADS * G_DV
G_GATE_RANK = 16
G_GATE_NORM = 16.0
G_BLOCK = 16
N_MEM = 256
X_HEADS = 4
X_HEAD_DIM = D_MODEL // X_HEADS
P_HEADS = 8
P_NKEYS = 128
P_QDIM = 256
P_HALF = P_QDIM // 2
P_TOPK = 16
EPS = 1e-6
NEG = -1e30

LANES = 128
VMEM_LIMIT = 56 * 1024 * 1024

TOKEN_TILE = 256
GLA_TILE = 256
PEER_TOKEN_TILE = 1024
PEER_EXPERT_TILE = 512


def _params(*sem):
    return pltpu.CompilerParams(dimension_semantics=sem, vmem_limit_bytes=VMEM_LIMIT)


def _full(shape):
    return pl.BlockSpec(shape, lambda *_: (0,) * len(shape))


def _rms(x, g):
    return x * lax.rsqrt(jnp.mean(x * x, axis=-1, keepdims=True) + EPS) * g


def _seg_mean_sq(z, seg_avg):
    sq = z * z
    hi = sq.astype(BF16)
    lo = (sq - hi.astype(F32)).astype(BF16)
    return (jnp.dot(hi, seg_avg, preferred_element_type=F32)
            + jnp.dot(lo, seg_avg, preferred_element_type=F32))


def _seg_avg_matrix(width, seg):
    r = jnp.arange(width) // seg
    return jnp.where(r[:, None] == r[None, :], 1.0 / seg, 0.0).astype(BF16)


def _mixer_in_kernel(x_ref, gmix_ref, w_ref, gq_ref, gk_ref, avg_ref, wg2_ref, bg2_ref,
                     qa_ref, kaf_ref, kab_ref, vaf_ref, vab_ref, qg_ref, kg_ref, vg_ref, la_ref, sr_ref):
    h = _rms(x_ref[...], gmix_ref[...]).astype(BF16)
    z = jnp.dot(h, w_ref[...], preferred_element_type=F32)
    o = 0
    qa = z[:, o:o + A_WIDTH]; o += A_WIDTH
    ka = z[:, o:o + A_WIDTH]; o += A_WIDTH
    va = z[:, o:o + A_WIDTH]; o += A_WIDTH
    qg = z[:, o:o + G_KW]; o += G_KW
    kg = z[:, o:o + G_KW]; o += G_KW
    vg = z[:, o:o + G_WIDTH]; o += G_WIDTH
    r = z[:, o:o + G_WIDTH]; o += G_WIDTH
    gl = z[:, o:o + LANES]
    avg = avg_ref[...]
    qa = qa * lax.rsqrt(_seg_mean_sq(qa, avg) + EPS) * gq_ref[...]
    ka = ka * lax.rsqrt(_seg_mean_sq(ka, avg) + EPS) * gk_ref[...]
    qa_ref[...] = (qa * (A_HEAD_DIM ** -0.5)).astype(BF16)
    kaf_ref[...] = ka
    kab_ref[...] = ka.astype(BF16)
    vaf_ref[...] = va
    vab_ref[...] = va.astype(BF16)
    qg_ref[...] = qg * (G_DK ** -0.5)
    kg_ref[...] = kg
    vg_ref[...] = vg
    pre = jnp.dot(gl, wg2_ref[...], precision=HIGHEST, preferred_element_type=F32) + bg2_ref[...]
    la_ref[...] = (jnp.minimum(pre, 0.0) - jnp.log1p(jnp.exp(-jnp.abs(pre)))) / G_GATE_NORM
    sr_ref[...] = r * jax.nn.sigmoid(r)


def _mixer_in(x, g_mix, w_in, g_q, g_k, w_gate2, b_gate2):
    n = x.shape[0]
    c_gl = 3 * A_WIDTH + 2 * G_KW + G_WIDTH
    w_cat = jnp.concatenate(
        [w_in[:, :c_gl], w_in[:, c_gl + G_GATE_RANK:],
         jnp.pad(w_in[:, c_gl:c_gl + G_GATE_RANK], ((0, 0), (0, LANES - G_GATE_RANK)))], axis=1).astype(BF16)
    wg2 = jnp.pad(w_gate2, ((0, LANES - G_GATE_RANK), (0, 0)))
    wcols = w_cat.shape[1]
    tm = TOKEN_TILE
    row = lambda w: pl.BlockSpec((tm, w), lambda i: (i, 0))
    outs = [(A_WIDTH, BF16), (A_WIDTH, F32), (A_WIDTH, BF16), (A_WIDTH, F32), (A_WIDTH, BF16),
            (G_KW, F32), (G_KW, F32), (G_WIDTH, F32), (G_KW, F32), (G_WIDTH, F32)]
    return pl.pallas_call(
        _mixer_in_kernel,
        grid=(n // tm,),
        in_specs=[row(D_MODEL), _full((1, D_MODEL)), _full((D_MODEL, wcols)), _full((1, A_WIDTH)),
                  _full((1, A_WIDTH)), _full((A_WIDTH, A_WIDTH)), _full((LANES, G_KW)), _full((1, G_KW))],
        out_specs=[row(w) for w, _ in outs],
        out_shape=[jax.ShapeDtypeStruct((n, w), dt) for w, dt in outs],
        compiler_params=_params("parallel"),
        name="mixer_in",
    )(x, g_mix.reshape(1, -1), w_cat, jnp.tile(g_q, A_HEADS).reshape(1, -1), jnp.tile(g_k, A_HEADS).reshape(1, -1),
      _seg_avg_matrix(A_WIDTH, A_HEAD_DIM), wg2, b_gate2.reshape(1, -1))


def _attend(q, segments):
    tq = q.shape[0]
    lane = lax.broadcasted_iota(jnp.int32, (tq, LANES), 1)
    outs = []
    for pair in range(A_HEADS // 2):
        sl = slice(pair * LANES, (pair + 1) * LANES)
        q2 = q[:, sl]
        halves = []
        for half in range(2):
            head = 2 * pair + half
            mine = (lane < A_HEAD_DIM) if half == 0 else (lane >= A_HEAD_DIM)
            qm = jnp.where(mine, q2, jnp.zeros_like(q2))
            scores = [lax.dot_general(qm, k[:, sl], (((1,), (1,)), ((), ())), preferred_element_type=F32) + bias(head)
                      for k, _, bias in segments]
            m = functools.reduce(jnp.maximum, [jnp.max(s, axis=-1, keepdims=True) for s in scores])
            es = [jnp.exp(s - m) for s in scores]
            inv = 1.0 / functools.reduce(jnp.add, [jnp.sum(e, axis=-1, keepdims=True) for e in es])
            o = functools.reduce(jnp.add, [
                jnp.dot((e * inv).astype(BF16), v[:, sl], preferred_element_type=F32)
                for e, (_, v, _) in zip(es, segments)])
            halves.append(o)
        outs.append(jnp.where(lane < A_HEAD_DIM, halves[0], halves[1]))
    return jnp.concatenate(outs, axis=1)


ATT_GROUP = 4
ATT_ROWS = ATT_GROUP * CHUNK
ATT_WINDOW = (N_PAST_CHUNKS + ATT_GROUP) * CHUNK
ATT_CASES = -(-N_PAST_CHUNKS // ATT_GROUP) + 1
BIAS_WRAP = 1024


def _attn_prompt_kernel(q_ref, k_ref, v_ref, tb_ref, o_ref, bias_ref):
    g = pl.program_id(1)
    first = jnp.maximum(g * ATT_GROUP - N_PAST_CHUNKS, 0)
    start = pl.multiple_of(first * CHUNK, CHUNK)

    for case in range(ATT_CASES):
        @pl.when(g == case)
        def _(case=case):
            off = min(case * ATT_GROUP, N_PAST_CHUNKS) * CHUNK
            q_chunk = (off + lax.broadcasted_iota(jnp.int32, (ATT_ROWS, ATT_WINDOW), 0)) // CHUNK
            k_chunk = lax.broadcasted_iota(jnp.int32, (ATT_ROWS, ATT_WINDOW), 1) // CHUNK
            valid = (k_chunk <= q_chunk) & (k_chunk >= q_chunk - N_PAST_CHUNKS)
            for head in range(A_HEADS):
                table = jnp.broadcast_to(tb_ref[head:head + 1, :], (ATT_ROWS, BIAS_WRAP))
                rolled = pltpu.roll(table, off, 1, stride=1, stride_axis=0)
                bias_ref[head] = jnp.where(valid, rolled[:, :ATT_WINDOW], NEG)

    kb = k_ref[0, pl.ds(start, ATT_WINDOW), :]
    vb = v_ref[0, pl.ds(start, ATT_WINDOW), :]
    o_ref[...] = _attend(q_ref[...], [(kb, vb, lambda head: bias_ref[head])]).astype(o_ref.dtype)


def _band_bias(rel_bias, q_pos, k_pos):
    rel = jnp.clip(q_pos[:, None] - k_pos[None, :], -REL_CLIP, REL_CLIP) + REL_CLIP
    return rel_bias[:, rel].astype(F32)


def _wrapped_bias_table(rel_bias):
    u = jnp.arange(BIAS_WRAP)
    dist = jnp.where(u < CHUNK, -u, BIAS_WRAP - u)
    return rel_bias[:, jnp.clip(dist, -REL_CLIP, REL_CLIP) + REL_CLIP].astype(F32)


def _attn_prompt(qa, ka, va, rel_bias, batch, seq):
    ng = seq // ATT_ROWS
    assert seq % ATT_ROWS == 0 and ng >= ATT_CASES and seq >= ATT_WINDOW
    kv = pl.BlockSpec((1, seq, A_WIDTH), lambda b, g: (b, 0, 0))
    return pl.pallas_call(
        _attn_prompt_kernel,
        grid=(batch, ng),
        in_specs=[pl.BlockSpec((ATT_ROWS, A_WIDTH), lambda b, g: (b * ng + g, 0)), kv, kv,
                  _full((A_HEADS, BIAS_WRAP))],
        out_specs=pl.BlockSpec((ATT_ROWS, A_WIDTH), lambda b, g: (b * ng + g, 0)),
        out_shape=jax.ShapeDtypeStruct((batch * seq, A_WIDTH), BF16),
        scratch_shapes=[pltpu.VMEM((A_HEADS, ATT_ROWS, ATT_WINDOW), F32)],
        compiler_params=_params("arbitrary", "arbitrary"),
        name="attn_prompt",
    )(qa, ka.reshape(batch, seq, A_WIDTH), va.reshape(batch, seq, A_WIDTH), _wrapped_bias_table(rel_bias))


def _attn_sample_kernel(q_ref, kn_ref, vn_ref, kc_ref, vc_ref, bc_ref, bn_ref, o_ref):
    segs = [(kc_ref[0].astype(BF16), vc_ref[0].astype(BF16), lambda head: bc_ref[head]),
            (kn_ref[...], vn_ref[...], lambda head: bn_ref[head])]
    o_ref[...] = _attend(q_ref[...], segs).astype(o_ref.dtype)


def _attn_sample(qa, ka, va, cache_k, cache_v, rel_bias, batch, t, past_len):
    r = cache_k.shape[1]
    q_pos = past_len + jnp.arange(t)
    c_pos = past_len - r + jnp.arange(r)
    valid_c = ((c_pos[None, :] // CHUNK) <= (q_pos[:, None] // CHUNK)) & \
              ((c_pos[None, :] // CHUNK) >= (q_pos[:, None] // CHUNK) - N_PAST_CHUNKS)
    valid_n = ((q_pos[None, :] // CHUNK) <= (q_pos[:, None] // CHUNK)) & \
              ((q_pos[None, :] // CHUNK) >= (q_pos[:, None] // CHUNK) - N_PAST_CHUNKS)
    bias_c = jnp.where(valid_c[None], _band_bias(rel_bias, q_pos, c_pos), NEG)
    bias_n = jnp.where(valid_n[None], _band_bias(rel_bias, q_pos, q_pos), NEG)
    tok = pl.BlockSpec((t, A_WIDTH), lambda b: (b, 0))
    cache = pl.BlockSpec((1, r, A_WIDTH), lambda b: (b, 0, 0))
    return pl.pallas_call(
        _attn_sample_kernel,
        grid=(batch,),
        in_specs=[tok, tok, tok, cache, cache, _full((A_HEADS, t, r)), _full((A_HEADS, t, t))],
        out_specs=tok,
        out_shape=jax.ShapeDtypeStruct((batch * t, A_WIDTH), BF16),
        compiler_params=_params("parallel"),
        name="attn_sample",
    )(qa, ka, va, cache_k.reshape(batch, r, A_WIDTH), cache_v.reshape(batch, r, A_WIDTH), bias_c, bias_n)


def _gla_kernel(q_ref, k_ref, v_ref, la_ref, s0_ref, tri_ref, expand_ref, bdmask_ref, o_ref, sfin_ref, state_ref):
    tb = q_ref.shape[0]
    nb = tb // G_BLOCK

    @pl.when(pl.program_id(1) == 0)
    def _():
        state_ref[...] = s0_ref[0]

    q = q_ref[...]
    k = k_ref[...]
    v = v_ref[...]
    cum = jnp.dot(tri_ref[...], la_ref[...], precision=HIGHEST, preferred_element_type=F32)
    q3 = q.reshape(nb, G_BLOCK, G_KW)
    k3 = k.reshape(nb, G_BLOCK, G_KW)
    v3 = v.reshape(nb, G_BLOCK, G_WIDTH)
    cum3 = cum.reshape(nb, G_BLOCK, G_KW)
    row = lax.broadcasted_iota(jnp.int32, (nb, G_BLOCK, G_KW), 1)
    expand = expand_ref[...]

    intra = jnp.zeros((nb, G_BLOCK, G_WIDTH), F32)
    for s in range(G_BLOCK):
        k_s = jnp.broadcast_to(k3[:, s:s + 1, :], (nb, G_BLOCK, G_KW))
        c_s = jnp.broadcast_to(cum3[:, s:s + 1, :], (nb, G_BLOCK, G_KW))
        v_s = jnp.broadcast_to(v3[:, s:s + 1, :], (nb, G_BLOCK, G_WIDTH))
        prod = q3 * k_s * jnp.exp(jnp.where(row >= s, cum3 - c_s, -jnp.inf))
        att = jnp.dot(prod.reshape(tb, G_KW).astype(BF16), expand, preferred_element_type=F32)
        intra = intra + att.reshape(nb, G_BLOCK, G_WIDTH) * v_s
    o_ref[...] = intra.reshape(tb, G_WIDTH)

    last3 = jnp.broadcast_to(cum3[:, G_BLOCK - 1:G_BLOCK, :], (nb, G_BLOCK, G_KW))
    q_dec = (q3 * jnp.exp(cum3)).reshape(tb, G_KW).astype(BF16)
    k_dec = (k3 * jnp.exp(last3 - cum3)).reshape(tb, G_KW).astype(BF16)
    carry = jnp.exp(last3).reshape(tb, G_KW)
    vb = v.astype(BF16)
    bdmask = bdmask_ref[...]
    for j in range(nb):
        rows = slice(j * G_BLOCK, (j + 1) * G_BLOCK)
        state = state_ref[...]
        o_ref[rows, :] += lax.dot_general(q_dec[rows], state.astype(BF16), (((1,), (1,)), ((), ())),
                                          preferred_element_type=F32)
        upd = lax.dot_general(vb[rows], k_dec[rows], (((0,), (0,)), ((), ())), preferred_element_type=F32)
        state_ref[...] = carry[j * G_BLOCK:j * G_BLOCK + 1, :] * state + upd * bdmask

    @pl.when(pl.program_id(1) == pl.num_programs(1) - 1)
    def _():
        sfin_ref[0] = state_ref[...]


def _gla(qg, kg, vg, la, s0, batch, t, tile):
    nt = t // tile
    hsel = jnp.arange(G_KW) // G_DK
    vsel = jnp.arange(G_WIDTH) // G_DV
    same = hsel[:, None] == vsel[None, :]
    expand = same.astype(BF16)
    bdmask = same.T.astype(F32)
    s0_t = jnp.swapaxes(s0, 2, 3).reshape(batch, G_WIDTH, G_DK)
    s0_bd = jnp.where(same.T[None], jnp.tile(s0_t, (1, 1, G_HEADS)), 0.0)
    rr = jnp.arange(tile)
    tri = ((rr[:, None] // G_BLOCK == rr[None, :] // G_BLOCK) & (rr[None, :] <= rr[:, None])).astype(F32)
    tok = lambda w: pl.BlockSpec((tile, w), lambda b, i: (b * nt + i, 0))
    st = pl.BlockSpec((1, G_WIDTH, G_KW), lambda b, i: (b, 0, 0))
    o, s_bd = pl.pallas_call(
        _gla_kernel,
        grid=(batch, nt),
        in_specs=[tok(G_KW), tok(G_KW), tok(G_WIDTH), tok(G_KW), st,
                  _full((tile, tile)), _full((G_KW, G_WIDTH)), _full((G_WIDTH, G_KW))],
        out_specs=[tok(G_WIDTH), st],
        out_shape=[jax.ShapeDtypeStruct((batch * t, G_WIDTH), F32),
                   jax.ShapeDtypeStruct((batch, G_WIDTH, G_KW), F32)],
        scratch_shapes=[pltpu.VMEM((G_WIDTH, G_KW), F32)],
        compiler_params=_params("parallel", "arbitrary"),
        name="gla",
    )(qg, kg, vg, la, s0_bd, tri, expand, bdmask)
    s_fin = jnp.stack([s_bd[:, h * G_DV:(h + 1) * G_DV, h * G_DK:(h + 1) * G_DK] for h in range(G_HEADS)], axis=1)
    return o, jnp.swapaxes(s_fin, 2, 3)


def _mixer_out_kernel(x_ref, oa_ref, og_ref, sr_ref, gg_ref, avg_ref, wa_ref, wg_ref, y_ref):
    og = og_ref[...]
    og = og * lax.rsqrt(_seg_mean_sq(og, avg_ref[...]) + EPS) * gg_ref[...] * sr_ref[...]
    y = jnp.dot(oa_ref[...], wa_ref[...], preferred_element_type=F32)
    y = y + jnp.dot(og.astype(BF16), wg_ref[...], preferred_element_type=F32)
    y_ref[...] = x_ref[...] + y


def _mixer_out(x, oa, og, sr, g_gla_out, w_out):
    n = x.shape[0]
    tm = TOKEN_TILE
    row = lambda w: pl.BlockSpec((tm, w), lambda i: (i, 0))
    w_out = w_out.astype(BF16)
    return pl.pallas_call(
        _mixer_out_kernel,
        grid=(n // tm,),
        in_specs=[row(D_MODEL), row(A_WIDTH), row(G_WIDTH), row(G_WIDTH), _full((1, G_WIDTH)),
                  _full((G_WIDTH, G_WIDTH)), _full((A_WIDTH, D_MODEL)), _full((G_WIDTH, D_MODEL))],
        out_specs=row(D_MODEL),
        out_shape=jax.ShapeDtypeStruct((n, D_MODEL), F32),
        compiler_params=_params("parallel"),
        name="mixer_out",
    )(x, oa, og, sr, jnp.tile(g_gla_out, G_HEADS).reshape(1, -1), _seg_avg_matrix(G_WIDTH, G_DV),
      w_out[:A_WIDTH], w_out[A_WIDTH:])


def _mem_kv_kernel(m_ref, gm_ref, wk_ref, wv_ref, gk_ref, avg_ref, k_ref, v_ref):
    m = _rms(m_ref[...], gm_ref[...]).astype(BF16)
    k = jnp.dot(m, wk_ref[...], preferred_element_type=F32)
    k_ref[...] = k * lax.rsqrt(_seg_mean_sq(k, avg_ref[...]) + EPS) * gk_ref[...]
    v_ref[...] = jnp.dot(m, wv_ref[...], preferred_element_type=F32)


def _mem_kv(mem, g_mem, w_ck, w_cv, g_ck):
    n = mem.shape[0]
    tm = TOKEN_TILE
    row = pl.BlockSpec((tm, D_MODEL), lambda i: (i, 0))
    sq = _full((D_MODEL, D_MODEL))
    vec = _full((1, D_MODEL))
    return pl.pallas_call(
        _mem_kv_kernel,
        grid=(n // tm,),
        in_specs=[row, vec, sq, sq, vec, sq],
        out_specs=[row, row],
        out_shape=[jax.ShapeDtypeStruct((n, D_MODEL), F32)] * 2,
        compiler_params=_params("parallel"),
        name="mem_kv",
    )(mem, g_mem.reshape(1, -1), w_ck.astype(BF16), w_cv.astype(BF16), jnp.tile(g_ck, X_HEADS).reshape(1, -1),
      _seg_avg_matrix(D_MODEL, X_HEAD_DIM))


def _cross_kernel(x_ref, gc_ref, wq_ref, gq_ref, avg_ref, mk_ref, mv_ref, wo_ref, y_ref):
    x = x_ref[...]
    h = _rms(x, gc_ref[...]).astype(BF16)
    q = jnp.dot(h, wq_ref[...], preferred_element_type=F32)
    q = q * lax.rsqrt(_seg_mean_sq(q, avg_ref[...]) + EPS) * gq_ref[...] * (X_HEAD_DIM ** -0.5)
    q = q.astype(BF16)
    mk = mk_ref[0].astype(BF16)
    mv = mv_ref[0].astype(BF16)
    outs = []
    for head in range(X_HEADS):
        sl = slice(head * X_HEAD_DIM, (head + 1) * X_HEAD_DIM)
        s = lax.dot_general(q[:, sl], mk[:, sl], (((1,), (1,)), ((), ())), preferred_element_type=F32)
        e = jnp.exp(s - jnp.max(s, axis=-1, keepdims=True))
        p = (e * (1.0 / jnp.sum(e, axis=-1, keepdims=True))).astype(BF16)
        outs.append(jnp.dot(p, mv[:, sl], preferred_element_type=F32))
    o = jnp.concatenate(outs, axis=1).astype(BF16)
    y_ref[...] = x + jnp.dot(o, wo_ref[...], preferred_element_type=F32)


def _cross(x, mem_k, mem_v, g_cross, w_cq, g_cq, w_co, batch, t, tile):
    nt = t // tile
    row = pl.BlockSpec((tile, D_MODEL), lambda b, i: (b * nt + i, 0))
    mem = pl.BlockSpec((1, N_MEM, D_MODEL), lambda b, i: (b, 0, 0))
    sq = _full((D_MODEL, D_MODEL))
    vec = _full((1, D_MODEL))
    return pl.pallas_call(
        _cross_kernel,
        grid=(batch, nt),
        in_specs=[row, vec, sq, vec, sq, mem, mem, sq],
        out_specs=row,
        out_shape=jax.ShapeDtypeStruct((batch * t, D_MODEL), F32),
        compiler_params=_params("parallel", "parallel"),
        name="cross",
    )(x, g_cross.reshape(1, -1), w_cq.astype(BF16), jnp.tile(g_cq, X_HEADS).reshape(1, -1),
      _seg_avg_matrix(D_MODEL, X_HEAD_DIM), mem_k, mem_v, w_co.astype(BF16))


def _top_values(work, count, want_rank=False):
    rows = lax.broadcasted_iota(jnp.int32, work.shape, 0).astype(F32)
    nrows = float(work.shape[0])
    rank = jnp.full(work.shape, float(count), F32)
    out = []
    for r in range(count):
        mx = jnp.max(work, axis=0, keepdims=True)
        first = jnp.min(jnp.where(work == mx, rows, nrows), axis=0, keepdims=True)
        kill = rows == first
        work = jnp.where(kill, -jnp.inf, work)
        if want_rank:
            rank = jnp.where(kill, float(r), rank)
        out.append(mx)
    vals = jnp.concatenate(out, axis=0)
    return (vals, rank) if want_rank else vals


def _route_kernel(x_ref, gf_ref, whi_ref, wlo_ref, sk1_ref, sk2_ref, ht_ref, r2_ref, e2_ref, n1_ref, c1_ref):
    hf = _rms(x_ref[...], gf_ref[...])
    ht_ref[...] = jnp.transpose(hf).astype(BF16)
    h_hi = hf.astype(BF16)
    h_lo = (hf - h_hi.astype(F32)).astype(BF16)
    qry = (jnp.dot(h_hi, whi_ref[...], preferred_element_type=F32)
           + jnp.dot(h_hi, wlo_ref[...], preferred_element_type=F32)
           + jnp.dot(h_lo, whi_ref[...], preferred_element_type=F32))
    nt = (((1,), (1,)), ((), ()))
    nk = P_TOPK + 1
    t = qry.shape[0]
    for head in range(P_HEADS):
        q1 = qry[:, head * P_QDIM:head * P_QDIM + P_HALF]
        q2 = qry[:, head * P_QDIM + P_HALF:(head + 1) * P_QDIM]
        s1 = lax.dot_general(sk1_ref[...], q1, nt, precision=HIGHEST, preferred_element_type=F32)
        s2 = lax.dot_general(sk2_ref[...], q2, nt, precision=HIGHEST, preferred_element_type=F32)
        v1 = _top_values(s1, nk)
        v2, rank2 = _top_values(s2, nk, want_rank=True)
        slabs = [v1[a:a + 1, :] + v2[:nk // (a + 1), :] for a in range(nk)]
        pad = (-sum(sl.shape[0] for sl in slabs)) % 8
        cand = jnp.concatenate(slabs + [jnp.full((pad, t), -jnp.inf, F32)], axis=0)
        top = _top_values(cand, nk)
        m1 = v1[0:1, :]
        m2 = v2[0:1, :]
        z = jnp.sum(jnp.exp(top[:P_TOPK, :] - (m1 + m2)), axis=0, keepdims=True)
        tau = 0.5 * (top[P_TOPK - 1:P_TOPK, :] + top[P_TOPK:nk, :])
        thr = tau - s1
        n1 = jnp.zeros_like(s1)
        for b in range(P_TOPK):
            n1 = n1 + jnp.where(v2[b:b + 1, :] >= thr, 1.0, 0.0)
        r2_ref[head] = rank2.astype(BF16)
        e2_ref[head] = jnp.exp(s2 - m2).astype(BF16)
        n1_ref[head] = n1
        c1_ref[head] = jnp.exp(s1 - m1) * (1.0 / z)


def _route(x, g_ffn, w_pq, sub_keys1, sub_keys2):
    n = x.shape[0]
    w_hi = w_pq.astype(BF16)
    w_lo = (w_pq - w_hi.astype(F32)).astype(BF16)
    tm = TOKEN_TILE
    keys = pl.BlockSpec((P_HEADS, P_NKEYS, tm), lambda i: (0, 0, i))
    tab = lambda dt: jax.ShapeDtypeStruct((P_HEADS, P_NKEYS, n), dt)
    return pl.pallas_call(
        _route_kernel,
        grid=(n // tm,),
        in_specs=[pl.BlockSpec((tm, D_MODEL), lambda i: (i, 0)), _full((1, D_MODEL)),
                  _full((D_MODEL, P_HEADS * P_QDIM)), _full((D_MODEL, P_HEADS * P_QDIM)),
                  _full((P_NKEYS, P_HALF)), _full((P_NKEYS, P_HALF))],
        out_specs=[pl.BlockSpec((D_MODEL, tm), lambda i: (0, i)), keys, keys, keys, keys],
        out_shape=[jax.ShapeDtypeStruct((D_MODEL, n), BF16), tab(BF16), tab(BF16), tab(F32), tab(F32)],
        compiler_params=_params("parallel"),
        name="peer_route",
    )(x, g_ffn.reshape(1, -1), w_hi, w_lo, sub_keys1, sub_keys2)


def _peer_kernel(ht_ref, x_ref, u_ref, vt_ref, r2_ref, e2_ref, n1_ref, c1_ref, y_ref, acc_ref):
    j = pl.program_id(1)

    @pl.when(j == 0)
    def _():
        acc_ref[...] = jnp.zeros_like(acc_ref)

    a = jnp.dot(u_ref[...], ht_ref[...], preferred_element_type=F32)
    act = (0.5 * a * (1.0 + lax.erf(a * (1.0 / math.sqrt(2.0))))).astype(BF16)
    t = a.shape[1]
    gates = []
    for half in range(PEER_EXPERT_TILE // P_NKEYS):
        i1 = j * (PEER_EXPERT_TILE // P_NKEYS) + half
        g = jnp.zeros((P_NKEYS, t), BF16)
        for head in range(P_HEADS):
            n1 = n1_ref[head, pl.ds(i1, 1), :].astype(BF16)
            c1 = c1_ref[head, pl.ds(i1, 1), :].astype(BF16)
            g = g + jnp.where(r2_ref[head] < n1, e2_ref[head], jnp.zeros((), BF16)) * c1
        gates.append(g)
    ga = act * jnp.concatenate(gates, axis=0)
    acc_ref[...] += jnp.dot(vt_ref[0], ga, preferred_element_type=F32)

    @pl.when(j == pl.num_programs(1) - 1)
    def _():
        y_ref[...] = x_ref[...] + jnp.transpose(acc_ref[...])


def _peer(ht, x, r2, e2, n1, c1, u, vt, tt):
    n = x.shape[0]
    et = PEER_EXPERT_TILE
    ne = u.shape[0] // et
    keys = pl.BlockSpec((P_HEADS, P_NKEYS, tt), lambda i, j: (0, 0, i))
    once = pl.Buffered(1)
    return pl.pallas_call(
        _peer_kernel,
        grid=(n // tt, ne),
        in_specs=[pl.BlockSpec((D_MODEL, tt), lambda i, j: (0, i)),
                  pl.BlockSpec((tt, D_MODEL), lambda i, j: (i, 0), pipeline_mode=once),
                  pl.BlockSpec((et, D_MODEL), lambda i, j: (j, 0)),
                  pl.BlockSpec((1, D_MODEL, et), lambda i, j: (j, 0, 0)),
                  keys, keys, keys, keys],
        out_specs=pl.BlockSpec((tt, D_MODEL), lambda i, j: (i, 0), pipeline_mode=once),
        out_shape=jax.ShapeDtypeStruct((n, D_MODEL), F32),
        scratch_shapes=[pltpu.VMEM((D_MODEL, tt), F32)],
        compiler_params=_params("parallel", "arbitrary"),
        name="peer_experts",
    )(ht, x, u, vt, r2, e2, n1, c1)


def _layer(xp, xs, cache_k, cache_v, state, cmem_k, cmem_v, mem_prompt, past_len,
           g_mix, w_in, g_q, g_k, rel_bias, w_gate2, b_gate2, g_gla_out, w_out,
           g_cross, g_mem, w_cq, w_ck, w_cv, g_cq, g_ck, w_co, g_ffn, w_pq, sub_keys1, sub_keys2, expert_u, expert_v):
    bp, sp, _ = xp.shape
    bs, ts, _ = xs.shape
    xp = xp.reshape(bp * sp, D_MODEL)
    xs = xs.reshape(bs * ts, D_MODEL)
    u = expert_u.astype(BF16)
    ne = expert_v.shape[0] // PEER_EXPERT_TILE
    vt = jnp.swapaxes(expert_v.reshape(ne, PEER_EXPERT_TILE, D_MODEL), 1, 2).astype(BF16)
    heads = lambda a, b, t: a.reshape(b, t, A_HEADS, A_HEAD_DIM)

    def tail(x, mem_k, mem_v, batch, t, cross_tile, peer_tile):
        x = _cross(x, mem_k, mem_v, g_cross, w_cq, g_cq, w_co, batch, t, cross_tile)
        ht, r2, e2, n1, c1 = _route(x, g_ffn, w_pq, sub_keys1, sub_keys2)
        return _peer(ht, x, r2, e2, n1, c1, u, vt, peer_tile)

    qa, ka_f, ka_b, va_f, va_b, qg, kg, vg, la, sr = _mixer_in(xp, g_mix, w_in, g_q, g_k, w_gate2, b_gate2)
    oa = _attn_prompt(qa, ka_b, va_b, rel_bias, bp, sp)
    og, gla_p = _gla(qg, kg, vg, la, jnp.zeros((bp, G_HEADS, G_DK, G_DV), F32), bp, sp, GLA_TILE)
    xp = _mixer_out(xp, oa, og, sr, g_gla_out, w_out)
    mk, mv = _mem_kv(mem_prompt.reshape(bp * N_MEM, D_MODEL), g_mem, w_ck, w_cv, g_ck)
    xp = tail(xp, mk.reshape(bp, N_MEM, D_MODEL), mv.reshape(bp, N_MEM, D_MODEL), bp, sp, TOKEN_TILE,
              PEER_TOKEN_TILE)
    keep = min(BAND, sp)
    outs_p = (heads(ka_f, bp, sp)[:, -keep:], heads(va_f, bp, sp)[:, -keep:], gla_p,
              mk.reshape(bp, N_MEM, X_HEADS, X_HEAD_DIM), mv.reshape(bp, N_MEM, X_HEADS, X_HEAD_DIM))

    qa, ka_f, ka_b, va_f, va_b, qg, kg, vg, la, sr = _mixer_in(xs, g_mix, w_in, g_q, g_k, w_gate2, b_gate2)
    oa = _attn_sample(qa, ka_b, va_b, cache_k, cache_v, rel_bias, bs, ts, past_len)
    og, gla_s = _gla(qg, kg, vg, la, state, bs, ts, ts)
    xs = _mixer_out(xs, oa, og, sr, g_gla_out, w_out)
    xs = tail(xs, cmem_k.reshape(bs, N_MEM, D_MODEL), cmem_v.reshape(bs, N_MEM, D_MODEL), bs, ts, ts, bs * ts)
    outs_s = (heads(ka_f, bs, ts), heads(va_f, bs, ts), gla_s)

    return (xp.reshape(bp, sp, D_MODEL), xs.reshape(bs, ts, D_MODEL)) + outs_p + outs_s


def kernel(x_prompt, x_sample, cache_att_k, cache_att_v, state_gla, cache_mem_k, cache_mem_v, mem_prompt, g_mix, w_in, g_q, g_k, rel_bias, w_gate2, b_gate2, g_gla_out, w_out, g_cross, g_mem, w_cq, w_ck, w_cv, g_cq, g_ck, w_co, g_ffn, w_pq, sub_keys1, sub_keys2, expert_u, expert_v):
    depth = w_in.shape[0]
    past_len = 4096
    xp, xs = x_prompt, x_sample
    per_layer = []
    for l in range(depth):
        outs = _layer(xp, xs, cache_att_k[l], cache_att_v[l], state_gla[l], cache_mem_k[l], cache_mem_v[l],
                      mem_prompt, past_len,
                      g_mix[l], w_in[l], g_q[l], g_k[l], rel_bias[l], w_gate2[l], b_gate2[l], g_gla_out[l], w_out[l],
                      g_cross[l], g_mem[l], w_cq[l], w_ck[l], w_cv[l], g_cq[l], g_ck[l], w_co[l],
                      g_ffn[l], w_pq[l], sub_keys1[l], sub_keys2[l], expert_u[l], expert_v[l])
        xp, xs = outs[0], outs[1]
        per_layer.append(outs[2:])
    stacked = [jnp.stack([p[i] for p in per_layer]) for i in range(8)]
    return (xp, xs) + tuple(stacked)
```

```python
import functools
import math

import jax
import jax.numpy as jnp
from jax import lax
from jax.experimental import pallas as pl
from jax.experimental.pallas import tpu as pltpu

F32 = jnp.float32
BF16 = jnp.bfloat16
HIGHEST = lax.Precision.HIGHEST

D_MODEL = 1024
CHUNK = 64
N_PAST_CHUNKS = 8
BAND = (N_PAST_CHUNKS + 1) * CHUNK
A_HEADS = 8
A_HEAD_DIM = 64
A_WIDTH = A_HEADS * A_HEAD_DIM
REL_CLIP = 128
G_HEADS = 4
G_DK = 64
G_DV = 128
G_KW = G_HEADS * G_DK
G_WIDTH = G_HEADS * G_DV
G_GATE_RANK = 16
G_GATE_NORM = 16.0
G_BLOCK = 16
N_MEM = 256
X_HEADS = 4
X_HEAD_DIM = D_MODEL // X_HEADS
P_HEADS = 8
P_NKEYS = 128
P_QDIM = 256
P_HALF = P_QDIM // 2
P_TOPK = 16
EPS = 1e-6
NEG = -1e30

LANES = 128
SUBLANES = 8
VMEM_LIMIT = 56 * 1024 * 1024

TOKEN_TILE = 256
GLA_TILE = 256
PEER_TOKEN_TILE = 1024
PEER_EXPERT_TILE = 1024


def _params(*sem):
    return pltpu.CompilerParams(dimension_semantics=sem, vmem_limit_bytes=VMEM_LIMIT)


def _full(shape):
    return pl.BlockSpec(shape, lambda *_: (0,) * len(shape))


def _rms(x, g):
    return x * lax.rsqrt(jnp.mean(x * x, axis=-1, keepdims=True) + EPS) * g


def _seg_mean_sq(z, seg_avg):
    sq = z * z
    hi = sq.astype(BF16)
    lo = (sq - hi.astype(F32)).astype(BF16)
    return (jnp.dot(hi, seg_avg, preferred_element_type=F32)
            + jnp.dot(lo, seg_avg, preferred_element_type=F32))


def _seg_avg_matrix(width, seg):
    r = jnp.arange(width) // seg
    return jnp.where(r[:, None] == r[None, :], 1.0 / seg, 0.0).astype(BF16)


def _mixer_in_kernel(x_ref, gmix_ref, w_ref, gq_ref, gk_ref, avg_ref, wg2_ref, bg2_ref,
                     qa_ref, kaf_ref, kab_ref, vaf_ref, vab_ref, qg_ref, kg_ref, vg_ref, la_ref, sr_ref):
    h = _rms(x_ref[...], gmix_ref[...]).astype(BF16)
    z = jnp.dot(h, w_ref[...], preferred_element_type=F32)
    o = 0
    qa = z[:, o:o + A_WIDTH]; o += A_WIDTH
    ka = z[:, o:o + A_WIDTH]; o += A_WIDTH
    va = z[:, o:o + A_WIDTH]; o += A_WIDTH
    qg = z[:, o:o + G_KW]; o += G_KW
    kg = z[:, o:o + G_KW]; o += G_KW
    vg = z[:, o:o + G_WIDTH]; o += G_WIDTH
    r = z[:, o:o + G_WIDTH]; o += G_WIDTH
    gl = z[:, o:o + LANES]
    avg = avg_ref[...]
    qa = qa * lax.rsqrt(_seg_mean_sq(qa, avg) + EPS) * gq_ref[...]
    ka = ka * lax.rsqrt(_seg_mean_sq(ka, avg) + EPS) * gk_ref[...]
    qa_ref[...] = (qa * (A_HEAD_DIM ** -0.5)).astype(BF16)
    kaf_ref[...] = ka
    kab_ref[...] = ka.astype(BF16)
    vaf_ref[...] = va
    vab_ref[...] = va.astype(BF16)
    qg_ref[...] = qg * (G_DK ** -0.5)
    kg_ref[...] = kg
    vg_ref[...] = vg
    pre = jnp.dot(gl, wg2_ref[...], precision=HIGHEST, preferred_element_type=F32) + bg2_ref[...]
    la_ref[...] = (jnp.minimum(pre, 0.0) - jnp.log1p(jnp.exp(-jnp.abs(pre)))) / G_GATE_NORM
    sr_ref[...] = r * jax.nn.sigmoid(r)


def _mixer_in(x, g_mix, w_in, g_q, g_k, w_gate2, b_gate2):
    n = x.shape[0]
    c_gl = 3 * A_WIDTH + 2 * G_KW + G_WIDTH
    w_cat = jnp.concatenate(
        [w_in[:, :c_gl], w_in[:, c_gl + G_GATE_RANK:],
         jnp.pad(w_in[:, c_gl:c_gl + G_GATE_RANK], ((0, 0), (0, LANES - G_GATE_RANK)))], axis=1).astype(BF16)
    wg2 = jnp.pad(w_gate2, ((0, LANES - G_GATE_RANK), (0, 0)))
    wcols = w_cat.shape[1]
    tm = TOKEN_TILE
    row = lambda w: pl.BlockSpec((tm, w), lambda i: (i, 0))
    outs = [(A_WIDTH, BF16), (A_WIDTH, F32), (A_WIDTH, BF16), (A_WIDTH, F32), (A_WIDTH, BF16),
            (G_KW, F32), (G_KW, F32), (G_WIDTH, F32), (G_KW, F32), (G_WIDTH, F32)]
    return pl.pallas_call(
        _mixer_in_kernel,
        grid=(n // tm,),
        in_specs=[row(D_MODEL), _full((1, D_MODEL)), _full((D_MODEL, wcols)), _full((1, A_WIDTH)),
                  _full((1, A_WIDTH)), _full((A_WIDTH, A_WIDTH)), _full((LANES, G_KW)), _full((1, G_KW))],
        out_specs=[row(w) for w, _ in outs],
        out_shape=[jax.ShapeDtypeStruct((n, w), dt) for w, dt in outs],
        compiler_params=_params("parallel"),
        name="mixer_in",
    )(x, g_mix.reshape(1, -1), w_cat, jnp.tile(g_q, A_HEADS).reshape(1, -1), jnp.tile(g_k, A_HEADS).reshape(1, -1),
      _seg_avg_matrix(A_WIDTH, A_HEAD_DIM), wg2, b_gate2.reshape(1, -1))


def _attend(q, segments):
    tq = q.shape[0]
    lane = lax.broadcasted_iota(jnp.int32, (tq, LANES), 1)
    outs = []
    for pair in range(A_HEADS // 2):
        sl = slice(pair * LANES, (pair + 1) * LANES)
        q2 = q[:, sl]
        halves = []
        for half in range(2):
            head = 2 * pair + half
            mine = (lane < A_HEAD_DIM) if half == 0 else (lane >= A_HEAD_DIM)
            qm = jnp.where(mine, q2, jnp.zeros_like(q2))
            scores = [lax.dot_general(qm, k[:, sl], (((1,), (1,)), ((), ())), preferred_element_type=F32) + bias(head)
                      for k, _, bias in segments]
            m = functools.reduce(jnp.maximum, [jnp.max(s, axis=-1, keepdims=True) for s in scores])
            es = [jnp.exp(s - m) for s in scores]
            inv = 1.0 / functools.reduce(jnp.add, [jnp.sum(e, axis=-1, keepdims=True) for e in es])
            o = functools.reduce(jnp.add, [
                jnp.dot((e * inv).astype(BF16), v[:, sl], preferred_element_type=F32)
                for e, (_, v, _) in zip(es, segments)])
            halves.append(o)
        outs.append(jnp.where(lane < A_HEAD_DIM, halves[0], halves[1]))
    return jnp.concatenate(outs, axis=1)


ATT_GROUP = 4
ATT_ROWS = ATT_GROUP * CHUNK
ATT_WINDOW = (N_PAST_CHUNKS + ATT_GROUP) * CHUNK
ATT_CASES = -(-N_PAST_CHUNKS // ATT_GROUP) + 1
BIAS_WRAP = 1024


def _attn_prompt_kernel(q_ref, k_ref, v_ref, tb_ref, o_ref, bias_ref):
    g = pl.program_id(1)
    first = jnp.maximum(g * ATT_GROUP - N_PAST_CHUNKS, 0)
    start = pl.multiple_of(first * CHUNK, CHUNK)

    for case in range(ATT_CASES):
        @pl.when(g == case)
        def _(case=case):
            off = min(case * ATT_GROUP, N_PAST_CHUNKS) * CHUNK
            q_chunk = (off + lax.broadcasted_iota(jnp.int32, (ATT_ROWS, ATT_WINDOW), 0)) // CHUNK
            k_chunk = lax.broadcasted_iota(jnp.int32, (ATT_ROWS, ATT_WINDOW), 1) // CHUNK
            valid = (k_chunk <= q_chunk) & (k_chunk >= q_chunk - N_PAST_CHUNKS)
            for head in range(A_HEADS):
                table = jnp.broadcast_to(tb_ref[head:head + 1, :], (ATT_ROWS, BIAS_WRAP))
                rolled = pltpu.roll(table, off, 1, stride=1, stride_axis=0)
                bias_ref[head] = jnp.where(valid, rolled[:, :ATT_WINDOW], NEG)

    kb = k_ref[0, pl.ds(start, ATT_WINDOW), :]
    vb = v_ref[0, pl.ds(start, ATT_WINDOW), :]
    o_ref[...] = _attend(q_ref[...], [(kb, vb, lambda head: bias_ref[head])]).astype(o_ref.dtype)


def _wrapped_bias_table(rel_bias):
    u = jnp.arange(BIAS_WRAP)
    dist = jnp.where(u < CHUNK, -u, BIAS_WRAP - u)
    return rel_bias[:, jnp.clip(dist, -REL_CLIP, REL_CLIP) + REL_CLIP].astype(F32)


def _attn_prompt(qa, ka, va, rel_bias, batch, seq):
    ng = seq // ATT_ROWS
    assert seq % ATT_ROWS == 0 and ng >= ATT_CASES and seq >= ATT_WINDOW
    kv = pl.BlockSpec((1, seq, A_WIDTH), lambda b, g: (b, 0, 0))
    return pl.pallas_call(
        _attn_prompt_kernel,
        grid=(batch, ng),
        in_specs=[pl.BlockSpec((ATT_ROWS, A_WIDTH), lambda b, g: (b * ng + g, 0)), kv, kv,
                  _full((A_HEADS, BIAS_WRAP))],
        out_specs=pl.BlockSpec((ATT_ROWS, A_WIDTH), lambda b, g: (b * ng + g, 0)),
        out_shape=jax.ShapeDtypeStruct((batch * seq, A_WIDTH), BF16),
        scratch_shapes=[pltpu.VMEM((A_HEADS, ATT_ROWS, ATT_WINDOW), F32)],
        compiler_params=_params("arbitrary", "arbitrary"),
        name="attn_prompt",
    )(qa, ka.reshape(batch, seq, A_WIDTH), va.reshape(batch, seq, A_WIDTH), _wrapped_bias_table(rel_bias))


def _attn_sample_kernel(q_ref, kn_ref, vn_ref, kc_ref, vc_ref, tb_ref, o_ref, *, first_valid):
    t = q_ref.shape[0]
    r = kc_ref.shape[1] // A_HEADS
    q = q_ref[...]
    kn = kn_ref[...]
    vn = vn_ref[...]
    valid_c = lax.broadcasted_iota(jnp.int32, (t, r), 1) >= first_valid
    nt = (((1,), (1,)), ((), ()))
    outs = []
    for head in range(A_HEADS):
        sl = slice(head * A_HEAD_DIM, (head + 1) * A_HEAD_DIM)
        kc = kc_ref[0, pl.ds(head, r, stride=A_HEADS), :].astype(BF16)
        vc = vc_ref[0, pl.ds(head, r, stride=A_HEADS), :].astype(BF16)
        table = jnp.broadcast_to(tb_ref[head:head + 1, :], (t, BIAS_WRAP))
        bias_c = pltpu.roll(table, r % BIAS_WRAP, 1, stride=1, stride_axis=0)[:, :r]
        bias_n = pltpu.roll(table, 0, 1, stride=1, stride_axis=0)[:, :t]
        sc = lax.dot_general(q[:, sl], kc, nt, preferred_element_type=F32) + jnp.where(valid_c, bias_c, NEG)
        sn = lax.dot_general(q[:, sl], kn[:, sl], nt, preferred_element_type=F32) + bias_n
        m = jnp.maximum(jnp.max(sc, axis=-1, keepdims=True), jnp.max(sn, axis=-1, keepdims=True))
        ec = jnp.exp(sc - m)
        en = jnp.exp(sn - m)
        inv = 1.0 / (jnp.sum(ec, axis=-1, keepdims=True) + jnp.sum(en, axis=-1, keepdims=True))
        outs.append(jnp.dot((ec * inv).astype(BF16), vc, preferred_element_type=F32)
                    + jnp.dot((en * inv).astype(BF16), vn[:, sl], preferred_element_type=F32))
    o_ref[...] = jnp.concatenate(outs, axis=1).astype(o_ref.dtype)


def _attn_sample(qa, ka, va, cache_k, cache_v, rel_bias, batch, t, past_len):
    r = cache_k.shape[1]
    q_chunk = past_len // CHUNK
    assert (past_len + t - 1) // CHUNK == q_chunk and r <= BAND
    first_valid = max(0, (q_chunk - N_PAST_CHUNKS) * CHUNK - (past_len - r))
    tok = pl.BlockSpec((t, A_WIDTH), lambda b: (b, 0))
    cache = pl.BlockSpec((1, r * A_HEADS, A_HEAD_DIM), lambda b: (b, 0, 0))
    return pl.pallas_call(
        functools.partial(_attn_sample_kernel, first_valid=first_valid),
        grid=(batch,),
        in_specs=[tok, tok, tok, cache, cache, _full((A_HEADS, BIAS_WRAP))],
        out_specs=tok,
        out_shape=jax.ShapeDtypeStruct((batch * t, A_WIDTH), BF16),
        compiler_params=_params("parallel"),
        name="attn_sample",
    )(qa, ka, va, cache_k.reshape(batch, r * A_HEADS, A_HEAD_DIM), cache_v.reshape(batch, r * A_HEADS, A_HEAD_DIM),
      _wrapped_bias_table(rel_bias))


def _gla_kernel(q_ref, k_ref, v_ref, la_ref, s0_ref, tri_ref, expand_ref, bdmask_ref, o_ref, sfin_ref, state_ref):
    tb = q_ref.shape[0]
    nb = tb // G_BLOCK

    @pl.when(pl.program_id(1) == 0)
    def _():
        state_ref[...] = s0_ref[0]

    q = q_ref[...]
    k = k_ref[...]
    v = v_ref[...]
    cum = jnp.dot(tri_ref[...], la_ref[...], precision=HIGHEST, preferred_element_type=F32)
    q3 = q.reshape(nb, G_BLOCK, G_KW)
    k3 = k.reshape(nb, G_BLOCK, G_KW)
    v3 = v.reshape(nb, G_BLOCK, G_WIDTH)
    cum3 = cum.reshape(nb, G_BLOCK, G_KW)
    row = lax.broadcasted_iota(jnp.int32, (nb, G_BLOCK, G_KW), 1)
    expand = expand_ref[...]

    intra = jnp.zeros((nb, G_BLOCK, G_WIDTH), F32)
    for s in range(G_BLOCK):
        k_s = jnp.broadcast_to(k3[:, s:s + 1, :], (nb, G_BLOCK, G_KW))
        c_s = jnp.broadcast_to(cum3[:, s:s + 1, :], (nb, G_BLOCK, G_KW))
        v_s = jnp.broadcast_to(v3[:, s:s + 1, :], (nb, G_BLOCK, G_WIDTH))
        prod = q3 * k_s * jnp.exp(jnp.where(row >= s, cum3 - c_s, -jnp.inf))
        att = jnp.dot(prod.reshape(tb, G_KW).astype(BF16), expand, preferred_element_type=F32)
        intra = intra + att.reshape(nb, G_BLOCK, G_WIDTH) * v_s
    o_ref[...] = intra.reshape(tb, G_WIDTH)

    last3 = jnp.broadcast_to(cum3[:, G_BLOCK - 1:G_BLOCK, :], (nb, G_BLOCK, G_KW))
    q_dec = (q3 * jnp.exp(cum3)).reshape(tb, G_KW).astype(BF16)
    k_dec = (k3 * jnp.exp(last3 - cum3)).reshape(tb, G_KW).astype(BF16)
    carry = jnp.exp(last3).reshape(tb, G_KW)
    vb = v.astype(BF16)
    bdmask = bdmask_ref[...]
    for j in range(nb):
        rows = slice(j * G_BLOCK, (j + 1) * G_BLOCK)
        state = state_ref[...]
        o_ref[rows, :] += lax.dot_general(q_dec[rows], state.astype(BF16), (((1,), (1,)), ((), ())),
                                          preferred_element_type=F32)
        upd = lax.dot_general(vb[rows], k_dec[rows], (((0,), (0,)), ((), ())), preferred_element_type=F32)
        state_ref[...] = carry[j * G_BLOCK:j * G_BLOCK + 1, :] * state + upd * bdmask

    @pl.when(pl.program_id(1) == pl.num_programs(1) - 1)
    def _():
        sfin_ref[0] = state_ref[...]


def _gla(qg, kg, vg, la, s0, batch, t, tile):
    nt = t // tile
    hsel = jnp.arange(G_KW) // G_DK
    vsel = jnp.arange(G_WIDTH) // G_DV
    same = hsel[:, None] == vsel[None, :]
    expand = same.astype(BF16)
    bdmask = same.T.astype(F32)
    s0_t = jnp.swapaxes(s0, 2, 3).reshape(batch, G_WIDTH, G_DK)
    s0_bd = jnp.where(same.T[None], jnp.tile(s0_t, (1, 1, G_HEADS)), 0.0)
    rr = jnp.arange(tile)
    tri = ((rr[:, None] // G_BLOCK == rr[None, :] // G_BLOCK) & (rr[None, :] <= rr[:, None])).astype(F32)
    tok = lambda w: pl.BlockSpec((tile, w), lambda b, i: (b * nt + i, 0))
    st = pl.BlockSpec((1, G_WIDTH, G_KW), lambda b, i: (b, 0, 0))
    o, s_bd = pl.pallas_call(
        _gla_kernel,
        grid=(batch, nt),
        in_specs=[tok(G_KW), tok(G_KW), tok(G_WIDTH), tok(G_KW), st,
                  _full((tile, tile)), _full((G_KW, G_WIDTH)), _full((G_WIDTH, G_KW))],
        out_specs=[tok(G_WIDTH), st],
        out_shape=[jax.ShapeDtypeStruct((batch * t, G_WIDTH), F32),
                   jax.ShapeDtypeStruct((batch, G_WIDTH, G_KW), F32)],
        scratch_shapes=[pltpu.VMEM((G_WIDTH, G_KW), F32)],
        compiler_params=_params("parallel", "arbitrary"),
        name="gla",
    )(qg, kg, vg, la, s0_bd, tri, expand, bdmask)
    s_fin = jnp.stack([s_bd[:, h * G_DV:(h + 1) * G_DV, h * G_DK:(h + 1) * G_DK] for h in range(G_HEADS)], axis=1)
    return o, jnp.swapaxes(s_fin, 2, 3)


def _mixer_out_kernel(x_ref, oa_ref, og_ref, sr_ref, gg_ref, avg_ref, wa_ref, wg_ref, y_ref):
    og = og_ref[...]
    og = og * lax.rsqrt(_seg_mean_sq(og, avg_ref[...]) + EPS) * gg_ref[...] * sr_ref[...]
    y = jnp.dot(oa_ref[...], wa_ref[...], preferred_element_type=F32)
    y = y + jnp.dot(og.astype(BF16), wg_ref[...], preferred_element_type=F32)
    y_ref[...] = x_ref[...] + y


def _mixer_out(x, oa, og, sr, g_gla_out, w_out):
    n = x.shape[0]
    tm = TOKEN_TILE
    row = lambda w: pl.BlockSpec((tm, w), lambda i: (i, 0))
    w_out = w_out.astype(BF16)
    return pl.pallas_call(
        _mixer_out_kernel,
        grid=(n // tm,),
        in_specs=[row(D_MODEL), row(A_WIDTH), row(G_WIDTH), row(G_WIDTH), _full((1, G_WIDTH)),
                  _full((G_WIDTH, G_WIDTH)), _full((A_WIDTH, D_MODEL)), _full((G_WIDTH, D_MODEL))],
        out_specs=row(D_MODEL),
        out_shape=jax.ShapeDtypeStruct((n, D_MODEL), F32),
        compiler_params=_params("parallel"),
        name="mixer_out",
    )(x, oa, og, sr, jnp.tile(g_gla_out, G_HEADS).reshape(1, -1), _seg_avg_matrix(G_WIDTH, G_DV),
      w_out[:A_WIDTH], w_out[A_WIDTH:])


def _mem_kv_kernel(m_ref, gm_ref, wk_ref, wv_ref, gk_ref, avg_ref, k_ref, v_ref):
    m = _rms(m_ref[...], gm_ref[...]).astype(BF16)
    k = jnp.dot(m, wk_ref[...], preferred_element_type=F32)
    k_ref[...] = k * lax.rsqrt(_seg_mean_sq(k, avg_ref[...]) + EPS) * gk_ref[...]
    v_ref[...] = jnp.dot(m, wv_ref[...], preferred_element_type=F32)


def _mem_kv(mem, g_mem, w_ck, w_cv, g_ck):
    n = mem.shape[0]
    tm = TOKEN_TILE
    row = pl.BlockSpec((tm, D_MODEL), lambda i: (i, 0))
    sq = _full((D_MODEL, D_MODEL))
    vec = _full((1, D_MODEL))
    return pl.pallas_call(
        _mem_kv_kernel,
        grid=(n // tm,),
        in_specs=[row, vec, sq, sq, vec, sq],
        out_specs=[row, row],
        out_shape=[jax.ShapeDtypeStruct((n, D_MODEL), F32)] * 2,
        compiler_params=_params("parallel"),
        name="mem_kv",
    )(mem, g_mem.reshape(1, -1), w_ck.astype(BF16), w_cv.astype(BF16), jnp.tile(g_ck, X_HEADS).reshape(1, -1),
      _seg_avg_matrix(D_MODEL, X_HEAD_DIM))


def _cross_kernel(x_ref, gc_ref, wq_ref, gq_ref, avg_ref, mk_ref, mv_ref, wo_ref, y_ref):
    x = x_ref[...]
    h = _rms(x, gc_ref[...]).astype(BF16)
    q = jnp.dot(h, wq_ref[...], preferred_element_type=F32)
    q = q * lax.rsqrt(_seg_mean_sq(q, avg_ref[...]) + EPS) * gq_ref[...] * (X_HEAD_DIM ** -0.5)
    q = q.astype(BF16)
    mk = mk_ref[0].astype(BF16)
    mv = mv_ref[0].astype(BF16)
    outs = []
    for head in range(X_HEADS):
        sl = slice(head * X_HEAD_DIM, (head + 1) * X_HEAD_DIM)
        s = lax.dot_general(q[:, sl], mk[:, sl], (((1,), (1,)), ((), ())), preferred_element_type=F32)
        e = jnp.exp(s - jnp.max(s, axis=-1, keepdims=True))
        p = (e * (1.0 / jnp.sum(e, axis=-1, keepdims=True))).astype(BF16)
        outs.append(jnp.dot(p, mv[:, sl], preferred_element_type=F32))
    o = jnp.concatenate(outs, axis=1).astype(BF16)
    y_ref[...] = x + jnp.dot(o, wo_ref[...], preferred_element_type=F32)


def _cross(x, mem_k, mem_v, g_cross, w_cq, g_cq, w_co, batch, t, tile):
    nt = t // tile
    row = pl.BlockSpec((tile, D_MODEL), lambda b, i: (b * nt + i, 0))
    mem = pl.BlockSpec((1, N_MEM, D_MODEL), lambda b, i: (b, 0, 0))
    sq = _full((D_MODEL, D_MODEL))
    vec = _full((1, D_MODEL))
    return pl.pallas_call(
        _cross_kernel,
        grid=(batch, nt),
        in_specs=[row, vec, sq, vec, sq, mem, mem, sq],
        out_specs=row,
        out_shape=jax.ShapeDtypeStruct((batch * t, D_MODEL), F32),
        compiler_params=_params("parallel", "parallel"),
        name="cross",
    )(x, g_cross.reshape(1, -1), w_cq.astype(BF16), jnp.tile(g_cq, X_HEADS).reshape(1, -1),
      _seg_avg_matrix(D_MODEL, X_HEAD_DIM), mem_k, mem_v, w_co.astype(BF16))


CAND_ROWS = 64


def _sorting_network(n):
    pairs = []
    p = 1
    while p < n:
        k = p
        while k >= 1:
            for j in range(k % p, n - k, 2 * k):
                for i in range(min(k, n - j - k)):
                    if (i + j) // (2 * p) == (i + j + k) // (2 * p):
                        pairs.append((i + j, i + j + k))
            k //= 2
        p *= 2
    return pairs


def _top_values_tiled(s, count):
    nparts = s.shape[0] // SUBLANES
    parts = [s[SUBLANES * k:SUBLANES * (k + 1), :] for k in range(nparts)]
    for i, j in _sorting_network(nparts):
        parts[i], parts[j] = jnp.maximum(parts[i], parts[j]), jnp.minimum(parts[i], parts[j])
    which = lax.broadcasted_iota(jnp.int32, parts[0].shape, 0).astype(F32)
    exhausted = jnp.full(parts[0].shape, -jnp.inf, F32)
    out = []
    for r in range(count):
        mx = jnp.max(parts[0], axis=0, keepdims=True)
        first = jnp.min(jnp.where(parts[0] == mx, which, float(SUBLANES)), axis=0, keepdims=True)
        pop = which == first
        out.append(mx)
        for k in range(min(count - r - 1, nparts)):
            parts[k] = jnp.where(pop, parts[k + 1] if k + 1 < nparts else exhausted, parts[k])
    return jnp.concatenate(out, axis=0)


def _route_kernel(x_ref, gf_ref, whi_ref, wlo_ref, sk1_ref, sk2_ref, ht_ref, r2_ref, e2_ref, n1_ref, c1_ref):
    hf = _rms(x_ref[...], gf_ref[...])
    ht_ref[...] = jnp.transpose(hf).astype(BF16)
    h_hi = hf.astype(BF16)
    h_lo = (hf - h_hi.astype(F32)).astype(BF16)
    qry = (jnp.dot(h_hi, whi_ref[...], preferred_element_type=F32)
           + jnp.dot(h_hi, wlo_ref[...], preferred_element_type=F32)
           + jnp.dot(h_lo, whi_ref[...], preferred_element_type=F32))
    nt = (((1,), (1,)), ((), ()))
    nk = P_TOPK + 1
    t = qry.shape[0]
    for head in range(P_HEADS):
        q1 = qry[:, head * P_QDIM:head * P_QDIM + P_HALF]
        q2 = qry[:, head * P_QDIM + P_HALF:(head + 1) * P_QDIM]
        s1 = lax.dot_general(sk1_ref[...], q1, nt, precision=HIGHEST, preferred_element_type=F32)
        s2 = lax.dot_general(sk2_ref[...], q2, nt, precision=HIGHEST, preferred_element_type=F32)
        v1 = _top_values_tiled(s1, nk)
        v2 = _top_values_tiled(s2, nk)
        slabs = [v1[a:a + 1, :] + v2[:nk // (a + 1), :] for a in range(nk)]
        pad = CAND_ROWS - sum(sl.shape[0] for sl in slabs)
        cand = jnp.concatenate(slabs + [jnp.full((pad, t), -jnp.inf, F32)], axis=0)
        top = _top_values_tiled(cand, nk)
        m1 = v1[0:1, :]
        m2 = v2[0:1, :]
        z = jnp.sum(jnp.exp(top[:P_TOPK, :] - (m1 + m2)), axis=0, keepdims=True)
        tau = 0.5 * (top[P_TOPK - 1:P_TOPK, :] + top[P_TOPK:nk, :])
        thr = tau - s1
        n1 = jnp.zeros_like(s1)
        rank2 = jnp.zeros_like(s2)
        for b in range(P_TOPK):
            n1 = n1 + jnp.where(v2[b:b + 1, :] >= thr, 1.0, 0.0)
            rank2 = rank2 + jnp.where(v2[b:b + 1, :] > s2, 1.0, 0.0)
        r2_ref[head] = rank2.astype(BF16)
        e2_ref[head] = jnp.exp(s2 - m2).astype(BF16)
        n1_ref[head] = n1
        c1_ref[head] = jnp.exp(s1 - m1) * (1.0 / z)


def _route(x, g_ffn, w_pq, sub_keys1, sub_keys2):
    n = x.shape[0]
    w_hi = w_pq.astype(BF16)
    w_lo = (w_pq - w_hi.astype(F32)).astype(BF16)
    tm = TOKEN_TILE
    keys = pl.BlockSpec((P_HEADS, P_NKEYS, tm), lambda i: (0, 0, i))
    tab = lambda dt: jax.ShapeDtypeStruct((P_HEADS, P_NKEYS, n), dt)
    return pl.pallas_call(
        _route_kernel,
        grid=(n // tm,),
        in_specs=[pl.BlockSpec((tm, D_MODEL), lambda i: (i, 0)), _full((1, D_MODEL)),
                  _full((D_MODEL, P_HEADS * P_QDIM)), _full((D_MODEL, P_HEADS * P_QDIM)),
                  _full((P_NKEYS, P_HALF)), _full((P_NKEYS, P_HALF))],
        out_specs=[pl.BlockSpec((D_MODEL, tm), lambda i: (0, i)), keys, keys, keys, keys],
        out_shape=[jax.ShapeDtypeStruct((D_MODEL, n), BF16), tab(BF16), tab(BF16), tab(F32), tab(F32)],
        compiler_params=_params("parallel"),
        name="peer_route",
    )(x, g_ffn.reshape(1, -1), w_hi, w_lo, sub_keys1, sub_keys2)


def _peer_kernel(ht_ref, x_ref, u_ref, vt_ref, r2_ref, e2_ref, n1_ref, c1_ref, y_ref, acc_ref):
    j = pl.program_id(1)

    @pl.when(j == 0)
    def _():
        acc_ref[...] = jnp.zeros_like(acc_ref)

    t = ht_ref.shape[1]
    n_i1 = PEER_EXPERT_TILE // P_NKEYS
    zero = jnp.zeros((), BF16)
    a = jnp.dot(u_ref[...], ht_ref[...], preferred_element_type=F32)
    act = (a * (1.0 + lax.erf(a * (1.0 / math.sqrt(2.0))))).astype(BF16)
    gates = []
    for r in range(n_i1):
        g = jnp.zeros((P_NKEYS, t), BF16)
        for head in range(P_HEADS):
            n1 = n1_ref[head, pl.ds(j * n_i1 + r, 1), :].astype(BF16)
            c1 = (0.5 * c1_ref[head, pl.ds(j * n_i1 + r, 1), :]).astype(BF16)
            g = g + jnp.where(r2_ref[head] < n1, e2_ref[head], zero) * c1
        gates.append(g)
    ga = act * jnp.concatenate(gates, axis=0)
    acc_ref[...] += jnp.dot(vt_ref[0], ga, preferred_element_type=F32)

    @pl.when(j == pl.num_programs(1) - 1)
    def _():
        y_ref[...] = x_ref[...] + jnp.transpose(acc_ref[...])


def _peer(ht, x, r2, e2, n1, c1, u, vt, tt):
    n = x.shape[0]
    et = PEER_EXPERT_TILE
    ne = u.shape[0] // et
    keys = pl.BlockSpec((P_HEADS, P_NKEYS, tt), lambda i, j: (0, 0, i))
    once = pl.Buffered(1)
    return pl.pallas_call(
        _peer_kernel,
        grid=(n // tt, ne),
        in_specs=[pl.BlockSpec((D_MODEL, tt), lambda i, j: (0, i)),
                  pl.BlockSpec((tt, D_MODEL), lambda i, j: (i, 0), pipeline_mode=once),
                  pl.BlockSpec((et, D_MODEL), lambda i, j: (j, 0)),
                  pl.BlockSpec((1, D_MODEL, et), lambda i, j: (j, 0, 0)),
                  keys, keys, keys, keys],
        out_specs=pl.BlockSpec((tt, D_MODEL), lambda i, j: (i, 0), pipeline_mode=once),
        out_shape=jax.ShapeDtypeStruct((n, D_MODEL), F32),
        scratch_shapes=[pltpu.VMEM((D_MODEL, tt), F32)],
        compiler_params=_params("parallel", "arbitrary"),
        name="peer_experts",
    )(ht, x, u, vt, r2, e2, n1, c1)


def _layer(xp, xs, cache_k, cache_v, state, cmem_k, cmem_v, mem_prompt, past_len,
           g_mix, w_in, g_q, g_k, rel_bias, w_gate2, b_gate2, g_gla_out, w_out,
           g_cross, g_mem, w_cq, w_ck, w_cv, g_cq, g_ck, w_co, g_ffn, w_pq, sub_keys1, sub_keys2, expert_u, expert_v):
    bp, sp, _ = xp.shape
    bs, ts, _ = xs.shape
    xp = xp.reshape(bp * sp, D_MODEL)
    xs = xs.reshape(bs * ts, D_MODEL)
    u = expert_u.astype(BF16)
    ne = expert_v.shape[0] // PEER_EXPERT_TILE
    vt = jnp.swapaxes(expert_v.reshape(ne, PEER_EXPERT_TILE, D_MODEL), 1, 2).astype(BF16)
    heads = lambda a, b, t: a.reshape(b, t, A_HEADS, A_HEAD_DIM)

    def tail(x, mem_k, mem_v, batch, t, cross_tile, peer_tile):
        x = _cross(x, mem_k, mem_v, g_cross, w_cq, g_cq, w_co, batch, t, cross_tile)
        ht, r2, e2, n1, c1 = _route(x, g_ffn, w_pq, sub_keys1, sub_keys2)
        return _peer(ht, x, r2, e2, n1, c1, u, vt, peer_tile)

    qa, ka_f, ka_b, va_f, va_b, qg, kg, vg, la, sr = _mixer_in(xp, g_mix, w_in, g_q, g_k, w_gate2, b_gate2)
    oa = _attn_prompt(qa, ka_b, va_b, rel_bias, bp, sp)
    og, gla_p = _gla(qg, kg, vg, la, jnp.zeros((bp, G_HEADS, G_DK, G_DV), F32), bp, sp, GLA_TILE)
    xp = _mixer_out(xp, oa, og, sr, g_gla_out, w_out)
    mk, mv = _mem_kv(mem_prompt.reshape(bp * N_MEM, D_MODEL), g_mem, w_ck, w_cv, g_ck)
    xp = tail(xp, mk.reshape(bp, N_MEM, D_MODEL), mv.reshape(bp, N_MEM, D_MODEL), bp, sp, TOKEN_TILE,
              PEER_TOKEN_TILE)
    keep = min(BAND, sp)
    outs_p = (heads(ka_f, bp, sp)[:, -keep:], heads(va_f, bp, sp)[:, -keep:], gla_p,
              mk.reshape(bp, N_MEM, X_HEADS, X_HEAD_DIM), mv.reshape(bp, N_MEM, X_HEADS, X_HEAD_DIM))

    qa, ka_f, ka_b, va_f, va_b, qg, kg, vg, la, sr = _mixer_in(xs, g_mix, w_in, g_q, g_k, w_gate2, b_gate2)
    oa = _attn_sample(qa, ka_b, va_b, cache_k, cache_v, rel_bias, bs, ts, past_len)
    og, gla_s = _gla(qg, kg, vg, la, state, bs, ts, ts)
    xs = _mixer_out(xs, oa, og, sr, g_gla_out, w_out)
    xs = tail(xs, cmem_k.reshape(bs, N_MEM, D_MODEL), cmem_v.reshape(bs, N_MEM, D_MODEL), bs, ts, ts, bs * ts)
    outs_s = (heads(ka_f, bs, ts), heads(va_f, bs, ts), gla_s)

    return (xp.reshape(bp, sp, D_MODEL), xs.reshape(bs, ts, D_MODEL)) + outs_p + outs_s


def kernel(x_prompt, x_sample, cache_att_k, cache_att_v, state_gla, cache_mem_k, cache_mem_v, mem_prompt, g_mix, w_in, g_q, g_k, rel_bias, w_gate2, b_gate2, g_gla_out, w_out, g_cross, g_mem, w_cq, w_ck, w_cv, g_cq, g_ck, w_co, g_ffn, w_pq, sub_keys1, sub_keys2, expert_u, expert_v):
    depth = w_in.shape[0]
    past_len = 4096
    xp, xs = x_prompt, x_sample
    per_layer = []
    for l in range(depth):
        outs = _layer(xp, xs, cache_att_k[l], cache_att_v[l], state_gla[l], cache_mem_k[l], cache_mem_v[l],
                      mem_prompt, past_len,
                      g_mix[l], w_in[l], g_q[l], g_k[l], rel_bias[l], w_gate2[l], b_gate2[l], g_gla_out[l], w_out[l],
                      g_cross[l], g_mem[l], w_cq[l], w_ck[l], w_cv[l], g_cq[l], g_ck[l], w_co[l],
                      g_ffn[l], w_pq[l], sub_keys1[l], sub_keys2[l], expert_u[l], expert_v[l])
        xp, xs = outs[0], outs[1]
        per_layer.append(outs[2:])
    stacked = [jnp.stack([p[i] for p in per_layer]) for i in range(8)]
    return (xp, xs) + tuple(stacked)
```

```python
import functools
import math

import jax
import jax.numpy as jnp
from jax import lax
from jax.experimental import pallas as pl
from jax.experimental.pallas import tpu as pltpu

F32 = jnp.float32
BF16 = jnp.bfloat16
HIGHEST = lax.Precision.HIGHEST

D_MODEL = 1024
CHUNK = 64
N_PAST_CHUNKS = 8
BAND = (N_PAST_CHUNKS + 1) * CHUNK
A_HEADS = 8
A_HEAD_DIM = 64
A_WIDTH = A_HEADS * A_HEAD_DIM
REL_CLIP = 128
G_HEADS = 4
G_DK = 64
G_DV = 128
G_KW = G_HEADS * G_DK
G_WIDTH = G_HEADS * G_DV
G_GATE_RANK = 16
G_GATE_NORM = 16.0
G_BLOCK = 16
N_MEM = 256
X_HEADS = 4
X_HEAD_DIM = D_MODEL // X_HEADS
P_HEADS = 8
P_NKEYS = 128
P_QDIM = 256
P_HALF = P_QDIM // 2
P_TOPK = 16
EPS = 1e-6
NEG = -1e30

LANES = 128
SUBLANES = 8
VMEM_LIMIT = 56 * 1024 * 1024

TOKEN_TILE = 512
ROUTE_TILE = 256
GLA_TILE = 256
PEER_TOKEN_TILE = 1024
PEER_EXPERT_TILE = 1024


def _params(*sem):
    return pltpu.CompilerParams(dimension_semantics=sem, vmem_limit_bytes=VMEM_LIMIT)


def _full(shape):
    return pl.BlockSpec(shape, lambda *_: (0,) * len(shape))


def _rms(x, g):
    return x * lax.rsqrt(jnp.mean(x * x, axis=-1, keepdims=True) + EPS) * g


def _seg_mean_sq(z, seg_avg):
    sq = z * z
    hi = sq.astype(BF16)
    lo = (sq - hi.astype(F32)).astype(BF16)
    return (jnp.dot(hi, seg_avg, preferred_element_type=F32)
            + jnp.dot(lo, seg_avg, preferred_element_type=F32))


def _seg_avg_matrix(width, seg):
    r = jnp.arange(width) // seg
    return jnp.where(r[:, None] == r[None, :], 1.0 / seg, 0.0).astype(BF16)


def _mixer_in_kernel(x_ref, gmix_ref, w_ref, gq_ref, gk_ref, avg_ref, wg2_ref, bg2_ref,
                     qa_ref, kaf_ref, kab_ref, vaf_ref, vab_ref, qg_ref, kg_ref, vg_ref, la_ref, sr_ref):
    h = _rms(x_ref[...], gmix_ref[...]).astype(BF16)
    z = jnp.dot(h, w_ref[...], preferred_element_type=F32)
    o = 0
    qa = z[:, o:o + A_WIDTH]; o += A_WIDTH
    ka = z[:, o:o + A_WIDTH]; o += A_WIDTH
    va = z[:, o:o + A_WIDTH]; o += A_WIDTH
    qg = z[:, o:o + G_KW]; o += G_KW
    kg = z[:, o:o + G_KW]; o += G_KW
    vg = z[:, o:o + G_WIDTH]; o += G_WIDTH
    r = z[:, o:o + G_WIDTH]; o += G_WIDTH
    gl = z[:, o:o + LANES]
    avg = avg_ref[...]
    qa = qa * lax.rsqrt(_seg_mean_sq(qa, avg) + EPS) * gq_ref[...]
    ka = ka * lax.rsqrt(_seg_mean_sq(ka, avg) + EPS) * gk_ref[...]
    qa_ref[...] = (qa * (A_HEAD_DIM ** -0.5)).astype(BF16)
    kaf_ref[...] = ka
    kab_ref[...] = ka.astype(BF16)
    vaf_ref[...] = va
    vab_ref[...] = va.astype(BF16)
    qg_ref[...] = qg * (G_DK ** -0.5)
    kg_ref[...] = kg
    vg_ref[...] = vg
    pre = jnp.dot(gl, wg2_ref[...], precision=HIGHEST, preferred_element_type=F32) + bg2_ref[...]
    la_ref[...] = (jnp.minimum(pre, 0.0) - jnp.log1p(jnp.exp(-jnp.abs(pre)))) / G_GATE_NORM
    sr_ref[...] = r * jax.nn.sigmoid(r)


def _mixer_in(x, g_mix, w_in, g_q, g_k, w_gate2, b_gate2):
    n = x.shape[0]
    c_gl = 3 * A_WIDTH + 2 * G_KW + G_WIDTH
    w_cat = jnp.concatenate(
        [w_in[:, :c_gl], w_in[:, c_gl + G_GATE_RANK:],
         jnp.pad(w_in[:, c_gl:c_gl + G_GATE_RANK], ((0, 0), (0, LANES - G_GATE_RANK)))], axis=1).astype(BF16)
    wg2 = jnp.pad(w_gate2, ((0, LANES - G_GATE_RANK), (0, 0)))
    wcols = w_cat.shape[1]
    tm = TOKEN_TILE
    row = lambda w: pl.BlockSpec((tm, w), lambda i: (i, 0))
    outs = [(A_WIDTH, BF16), (A_WIDTH, F32), (A_WIDTH, BF16), (A_WIDTH, F32), (A_WIDTH, BF16),
            (G_KW, F32), (G_KW, F32), (G_WIDTH, F32), (G_KW, F32), (G_WIDTH, F32)]
    return pl.pallas_call(
        _mixer_in_kernel,
        grid=(n // tm,),
        in_specs=[row(D_MODEL), _full((1, D_MODEL)), _full((D_MODEL, wcols)), _full((1, A_WIDTH)),
                  _full((1, A_WIDTH)), _full((A_WIDTH, A_WIDTH)), _full((LANES, G_KW)), _full((1, G_KW))],
        out_specs=[row(w) for w, _ in outs],
        out_shape=[jax.ShapeDtypeStruct((n, w), dt) for w, dt in outs],
        compiler_params=_params("parallel"),
        name="mixer_in",
    )(x, g_mix.reshape(1, -1), w_cat, jnp.tile(g_q, A_HEADS).reshape(1, -1), jnp.tile(g_k, A_HEADS).reshape(1, -1),
      _seg_avg_matrix(A_WIDTH, A_HEAD_DIM), wg2, b_gate2.reshape(1, -1))


def _attend(q, segments):
    tq = q.shape[0]
    lane = lax.broadcasted_iota(jnp.int32, (tq, LANES), 1)
    outs = []
    for pair in range(A_HEADS // 2):
        sl = slice(pair * LANES, (pair + 1) * LANES)
        q2 = q[:, sl]
        halves = []
        for half in range(2):
            head = 2 * pair + half
            mine = (lane < A_HEAD_DIM) if half == 0 else (lane >= A_HEAD_DIM)
            qm = jnp.where(mine, q2, jnp.zeros_like(q2))
            scores = [lax.dot_general(qm, k[:, sl], (((1,), (1,)), ((), ())), preferred_element_type=F32) + bias(head)
                      for k, _, bias in segments]
            m = functools.reduce(jnp.maximum, [jnp.max(s, axis=-1, keepdims=True) for s in scores])
            es = [jnp.exp(s - m) for s in scores]
            inv = 1.0 / functools.reduce(jnp.add, [jnp.sum(e, axis=-1, keepdims=True) for e in es])
            o = functools.reduce(jnp.add, [
                jnp.dot((e * inv).astype(BF16), v[:, sl], preferred_element_type=F32)
                for e, (_, v, _) in zip(es, segments)])
            halves.append(o)
        outs.append(jnp.where(lane < A_HEAD_DIM, halves[0], halves[1]))
    return jnp.concatenate(outs, axis=1)


ATT_GROUP = 4
ATT_ROWS = ATT_GROUP * CHUNK
ATT_WINDOW = (N_PAST_CHUNKS + ATT_GROUP) * CHUNK
ATT_CASES = -(-N_PAST_CHUNKS // ATT_GROUP) + 1
BIAS_WRAP = 1024


def _attn_prompt_kernel(q_ref, k_ref, v_ref, tb_ref, o_ref, bias_ref):
    g = pl.program_id(1)
    first = jnp.maximum(g * ATT_GROUP - N_PAST_CHUNKS, 0)
    start = pl.multiple_of(first * CHUNK, CHUNK)

    for case in range(ATT_CASES):
        @pl.when(g == case)
        def _(case=case):
            off = min(case * ATT_GROUP, N_PAST_CHUNKS) * CHUNK
            q_chunk = (off + lax.broadcasted_iota(jnp.int32, (ATT_ROWS, ATT_WINDOW), 0)) // CHUNK
            k_chunk = lax.broadcasted_iota(jnp.int32, (ATT_ROWS, ATT_WINDOW), 1) // CHUNK
            valid = (k_chunk <= q_chunk) & (k_chunk >= q_chunk - N_PAST_CHUNKS)
            for head in range(A_HEADS):
                table = jnp.broadcast_to(tb_ref[head:head + 1, :], (ATT_ROWS, BIAS_WRAP))
                rolled = pltpu.roll(table, off, 1, stride=1, stride_axis=0)
                bias_ref[head] = jnp.where(valid, rolled[:, :ATT_WINDOW], NEG)

    kb = k_ref[0, pl.ds(start, ATT_WINDOW), :]
    vb = v_ref[0, pl.ds(start, ATT_WINDOW), :]
    o_ref[...] = _attend(q_ref[...], [(kb, vb, lambda head: bias_ref[head])]).astype(o_ref.dtype)


def _wrapped_bias_table(rel_bias):
    u = jnp.arange(BIAS_WRAP)
    dist = jnp.where(u < CHUNK, -u, BIAS_WRAP - u)
    return rel_bias[:, jnp.clip(dist, -REL_CLIP, REL_CLIP) + REL_CLIP].astype(F32)


def _attn_prompt(qa, ka, va, rel_bias, batch, seq):
    ng = seq // ATT_ROWS
    assert seq % ATT_ROWS == 0 and ng >= ATT_CASES and seq >= ATT_WINDOW
    kv = pl.BlockSpec((1, seq, A_WIDTH), lambda b, g: (b, 0, 0))
    return pl.pallas_call(
        _attn_prompt_kernel,
        grid=(batch, ng),
        in_specs=[pl.BlockSpec((ATT_ROWS, A_WIDTH), lambda b, g: (b * ng + g, 0)), kv, kv,
                  _full((A_HEADS, BIAS_WRAP))],
        out_specs=pl.BlockSpec((ATT_ROWS, A_WIDTH), lambda b, g: (b * ng + g, 0)),
        out_shape=jax.ShapeDtypeStruct((batch * seq, A_WIDTH), BF16),
        scratch_shapes=[pltpu.VMEM((A_HEADS, ATT_ROWS, ATT_WINDOW), F32)],
        compiler_params=_params("arbitrary", "arbitrary"),
        name="attn_prompt",
    )(qa, ka.reshape(batch, seq, A_WIDTH), va.reshape(batch, seq, A_WIDTH), _wrapped_bias_table(rel_bias))


def _attn_sample_kernel(q_ref, kn_ref, vn_ref, kc_ref, vc_ref, tb_ref, o_ref, *, first_valid):
    t = q_ref.shape[0]
    r = kc_ref.shape[0]
    q = q_ref[...]
    kn = kn_ref[...]
    vn = vn_ref[...]
    valid_c = lax.broadcasted_iota(jnp.int32, (t, r), 1) >= first_valid
    nt = (((1,), (1,)), ((), ()))
    outs = []
    for head in range(A_HEADS):
        sl = slice(head * A_HEAD_DIM, (head + 1) * A_HEAD_DIM)
        kc = kc_ref[:, head, :].astype(BF16)
        vc = vc_ref[:, head, :].astype(BF16)
        table = jnp.broadcast_to(tb_ref[head:head + 1, :], (t, BIAS_WRAP))
        bias_c = pltpu.roll(table, r % BIAS_WRAP, 1, stride=1, stride_axis=0)[:, :r]
        bias_n = pltpu.roll(table, 0, 1, stride=1, stride_axis=0)[:, :t]
        sc = lax.dot_general(q[:, sl], kc, nt, preferred_element_type=F32) + jnp.where(valid_c, bias_c, NEG)
        sn = lax.dot_general(q[:, sl], kn[:, sl], nt, preferred_element_type=F32) + bias_n
        m = jnp.maximum(jnp.max(sc, axis=-1, keepdims=True), jnp.max(sn, axis=-1, keepdims=True))
        ec = jnp.exp(sc - m)
        en = jnp.exp(sn - m)
        inv = 1.0 / (jnp.sum(ec, axis=-1, keepdims=True) + jnp.sum(en, axis=-1, keepdims=True))
        outs.append(jnp.dot((ec * inv).astype(BF16), vc, preferred_element_type=F32)
                    + jnp.dot((en * inv).astype(BF16), vn[:, sl], preferred_element_type=F32))
    o_ref[...] = jnp.concatenate(outs, axis=1).astype(o_ref.dtype)


def _attn_sample(qa, ka, va, cache_k, cache_v, layer, rel_bias, batch, t, past_len):
    r = cache_k.shape[2]
    q_chunk = past_len // CHUNK
    assert (past_len + t - 1) // CHUNK == q_chunk and r <= BAND
    first_valid = max(0, (q_chunk - N_PAST_CHUNKS) * CHUNK - (past_len - r))
    tok = pl.BlockSpec((t, A_WIDTH), lambda b: (b, 0))
    cache = pl.BlockSpec((None, None, r, A_HEADS, A_HEAD_DIM), lambda b: (layer, b, 0, 0, 0))
    return pl.pallas_call(
        functools.partial(_attn_sample_kernel, first_valid=first_valid),
        grid=(batch,),
        in_specs=[tok, tok, tok, cache, cache, _full((A_HEADS, BIAS_WRAP))],
        out_specs=tok,
        out_shape=jax.ShapeDtypeStruct((batch * t, A_WIDTH), BF16),
        compiler_params=_params("parallel"),
        name="attn_sample",
    )(qa, ka, va, cache_k, cache_v, _wrapped_bias_table(rel_bias))


def _gla_kernel(q_ref, k_ref, v_ref, la_ref, s0_ref, tri_ref, expand_ref, bdmask_ref, o_ref, sfin_ref, state_ref):
    tb = q_ref.shape[0]
    nb = tb // G_BLOCK

    @pl.when(pl.program_id(1) == 0)
    def _():
        state_ref[...] = s0_ref[0]

    q = q_ref[...]
    k = k_ref[...]
    v = v_ref[...]
    cum = jnp.dot(tri_ref[...], la_ref[...], precision=HIGHEST, preferred_element_type=F32)
    q3 = q.reshape(nb, G_BLOCK, G_KW)
    k3 = k.reshape(nb, G_BLOCK, G_KW)
    v3 = v.reshape(nb, G_BLOCK, G_WIDTH)
    cum3 = cum.reshape(nb, G_BLOCK, G_KW)
    row = lax.broadcasted_iota(jnp.int32, (nb, G_BLOCK, G_KW), 1)
    expand = expand_ref[...]

    intra = jnp.zeros((nb, G_BLOCK, G_WIDTH), F32)
    for s in range(G_BLOCK):
        k_s = jnp.broadcast_to(k3[:, s:s + 1, :], (nb, G_BLOCK, G_KW))
        c_s = jnp.broadcast_to(cum3[:, s:s + 1, :], (nb, G_BLOCK, G_KW))
        v_s = jnp.broadcast_to(v3[:, s:s + 1, :], (nb, G_BLOCK, G_WIDTH))
        prod = q3 * k_s * jnp.exp(jnp.where(row >= s, cum3 - c_s, -jnp.inf))
        att = jnp.dot(prod.reshape(tb, G_KW).astype(BF16), expand, preferred_element_type=F32)
        intra = intra + att.reshape(nb, G_BLOCK, G_WIDTH) * v_s
    o_ref[...] = intra.reshape(tb, G_WIDTH)

    last3 = jnp.broadcast_to(cum3[:, G_BLOCK - 1:G_BLOCK, :], (nb, G_BLOCK, G_KW))
    q_dec = (q3 * jnp.exp(cum3)).reshape(tb, G_KW).astype(BF16)
    k_dec = (k3 * jnp.exp(last3 - cum3)).reshape(tb, G_KW).astype(BF16)
    carry = jnp.exp(last3).reshape(tb, G_KW)
    vb = v.astype(BF16)
    bdmask = bdmask_ref[...]
    for j in range(nb):
        rows = slice(j * G_BLOCK, (j + 1) * G_BLOCK)
        state = state_ref[...]
        o_ref[rows, :] += lax.dot_general(q_dec[rows], state.astype(BF16), (((1,), (1,)), ((), ())),
                                          preferred_element_type=F32)
        upd = lax.dot_general(vb[rows], k_dec[rows], (((0,), (0,)), ((), ())), preferred_element_type=F32)
        state_ref[...] = carry[j * G_BLOCK:j * G_BLOCK + 1, :] * state + upd * bdmask

    @pl.when(pl.program_id(1) == pl.num_programs(1) - 1)
    def _():
        sfin_ref[0] = state_ref[...]


def _gla(qg, kg, vg, la, s0, batch, t, tile):
    nt = t // tile
    hsel = jnp.arange(G_KW) // G_DK
    vsel = jnp.arange(G_WIDTH) // G_DV
    same = hsel[:, None] == vsel[None, :]
    expand = same.astype(BF16)
    bdmask = same.T.astype(F32)
    s0_t = jnp.swapaxes(s0, 2, 3).reshape(batch, G_WIDTH, G_DK)
    s0_bd = jnp.where(same.T[None], jnp.tile(s0_t, (1, 1, G_HEADS)), 0.0)
    rr = jnp.arange(tile)
    tri = ((rr[:, None] // G_BLOCK == rr[None, :] // G_BLOCK) & (rr[None, :] <= rr[:, None])).astype(F32)
    tok = lambda w: pl.BlockSpec((tile, w), lambda b, i: (b * nt + i, 0))
    st = pl.BlockSpec((1, G_WIDTH, G_KW), lambda b, i: (b, 0, 0))
    o, s_bd = pl.pallas_call(
        _gla_kernel,
        grid=(batch, nt),
        in_specs=[tok(G_KW), tok(G_KW), tok(G_WIDTH), tok(G_KW), st,
                  _full((tile, tile)), _full((G_KW, G_WIDTH)), _full((G_WIDTH, G_KW))],
        out_specs=[tok(G_WIDTH), st],
        out_shape=[jax.ShapeDtypeStruct((batch * t, G_WIDTH), F32),
                   jax.ShapeDtypeStruct((batch, G_WIDTH, G_KW), F32)],
        scratch_shapes=[pltpu.VMEM((G_WIDTH, G_KW), F32)],
        compiler_params=_params("parallel", "arbitrary"),
        name="gla",
    )(qg, kg, vg, la, s0_bd, tri, expand, bdmask)
    s_fin = jnp.stack([s_bd[:, h * G_DV:(h + 1) * G_DV, h * G_DK:(h + 1) * G_DK] for h in range(G_HEADS)], axis=1)
    return o, jnp.swapaxes(s_fin, 2, 3)


def _mixer_out_kernel(x_ref, oa_ref, og_ref, sr_ref, gg_ref, avg_ref, wa_ref, wg_ref, y_ref):
    og = og_ref[...]
    og = og * lax.rsqrt(_seg_mean_sq(og, avg_ref[...]) + EPS) * gg_ref[...] * sr_ref[...]
    y = jnp.dot(oa_ref[...], wa_ref[...], preferred_element_type=F32)
    y = y + jnp.dot(og.astype(BF16), wg_ref[...], preferred_element_type=F32)
    y_ref[...] = x_ref[...] + y


def _mixer_out(x, oa, og, sr, g_gla_out, w_out):
    n = x.shape[0]
    tm = TOKEN_TILE
    row = lambda w: pl.BlockSpec((tm, w), lambda i: (i, 0))
    w_out = w_out.astype(BF16)
    return pl.pallas_call(
        _mixer_out_kernel,
        grid=(n // tm,),
        in_specs=[row(D_MODEL), row(A_WIDTH), row(G_WIDTH), row(G_WIDTH), _full((1, G_WIDTH)),
                  _full((G_WIDTH, G_WIDTH)), _full((A_WIDTH, D_MODEL)), _full((G_WIDTH, D_MODEL))],
        out_specs=row(D_MODEL),
        out_shape=jax.ShapeDtypeStruct((n, D_MODEL), F32),
        compiler_params=_params("parallel"),
        name="mixer_out",
    )(x, oa, og, sr, jnp.tile(g_gla_out, G_HEADS).reshape(1, -1), _seg_avg_matrix(G_WIDTH, G_DV),
      w_out[:A_WIDTH], w_out[A_WIDTH:])


def _head_rms(z, g_ref, scale=1.0):
    outs = []
    for head in range(X_HEADS):
        sl = slice(head * X_HEAD_DIM, (head + 1) * X_HEAD_DIM)
        zh = z[:, sl]
        outs.append(zh * (lax.rsqrt(jnp.mean(zh * zh, axis=-1, keepdims=True) + EPS) * scale) * g_ref[:, sl])
    return outs


def _mem_kv_kernel(m_ref, gm_ref, wk_ref, wv_ref, gk_ref, k_ref, v_ref):
    m = _rms(m_ref[...], gm_ref[...]).astype(BF16)
    k = _head_rms(jnp.dot(m, wk_ref[...], preferred_element_type=F32), gk_ref)
    v = jnp.dot(m, wv_ref[...], preferred_element_type=F32)
    for head in range(X_HEADS):
        k_ref[:, head, :] = k[head]
        v_ref[:, head, :] = v[:, head * X_HEAD_DIM:(head + 1) * X_HEAD_DIM]


def _mem_kv(mem, g_mem, w_ck, w_cv, g_ck):
    batch = mem.shape[0]
    sq = _full((D_MODEL, D_MODEL))
    vec = _full((1, D_MODEL))
    out = pl.BlockSpec((None, N_MEM, X_HEADS, X_HEAD_DIM), lambda i: (i, 0, 0, 0))
    return pl.pallas_call(
        _mem_kv_kernel,
        grid=(batch,),
        in_specs=[pl.BlockSpec((None, N_MEM, D_MODEL), lambda i: (i, 0, 0)), vec, sq, sq, vec],
        out_specs=[out, out],
        out_shape=[jax.ShapeDtypeStruct((batch, N_MEM, X_HEADS, X_HEAD_DIM), F32)] * 2,
        compiler_params=_params("parallel"),
        name="mem_kv",
    )(mem, g_mem.reshape(1, -1), w_ck.astype(BF16), w_cv.astype(BF16), jnp.tile(g_ck, X_HEADS).reshape(1, -1))


def _cross_kernel(x_ref, gc_ref, wq_ref, gq_ref, mk_ref, mv_ref, wo_ref, y_ref):
    x = x_ref[...]
    h = _rms(x, gc_ref[...]).astype(BF16)
    q = _head_rms(jnp.dot(h, wq_ref[...], preferred_element_type=F32), gq_ref, X_HEAD_DIM ** -0.5)
    outs = []
    for head in range(X_HEADS):
        mk = mk_ref[:, head, :].astype(BF16)
        mv = mv_ref[:, head, :].astype(BF16)
        s = lax.dot_general(q[head].astype(BF16), mk, (((1,), (1,)), ((), ())), preferred_element_type=F32)
        e = jnp.exp(s - jnp.max(s, axis=-1, keepdims=True))
        p = (e * (1.0 / jnp.sum(e, axis=-1, keepdims=True))).astype(BF16)
        outs.append(jnp.dot(p, mv, preferred_element_type=F32))
    o = jnp.concatenate(outs, axis=1).astype(BF16)
    y_ref[...] = x + jnp.dot(o, wo_ref[...], preferred_element_type=F32)


def _cross(x, mem_k, mem_v, mem_index, g_cross, w_cq, g_cq, w_co, batch, t, tile):
    nt = t // tile
    row = pl.BlockSpec((tile, D_MODEL), lambda b, i: (b * nt + i, 0))
    lead = (None,) * (mem_k.ndim - 3)
    mem = pl.BlockSpec(lead + (N_MEM, X_HEADS, X_HEAD_DIM), lambda b, i: mem_index(b) + (0, 0, 0))
    sq = _full((D_MODEL, D_MODEL))
    vec = _full((1, D_MODEL))
    return pl.pallas_call(
        _cross_kernel,
        grid=(batch, nt),
        in_specs=[row, vec, sq, vec, mem, mem, sq],
        out_specs=row,
        out_shape=jax.ShapeDtypeStruct((batch * t, D_MODEL), F32),
        compiler_params=_params("parallel", "parallel"),
        name="cross",
    )(x, g_cross.reshape(1, -1), w_cq.astype(BF16), jnp.tile(g_cq, X_HEADS).reshape(1, -1),
      mem_k, mem_v, w_co.astype(BF16))


CAND_ROWS = 64


def _sorting_network(n):
    pairs = []
    p = 1
    while p < n:
        k = p
        while k >= 1:
            for j in range(k % p, n - k, 2 * k):
                for i in range(min(k, n - j - k)):
                    if (i + j) // (2 * p) == (i + j + k) // (2 * p):
                        pairs.append((i + j, i + j + k))
            k //= 2
        p *= 2
    return pairs


def _top_values_tiled(s, count):
    nparts = s.shape[0] // SUBLANES
    parts = [s[SUBLANES * k:SUBLANES * (k + 1), :] for k in range(nparts)]
    for i, j in _sorting_network(nparts):
        parts[i], parts[j] = jnp.maximum(parts[i], parts[j]), jnp.minimum(parts[i], parts[j])
    which = lax.broadcasted_iota(jnp.int32, parts[0].shape, 0).astype(F32)
    exhausted = jnp.full(parts[0].shape, -jnp.inf, F32)
    out = []
    for r in range(count):
        mx = jnp.max(parts[0], axis=0, keepdims=True)
        first = jnp.min(jnp.where(parts[0] == mx, which, float(SUBLANES)), axis=0, keepdims=True)
        pop = which == first
        out.append(mx)
        for k in range(min(count - r - 1, nparts)):
            parts[k] = jnp.where(pop, parts[k + 1] if k + 1 < nparts else exhausted, parts[k])
    return jnp.concatenate(out, axis=0)


def _route_kernel(x_ref, gf_ref, whi_ref, wlo_ref, sk1_ref, sk2_ref, ht_ref, r2_ref, e2_ref, n1_ref, c1_ref):
    hf = _rms(x_ref[...], gf_ref[...])
    ht_ref[...] = jnp.transpose(hf).astype(BF16)
    h_hi = hf.astype(BF16)
    h_lo = (hf - h_hi.astype(F32)).astype(BF16)
    qry = (jnp.dot(h_hi, whi_ref[...], preferred_element_type=F32)
           + jnp.dot(h_hi, wlo_ref[...], preferred_element_type=F32)
           + jnp.dot(h_lo, whi_ref[...], preferred_element_type=F32))
    nt = (((1,), (1,)), ((), ()))
    nk = P_TOPK + 1
    t = qry.shape[0]
    for head in range(P_HEADS):
        q1 = qry[:, head * P_QDIM:head * P_QDIM + P_HALF]
        q2 = qry[:, head * P_QDIM + P_HALF:(head + 1) * P_QDIM]
        s1 = lax.dot_general(sk1_ref[...], q1, nt, precision=HIGHEST, preferred_element_type=F32)
        s2 = lax.dot_general(sk2_ref[...], q2, nt, precision=HIGHEST, preferred_element_type=F32)
        v1 = _top_values_tiled(s1, nk)
        v2 = _top_values_tiled(s2, nk)
        slabs = [v1[a:a + 1, :] + v2[:nk // (a + 1), :] for a in range(nk)]
        pad = CAND_ROWS - sum(sl.shape[0] for sl in slabs)
        cand = jnp.concatenate(slabs + [jnp.full((pad, t), -jnp.inf, F32)], axis=0)
        top = _top_values_tiled(cand, nk)
        m1 = v1[0:1, :]
        m2 = v2[0:1, :]
        z = jnp.sum(jnp.exp(top[:P_TOPK, :] - (m1 + m2)), axis=0, keepdims=True)
        tau = 0.5 * (top[P_TOPK - 1:P_TOPK, :] + top[P_TOPK:nk, :])
        thr = tau - s1
        n1 = jnp.zeros_like(s1)
        rank2 = jnp.zeros_like(s2)
        for b in range(P_TOPK):
            n1 = n1 + jnp.where(v2[b:b + 1, :] >= thr, 1.0, 0.0)
            rank2 = rank2 + jnp.where(v2[b:b + 1, :] > s2, 1.0, 0.0)
        r2_ref[head] = rank2.astype(BF16)
        e2_ref[head] = jnp.exp(s2 - m2).astype(BF16)
        n1_ref[head] = n1
        c1_ref[head] = jnp.exp(s1 - m1) * (1.0 / z)


def _route(x, g_ffn, w_pq, sub_keys1, sub_keys2):
    n = x.shape[0]
    w_hi = w_pq.astype(BF16)
    w_lo = (w_pq - w_hi.astype(F32)).astype(BF16)
    tm = ROUTE_TILE
    keys = pl.BlockSpec((P_HEADS, P_NKEYS, tm), lambda i: (0, 0, i))
    tab = lambda dt: jax.ShapeDtypeStruct((P_HEADS, P_NKEYS, n), dt)
    return pl.pallas_call(
        _route_kernel,
        grid=(n // tm,),
        in_specs=[pl.BlockSpec((tm, D_MODEL), lambda i: (i, 0)), _full((1, D_MODEL)),
                  _full((D_MODEL, P_HEADS * P_QDIM)), _full((D_MODEL, P_HEADS * P_QDIM)),
                  _full((P_NKEYS, P_HALF)), _full((P_NKEYS, P_HALF))],
        out_specs=[pl.BlockSpec((D_MODEL, tm), lambda i: (0, i)), keys, keys, keys, keys],
        out_shape=[jax.ShapeDtypeStruct((D_MODEL, n), BF16), tab(BF16), tab(BF16), tab(F32), tab(F32)],
        compiler_params=_params("parallel"),
        name="peer_route",
    )(x, g_ffn.reshape(1, -1), w_hi, w_lo, sub_keys1, sub_keys2)


def _peer_kernel(ht_ref, x_ref, u_ref, vt_ref, r2_ref, e2_ref, n1_ref, c1_ref, y_ref, acc_ref):
    j = pl.program_id(1)

    @pl.when(j == 0)
    def _():
        acc_ref[...] = jnp.zeros_like(acc_ref)

    t = ht_ref.shape[1]
    n_i1 = PEER_EXPERT_TILE // P_NKEYS
    zero = jnp.zeros((), BF16)
    a = jnp.dot(u_ref[...], ht_ref[...], preferred_element_type=F32)
    act = (a * (1.0 + lax.erf(a * (1.0 / math.sqrt(2.0))))).astype(BF16)
    gates = []
    for r in range(n_i1):
        g = jnp.zeros((P_NKEYS, t), BF16)
        for head in range(P_HEADS):
            n1 = n1_ref[head, pl.ds(j * n_i1 + r, 1), :].astype(BF16)
            c1 = (0.5 * c1_ref[head, pl.ds(j * n_i1 + r, 1), :]).astype(BF16)
            g = g + jnp.where(r2_ref[head] < n1, e2_ref[head], zero) * c1
        gates.append(g)
    ga = act * jnp.concatenate(gates, axis=0)
    acc_ref[...] += jnp.dot(vt_ref[0], ga, preferred_element_type=F32)

    @pl.when(j == pl.num_programs(1) - 1)
    def _():
        y_ref[...] = x_ref[...] + jnp.transpose(acc_ref[...])


def _peer(ht, x, r2, e2, n1, c1, u, vt, tt):
    n = x.shape[0]
    et = PEER_EXPERT_TILE
    ne = u.shape[0] // et
    once = pl.Buffered(1)
    keys = pl.BlockSpec((P_HEADS, P_NKEYS, tt), lambda i, j: (0, 0, i))
    return pl.pallas_call(
        _peer_kernel,
        grid=(n // tt, ne),
        in_specs=[pl.BlockSpec((D_MODEL, tt), lambda i, j: (0, i)),
                  pl.BlockSpec((tt, D_MODEL), lambda i, j: (i, 0), pipeline_mode=once),
                  pl.BlockSpec((et, D_MODEL), lambda i, j: (j, 0)),
                  pl.BlockSpec((1, D_MODEL, et), lambda i, j: (j, 0, 0)),
                  keys, keys, keys, keys],
        out_specs=pl.BlockSpec((tt, D_MODEL), lambda i, j: (i, 0), pipeline_mode=once),
        out_shape=jax.ShapeDtypeStruct((n, D_MODEL), F32),
        scratch_shapes=[pltpu.VMEM((D_MODEL, tt), F32)],
        compiler_params=_params("parallel", "arbitrary"),
        name="peer_experts",
    )(ht, x, u, vt, r2, e2, n1, c1)


def _layer(xp, xs, layer, cache_k, cache_v, state, cmem_k, cmem_v, mem_prompt, past_len,
           g_mix, w_in, g_q, g_k, rel_bias, w_gate2, b_gate2, g_gla_out, w_out,
           g_cross, g_mem, w_cq, w_ck, w_cv, g_cq, g_ck, w_co, g_ffn, w_pq, sub_keys1, sub_keys2, expert_u, expert_v):
    bp, sp, _ = xp.shape
    bs, ts, _ = xs.shape
    xp = xp.reshape(bp * sp, D_MODEL)
    xs = xs.reshape(bs * ts, D_MODEL)
    u = expert_u.astype(BF16)
    ne = expert_v.shape[0] // PEER_EXPERT_TILE
    vt = jnp.swapaxes(expert_v.reshape(ne, PEER_EXPERT_TILE, D_MODEL), 1, 2).astype(BF16)
    heads = lambda a, b, t: a.reshape(b, t, A_HEADS, A_HEAD_DIM)

    def tail(x, mem_k, mem_v, mem_index, batch, t, cross_tile, peer_tile):
        x = _cross(x, mem_k, mem_v, mem_index, g_cross, w_cq, g_cq, w_co, batch, t, cross_tile)
        ht, r2, e2, n1, c1 = _route(x, g_ffn, w_pq, sub_keys1, sub_keys2)
        return _peer(ht, x, r2, e2, n1, c1, u, vt, peer_tile)

    qa, ka_f, ka_b, va_f, va_b, qg, kg, vg, la, sr = _mixer_in(xp, g_mix, w_in, g_q, g_k, w_gate2, b_gate2)
    oa = _attn_prompt(qa, ka_b, va_b, rel_bias, bp, sp)
    og, gla_p = _gla(qg, kg, vg, la, jnp.zeros((bp, G_HEADS, G_DK, G_DV), F32), bp, sp, GLA_TILE)
    xp = _mixer_out(xp, oa, og, sr, g_gla_out, w_out)
    mk, mv = _mem_kv(mem_prompt, g_mem, w_ck, w_cv, g_ck)
    xp = tail(xp, mk, mv, lambda b: (b,), bp, sp, TOKEN_TILE, PEER_TOKEN_TILE)
    keep = min(BAND, sp)
    outs_p = (heads(ka_f, bp, sp)[:, -keep:], heads(va_f, bp, sp)[:, -keep:], gla_p, mk, mv)

    qa, ka_f, ka_b, va_f, va_b, qg, kg, vg, la, sr = _mixer_in(xs, g_mix, w_in, g_q, g_k, w_gate2, b_gate2)
    oa = _attn_sample(qa, ka_b, va_b, cache_k, cache_v, layer, rel_bias, bs, ts, past_len)
    og, gla_s = _gla(qg, kg, vg, la, state, bs, ts, ts)
    xs = _mixer_out(xs, oa, og, sr, g_gla_out, w_out)
    xs = tail(xs, cmem_k, cmem_v, lambda b: (layer, b), bs, ts, ts, bs * ts)
    outs_s = (heads(ka_f, bs, ts), heads(va_f, bs, ts), gla_s)

    return (xp.reshape(bp, sp, D_MODEL), xs.reshape(bs, ts, D_MODEL)) + outs_p + outs_s


def kernel(x_prompt, x_sample, cache_att_k, cache_att_v, state_gla, cache_mem_k, cache_mem_v, mem_prompt, g_mix, w_in, g_q, g_k, rel_bias, w_gate2, b_gate2, g_gla_out, w_out, g_cross, g_mem, w_cq, w_ck, w_cv, g_cq, g_ck, w_co, g_ffn, w_pq, sub_keys1, sub_keys2, expert_u, expert_v):
    depth = w_in.shape[0]
    past_len = 4096
    xp, xs = x_prompt, x_sample
    per_layer = []
    for l in range(depth):
        outs = _layer(xp, xs, l, cache_att_k, cache_att_v, state_gla[l], cache_mem_k, cache_mem_v,
                      mem_prompt, past_len,
                      g_mix[l], w_in[l], g_q[l], g_k[l], rel_bias[l], w_gate2[l], b_gate2[l], g_gla_out[l], w_out[l],
                      g_cross[l], g_mem[l], w_cq[l], w_ck[l], w_cv[l], g_cq[l], g_ck[l], w_co[l],
                      g_ffn[l], w_pq[l], sub_keys1[l], sub_keys2[l], expert_u[l], expert_v[l])
        xp, xs = outs[0], outs[1]
        per_layer.append(outs[2:])
    stacked = [jnp.stack([p[i] for p in per_layer]) for i in range(8)]
    return (xp, xs) + tuple(stacked)
```

```python
import functools
import math

import jax
import jax.numpy as jnp
from jax import lax
from jax.experimental import pallas as pl
from jax.experimental.pallas import tpu as pltpu

F32 = jnp.float32
BF16 = jnp.bfloat16
HIGHEST = lax.Precision.HIGHEST

D_MODEL = 1024
CHUNK = 64
N_PAST_CHUNKS = 8
BAND = (N_PAST_CHUNKS + 1) * CHUNK
A_HEADS = 8
A_HEAD_DIM = 64
A_WIDTH = A_HEADS * A_HEAD_DIM
REL_CLIP = 128
G_HEADS = 4
G_DK = 64
G_DV = 128
G_KW = G_HEADS * G_DK
G_WIDTH = G_HEADS * G_DV
G_GATE_RANK = 16
G_GATE_NORM = 16.0
G_BLOCK = 16
N_MEM = 256
X_HEADS = 4
X_HEAD_DIM = D_MODEL // X_HEADS
P_HEADS = 8
P_NKEYS = 128
P_QDIM = 256
P_HALF = P_QDIM // 2
P_TOPK = 16
EPS = 1e-6
NEG = -1e30

LANES = 128
SUBLANES = 8
VMEM_LIMIT = 56 * 1024 * 1024

TOKEN_TILE = 512
ROUTE_TILE = 256
GLA_TILE = 256
PEER_TOKEN_TILE = 1024
PEER_EXPERT_TILE = 1024


def _params(*sem):
    return pltpu.CompilerParams(dimension_semantics=sem, vmem_limit_bytes=VMEM_LIMIT)


def _full(shape):
    return pl.BlockSpec(shape, lambda *_: (0,) * len(shape))


def _rms(x, g):
    return x * lax.rsqrt(jnp.mean(x * x, axis=-1, keepdims=True) + EPS) * g


def _seg_mean_sq(z, seg_avg):
    sq = z * z
    hi = sq.astype(BF16)
    lo = (sq - hi.astype(F32)).astype(BF16)
    return (jnp.dot(hi, seg_avg, preferred_element_type=F32)
            + jnp.dot(lo, seg_avg, preferred_element_type=F32))


def _seg_avg_matrix(width, seg):
    r = jnp.arange(width) // seg
    return jnp.where(r[:, None] == r[None, :], 1.0 / seg, 0.0).astype(BF16)


def _mixer_in_kernel(x_ref, gmix_ref, w_ref, gq_ref, gk_ref, avg_ref, wg2_ref, bg2_ref,
                     qa_ref, kaf_ref, kab_ref, vaf_ref, vab_ref, qg_ref, kg_ref, vg_ref, la_ref, sr_ref):
    h = _rms(x_ref[...], gmix_ref[...]).astype(BF16)
    z = jnp.dot(h, w_ref[...], preferred_element_type=F32)
    o = 0
    qa = z[:, o:o + A_WIDTH]; o += A_WIDTH
    ka = z[:, o:o + A_WIDTH]; o += A_WIDTH
    va = z[:, o:o + A_WIDTH]; o += A_WIDTH
    qg = z[:, o:o + G_KW]; o += G_KW
    kg = z[:, o:o + G_KW]; o += G_KW
    vg = z[:, o:o + G_WIDTH]; o += G_WIDTH
    r = z[:, o:o + G_WIDTH]; o += G_WIDTH
    gl = z[:, o:o + LANES]
    avg = avg_ref[...]
    qa = qa * lax.rsqrt(_seg_mean_sq(qa, avg) + EPS) * gq_ref[...]
    ka = ka * lax.rsqrt(_seg_mean_sq(ka, avg) + EPS) * gk_ref[...]
    qa_ref[...] = (qa * (A_HEAD_DIM ** -0.5)).astype(BF16)
    kaf_ref[...] = ka
    kab_ref[...] = ka.astype(BF16)
    vaf_ref[...] = va
    vab_ref[...] = va.astype(BF16)
    qg_ref[...] = qg * (G_DK ** -0.5)
    kg_ref[...] = kg
    vg_ref[...] = vg
    pre = jnp.dot(gl, wg2_ref[...], precision=HIGHEST, preferred_element_type=F32) + bg2_ref[...]
    la_ref[...] = (jnp.minimum(pre, 0.0) - jnp.log1p(jnp.exp(-jnp.abs(pre)))) / G_GATE_NORM
    sr_ref[...] = r * jax.nn.sigmoid(r)


def _mixer_in(x, g_mix, w_in, g_q, g_k, w_gate2, b_gate2):
    n = x.shape[0]
    c_gl = 3 * A_WIDTH + 2 * G_KW + G_WIDTH
    w_cat = jnp.concatenate(
        [w_in[:, :c_gl], w_in[:, c_gl + G_GATE_RANK:],
         jnp.pad(w_in[:, c_gl:c_gl + G_GATE_RANK], ((0, 0), (0, LANES - G_GATE_RANK)))], axis=1).astype(BF16)
    wg2 = jnp.pad(w_gate2, ((0, LANES - G_GATE_RANK), (0, 0)))
    wcols = w_cat.shape[1]
    tm = TOKEN_TILE
    row = lambda w: pl.BlockSpec((tm, w), lambda i: (i, 0))
    outs = [(A_WIDTH, BF16), (A_WIDTH, F32), (A_WIDTH, BF16), (A_WIDTH, F32), (A_WIDTH, BF16),
            (G_KW, F32), (G_KW, F32), (G_WIDTH, F32), (G_KW, F32), (G_WIDTH, F32)]
    return pl.pallas_call(
        _mixer_in_kernel,
        grid=(n // tm,),
        in_specs=[row(D_MODEL), _full((1, D_MODEL)), _full((D_MODEL, wcols)), _full((1, A_WIDTH)),
                  _full((1, A_WIDTH)), _full((A_WIDTH, A_WIDTH)), _full((LANES, G_KW)), _full((1, G_KW))],
        out_specs=[row(w) for w, _ in outs],
        out_shape=[jax.ShapeDtypeStruct((n, w), dt) for w, dt in outs],
        compiler_params=_params("parallel"),
        name="mixer_in",
    )(x, g_mix.reshape(1, -1), w_cat, jnp.tile(g_q, A_HEADS).reshape(1, -1), jnp.tile(g_k, A_HEADS).reshape(1, -1),
      _seg_avg_matrix(A_WIDTH, A_HEAD_DIM), wg2, b_gate2.reshape(1, -1))


def _attend(q, segments):
    tq = q.shape[0]
    lane = lax.broadcasted_iota(jnp.int32, (tq, LANES), 1)
    outs = []
    for pair in range(A_HEADS // 2):
        sl = slice(pair * LANES, (pair + 1) * LANES)
        q2 = q[:, sl]
        halves = []
        for half in range(2):
            head = 2 * pair + half
            mine = (lane < A_HEAD_DIM) if half == 0 else (lane >= A_HEAD_DIM)
            qm = jnp.where(mine, q2, jnp.zeros_like(q2))
            scores = [lax.dot_general(qm, k[:, sl], (((1,), (1,)), ((), ())), preferred_element_type=F32) + bias(head)
                      for k, _, bias in segments]
            m = functools.reduce(jnp.maximum, [jnp.max(s, axis=-1, keepdims=True) for s in scores])
            es = [jnp.exp(s - m) for s in scores]
            inv = 1.0 / functools.reduce(jnp.add, [jnp.sum(e, axis=-1, keepdims=True) for e in es])
            o = functools.reduce(jnp.add, [
                jnp.dot((e * inv).astype(BF16), v[:, sl], preferred_element_type=F32)
                for e, (_, v, _) in zip(es, segments)])
            halves.append(o)
        outs.append(jnp.where(lane < A_HEAD_DIM, halves[0], halves[1]))
    return jnp.concatenate(outs, axis=1)


ATT_GROUP = 4
ATT_ROWS = ATT_GROUP * CHUNK
ATT_WINDOW = (N_PAST_CHUNKS + ATT_GROUP) * CHUNK
ATT_CASES = -(-N_PAST_CHUNKS // ATT_GROUP) + 1
BIAS_WRAP = 1024


def _attn_prompt_kernel(q_ref, k_ref, v_ref, tb_ref, o_ref, bias_ref):
    g = pl.program_id(1)
    first = jnp.maximum(g * ATT_GROUP - N_PAST_CHUNKS, 0)
    start = pl.multiple_of(first * CHUNK, CHUNK)

    for case in range(ATT_CASES):
        @pl.when(g == case)
        def _(case=case):
            off = min(case * ATT_GROUP, N_PAST_CHUNKS) * CHUNK
            q_chunk = (off + lax.broadcasted_iota(jnp.int32, (ATT_ROWS, ATT_WINDOW), 0)) // CHUNK
            k_chunk = lax.broadcasted_iota(jnp.int32, (ATT_ROWS, ATT_WINDOW), 1) // CHUNK
            valid = (k_chunk <= q_chunk) & (k_chunk >= q_chunk - N_PAST_CHUNKS)
            for head in range(A_HEADS):
                table = jnp.broadcast_to(tb_ref[head:head + 1, :], (ATT_ROWS, BIAS_WRAP))
                rolled = pltpu.roll(table, off, 1, stride=1, stride_axis=0)
                bias_ref[head] = jnp.where(valid, rolled[:, :ATT_WINDOW], NEG)

    kb = k_ref[0, pl.ds(start, ATT_WINDOW), :]
    vb = v_ref[0, pl.ds(start, ATT_WINDOW), :]
    o_ref[...] = _attend(q_ref[...], [(kb, vb, lambda head: bias_ref[head])]).astype(o_ref.dtype)


def _wrapped_bias_table(rel_bias):
    u = jnp.arange(BIAS_WRAP)
    dist = jnp.where(u < CHUNK, -u, BIAS_WRAP - u)
    return rel_bias[:, jnp.clip(dist, -REL_CLIP, REL_CLIP) + REL_CLIP].astype(F32)


def _attn_prompt(qa, ka, va, rel_bias, batch, seq):
    ng = seq // ATT_ROWS
    assert seq % ATT_ROWS == 0 and ng >= ATT_CASES and seq >= ATT_WINDOW
    kv = pl.BlockSpec((1, seq, A_WIDTH), lambda b, g: (b, 0, 0))
    return pl.pallas_call(
        _attn_prompt_kernel,
        grid=(batch, ng),
        in_specs=[pl.BlockSpec((ATT_ROWS, A_WIDTH), lambda b, g: (b * ng + g, 0)), kv, kv,
                  _full((A_HEADS, BIAS_WRAP))],
        out_specs=pl.BlockSpec((ATT_ROWS, A_WIDTH), lambda b, g: (b * ng + g, 0)),
        out_shape=jax.ShapeDtypeStruct((batch * seq, A_WIDTH), BF16),
        scratch_shapes=[pltpu.VMEM((A_HEADS, ATT_ROWS, ATT_WINDOW), F32)],
        compiler_params=_params("arbitrary", "arbitrary"),
        name="attn_prompt",
    )(qa, ka.reshape(batch, seq, A_WIDTH), va.reshape(batch, seq, A_WIDTH), _wrapped_bias_table(rel_bias))


SAMPLE_STREAMS = 4


def _attn_sample_kernel(q_ref, kn_ref, vn_ref, kc_ref, vc_ref, tb_ref, o_ref, *, t, first_valid):
    r = kc_ref.shape[1]
    valid_c = lax.broadcasted_iota(jnp.int32, (t, r), 1) >= first_valid
    bias_c, bias_n = [], []
    for head in range(A_HEADS):
        table = jnp.broadcast_to(tb_ref[head:head + 1, :], (t, BIAS_WRAP))
        bias_c.append(jnp.where(valid_c, pltpu.roll(table, r % BIAS_WRAP, 1, stride=1, stride_axis=0)[:, :r], NEG))
        bias_n.append(pltpu.roll(table, 0, 1, stride=1, stride_axis=0)[:, :t])
    for stream in range(kc_ref.shape[0]):
        rows = slice(stream * t, (stream + 1) * t)
        segs = [(kc_ref[stream].astype(BF16), vc_ref[stream].astype(BF16), lambda head: bias_c[head]),
                (kn_ref[rows, :], vn_ref[rows, :], lambda head: bias_n[head])]
        o_ref[rows, :] = _attend(q_ref[rows, :], segs).astype(o_ref.dtype)


def _attn_sample(qa, ka, va, cache_k, cache_v, rel_bias, batch, t, past_len):
    r = cache_k.shape[1]
    q_chunk = past_len // CHUNK
    assert (past_len + t - 1) // CHUNK == q_chunk and r <= BAND
    first_valid = max(0, (q_chunk - N_PAST_CHUNKS) * CHUNK - (past_len - r))
    sb = math.gcd(batch, SAMPLE_STREAMS)
    tok = pl.BlockSpec((sb * t, A_WIDTH), lambda b: (b, 0))
    cache = pl.BlockSpec((sb, r, A_WIDTH), lambda b: (b, 0, 0))
    return pl.pallas_call(
        functools.partial(_attn_sample_kernel, t=t, first_valid=first_valid),
        grid=(batch // sb,),
        in_specs=[tok, tok, tok, cache, cache, _full((A_HEADS, BIAS_WRAP))],
        out_specs=tok,
        out_shape=jax.ShapeDtypeStruct((batch * t, A_WIDTH), BF16),
        compiler_params=_params("parallel"),
        name="attn_sample",
    )(qa, ka, va, cache_k.reshape(batch, r, A_WIDTH), cache_v.reshape(batch, r, A_WIDTH),
      _wrapped_bias_table(rel_bias))


def _gla_kernel(q_ref, k_ref, v_ref, la_ref, s0_ref, tri_ref, expand_ref, bdmask_ref, o_ref, sfin_ref, state_ref):
    tb = q_ref.shape[0]
    nb = tb // G_BLOCK

    @pl.when(pl.program_id(1) == 0)
    def _():
        state_ref[...] = s0_ref[0]

    q = q_ref[...]
    k = k_ref[...]
    v = v_ref[...]
    cum = jnp.dot(tri_ref[...], la_ref[...], precision=HIGHEST, preferred_element_type=F32)
    q3 = q.reshape(nb, G_BLOCK, G_KW)
    k3 = k.reshape(nb, G_BLOCK, G_KW)
    v3 = v.reshape(nb, G_BLOCK, G_WIDTH)
    cum3 = cum.reshape(nb, G_BLOCK, G_KW)
    row = lax.broadcasted_iota(jnp.int32, (nb, G_BLOCK, G_KW), 1)
    expand = expand_ref[...]

    intra = jnp.zeros((nb, G_BLOCK, G_WIDTH), F32)
    for s in range(G_BLOCK):
        k_s = jnp.broadcast_to(k3[:, s:s + 1, :], (nb, G_BLOCK, G_KW))
        c_s = jnp.broadcast_to(cum3[:, s:s + 1, :], (nb, G_BLOCK, G_KW))
        v_s = jnp.broadcast_to(v3[:, s:s + 1, :], (nb, G_BLOCK, G_WIDTH))
        prod = q3 * k_s * jnp.exp(jnp.where(row >= s, cum3 - c_s, -jnp.inf))
        att = jnp.dot(prod.reshape(tb, G_KW).astype(BF16), expand, preferred_element_type=F32)
        intra = intra + att.reshape(nb, G_BLOCK, G_WIDTH) * v_s
    o_ref[...] = intra.reshape(tb, G_WIDTH)

    last3 = jnp.broadcast_to(cum3[:, G_BLOCK - 1:G_BLOCK, :], (nb, G_BLOCK, G_KW))
    q_dec = (q3 * jnp.exp(cum3)).reshape(tb, G_KW).astype(BF16)
    k_dec = (k3 * jnp.exp(last3 - cum3)).reshape(tb, G_KW).astype(BF16)
    carry = jnp.exp(last3).reshape(tb, G_KW)
    vb = v.astype(BF16)
    bdmask = bdmask_ref[...]
    for j in range(nb):
        rows = slice(j * G_BLOCK, (j + 1) * G_BLOCK)
        state = state_ref[...]
        o_ref[rows, :] += lax.dot_general(q_dec[rows], state.astype(BF16), (((1,), (1,)), ((), ())),
                                          preferred_element_type=F32)
        upd = lax.dot_general(vb[rows], k_dec[rows], (((0,), (0,)), ((), ())), preferred_element_type=F32)
        state_ref[...] = carry[j * G_BLOCK:j * G_BLOCK + 1, :] * state + upd * bdmask

    @pl.when(pl.program_id(1) == pl.num_programs(1) - 1)
    def _():
        sfin_ref[0] = state_ref[...]


def _gla(qg, kg, vg, la, s0, batch, t, tile):
    nt = t // tile
    hsel = jnp.arange(G_KW) // G_DK
    vsel = jnp.arange(G_WIDTH) // G_DV
    same = hsel[:, None] == vsel[None, :]
    expand = same.astype(BF16)
    bdmask = same.T.astype(F32)
    s0_t = jnp.swapaxes(s0, 2, 3).reshape(batch, G_WIDTH, G_DK)
    s0_bd = jnp.where(same.T[None], jnp.tile(s0_t, (1, 1, G_HEADS)), 0.0)
    rr = jnp.arange(tile)
    tri = ((rr[:, None] // G_BLOCK == rr[None, :] // G_BLOCK) & (rr[None, :] <= rr[:, None])).astype(F32)
    tok = lambda w: pl.BlockSpec((tile, w), lambda b, i: (b * nt + i, 0))
    st = pl.BlockSpec((1, G_WIDTH, G_KW), lambda b, i: (b, 0, 0))
    o, s_bd = pl.pallas_call(
        _gla_kernel,
        grid=(batch, nt),
        in_specs=[tok(G_KW), tok(G_KW), tok(G_WIDTH), tok(G_KW), st,
                  _full((tile, tile)), _full((G_KW, G_WIDTH)), _full((G_WIDTH, G_KW))],
        out_specs=[tok(G_WIDTH), st],
        out_shape=[jax.ShapeDtypeStruct((batch * t, G_WIDTH), F32),
                   jax.ShapeDtypeStruct((batch, G_WIDTH, G_KW), F32)],
        scratch_shapes=[pltpu.VMEM((G_WIDTH, G_KW), F32)],
        compiler_params=_params("parallel", "arbitrary"),
        name="gla",
    )(qg, kg, vg, la, s0_bd, tri, expand, bdmask)
    s_fin = jnp.stack([s_bd[:, h * G_DV:(h + 1) * G_DV, h * G_DK:(h + 1) * G_DK] for h in range(G_HEADS)], axis=1)
    return o, jnp.swapaxes(s_fin, 2, 3)


def _mixer_out_kernel(x_ref, oa_ref, og_ref, sr_ref, gg_ref, avg_ref, wa_ref, wg_ref, y_ref):
    og = og_ref[...]
    og = og * lax.rsqrt(_seg_mean_sq(og, avg_ref[...]) + EPS) * gg_ref[...] * sr_ref[...]
    y = jnp.dot(oa_ref[...], wa_ref[...], preferred_element_type=F32)
    y = y + jnp.dot(og.astype(BF16), wg_ref[...], preferred_element_type=F32)
    y_ref[...] = x_ref[...] + y


def _mixer_out(x, oa, og, sr, g_gla_out, w_out):
    n = x.shape[0]
    tm = TOKEN_TILE
    row = lambda w: pl.BlockSpec((tm, w), lambda i: (i, 0))
    w_out = w_out.astype(BF16)
    return pl.pallas_call(
        _mixer_out_kernel,
        grid=(n // tm,),
        in_specs=[row(D_MODEL), row(A_WIDTH), row(G_WIDTH), row(G_WIDTH), _full((1, G_WIDTH)),
                  _full((G_WIDTH, G_WIDTH)), _full((A_WIDTH, D_MODEL)), _full((G_WIDTH, D_MODEL))],
        out_specs=row(D_MODEL),
        out_shape=jax.ShapeDtypeStruct((n, D_MODEL), F32),
        compiler_params=_params("parallel"),
        name="mixer_out",
    )(x, oa, og, sr, jnp.tile(g_gla_out, G_HEADS).reshape(1, -1), _seg_avg_matrix(G_WIDTH, G_DV),
      w_out[:A_WIDTH], w_out[A_WIDTH:])


def _head_rms(z, g_ref, scale=1.0):
    outs = []
    for head in range(X_HEADS):
        sl = slice(head * X_HEAD_DIM, (head + 1) * X_HEAD_DIM)
        zh = z[:, sl]
        outs.append(zh * (lax.rsqrt(jnp.mean(zh * zh, axis=-1, keepdims=True) + EPS) * scale) * g_ref[:, sl])
    return outs


def _mem_kv_kernel(m_ref, gm_ref, wk_ref, wv_ref, gk_ref, k_ref, v_ref):
    m = _rms(m_ref[...], gm_ref[...]).astype(BF16)
    k = _head_rms(jnp.dot(m, wk_ref[...], preferred_element_type=F32), gk_ref)
    v = jnp.dot(m, wv_ref[...], preferred_element_type=F32)
    for head in range(X_HEADS):
        k_ref[:, head, :] = k[head]
        v_ref[:, head, :] = v[:, head * X_HEAD_DIM:(head + 1) * X_HEAD_DIM]


def _mem_kv(mem, g_mem, w_ck, w_cv, g_ck):
    batch = mem.shape[0]
    sq = _full((D_MODEL, D_MODEL))
    vec = _full((1, D_MODEL))
    out = pl.BlockSpec((None, N_MEM, X_HEADS, X_HEAD_DIM), lambda i: (i, 0, 0, 0))
    return pl.pallas_call(
        _mem_kv_kernel,
        grid=(batch,),
        in_specs=[pl.BlockSpec((None, N_MEM, D_MODEL), lambda i: (i, 0, 0)), vec, sq, sq, vec],
        out_specs=[out, out],
        out_shape=[jax.ShapeDtypeStruct((batch, N_MEM, X_HEADS, X_HEAD_DIM), F32)] * 2,
        compiler_params=_params("parallel"),
        name="mem_kv",
    )(mem, g_mem.reshape(1, -1), w_ck.astype(BF16), w_cv.astype(BF16), jnp.tile(g_ck, X_HEADS).reshape(1, -1))


def _cross_kernel(x_ref, gc_ref, wq_ref, gq_ref, mk_ref, mv_ref, wo_ref, y_ref):
    x = x_ref[...]
    h = _rms(x, gc_ref[...]).astype(BF16)
    q = _head_rms(jnp.dot(h, wq_ref[...], preferred_element_type=F32), gq_ref, X_HEAD_DIM ** -0.5)
    outs = []
    for head in range(X_HEADS):
        mk = mk_ref[:, head, :].astype(BF16)
        mv = mv_ref[:, head, :].astype(BF16)
        s = lax.dot_general(q[head].astype(BF16), mk, (((1,), (1,)), ((), ())), preferred_element_type=F32)
        e = jnp.exp(s - jnp.max(s, axis=-1, keepdims=True))
        p = (e * (1.0 / jnp.sum(e, axis=-1, keepdims=True))).astype(BF16)
        outs.append(jnp.dot(p, mv, preferred_element_type=F32))
    o = jnp.concatenate(outs, axis=1).astype(BF16)
    y_ref[...] = x + jnp.dot(o, wo_ref[...], preferred_element_type=F32)


def _cross_streams_kernel(x_ref, gc_ref, wq_ref, gq_ref, mk_ref, mv_ref, wo_ref, y_ref, q_scr, o_scr, *, t):
    b = pl.program_id(0)

    @pl.when(b == 0)
    def _():
        h = _rms(x_ref[...], gc_ref[...]).astype(BF16)
        q = _head_rms(jnp.dot(h, wq_ref[...], preferred_element_type=F32), gq_ref, X_HEAD_DIM ** -0.5)
        q_scr[...] = jnp.concatenate(q, axis=1).astype(BF16)

    rows = pl.ds(pl.multiple_of(b * t, t), t)
    q = q_scr[rows, :]
    outs = []
    for head in range(X_HEADS):
        sl = slice(head * X_HEAD_DIM, (head + 1) * X_HEAD_DIM)
        s = lax.dot_general(q[:, sl], mk_ref[0, :, sl].astype(BF16), (((1,), (1,)), ((), ())),
                            preferred_element_type=F32)
        e = jnp.exp(s - jnp.max(s, axis=-1, keepdims=True))
        p = (e * (1.0 / jnp.sum(e, axis=-1, keepdims=True))).astype(BF16)
        outs.append(jnp.dot(p, mv_ref[0, :, sl].astype(BF16), preferred_element_type=F32))
    o_scr[rows, :] = jnp.concatenate(outs, axis=1).astype(BF16)

    @pl.when(b == pl.num_programs(0) - 1)
    def _():
        y_ref[...] = x_ref[...] + jnp.dot(o_scr[...], wo_ref[...], preferred_element_type=F32)


def _cross_streams(x, mem_k, mem_v, g_cross, w_cq, g_cq, w_co, batch, t):
    n = batch * t
    whole = _full((n, D_MODEL))
    mem = pl.BlockSpec((1, N_MEM, D_MODEL), lambda b: (b, 0, 0))
    sq = _full((D_MODEL, D_MODEL))
    vec = _full((1, D_MODEL))
    return pl.pallas_call(
        functools.partial(_cross_streams_kernel, t=t),
        grid=(batch,),
        in_specs=[whole, vec, sq, vec, mem, mem, sq],
        out_specs=whole,
        out_shape=jax.ShapeDtypeStruct((n, D_MODEL), F32),
        scratch_shapes=[pltpu.VMEM((n, D_MODEL), BF16), pltpu.VMEM((n, D_MODEL), BF16)],
        compiler_params=_params("arbitrary"),
        name="cross_streams",
    )(x, g_cross.reshape(1, -1), w_cq.astype(BF16), jnp.tile(g_cq, X_HEADS).reshape(1, -1),
      mem_k, mem_v, w_co.astype(BF16))


def _cross(x, mem_k, mem_v, g_cross, w_cq, g_cq, w_co, batch, t, tile):
    nt = t // tile
    row = pl.BlockSpec((tile, D_MODEL), lambda b, i: (b * nt + i, 0))
    mem = pl.BlockSpec((None, N_MEM, X_HEADS, X_HEAD_DIM), lambda b, i: (b, 0, 0, 0))
    sq = _full((D_MODEL, D_MODEL))
    vec = _full((1, D_MODEL))
    return pl.pallas_call(
        _cross_kernel,
        grid=(batch, nt),
        in_specs=[row, vec, sq, vec, mem, mem, sq],
        out_specs=row,
        out_shape=jax.ShapeDtypeStruct((batch * t, D_MODEL), F32),
        compiler_params=_params("parallel", "parallel"),
        name="cross",
    )(x, g_cross.reshape(1, -1), w_cq.astype(BF16), jnp.tile(g_cq, X_HEADS).reshape(1, -1),
      mem_k, mem_v, w_co.astype(BF16))


CAND_ROWS = 64


def _sorting_network(n):
    pairs = []
    p = 1
    while p < n:
        k = p
        while k >= 1:
            for j in range(k % p, n - k, 2 * k):
                for i in range(min(k, n - j - k)):
                    if (i + j) // (2 * p) == (i + j + k) // (2 * p):
                        pairs.append((i + j, i + j + k))
            k //= 2
        p *= 2
    return pairs


def _top_values_tiled(s, count):
    nparts = s.shape[0] // SUBLANES
    parts = [s[SUBLANES * k:SUBLANES * (k + 1), :] for k in range(nparts)]
    for i, j in _sorting_network(nparts):
        parts[i], parts[j] = jnp.maximum(parts[i], parts[j]), jnp.minimum(parts[i], parts[j])
    which = lax.broadcasted_iota(jnp.int32, parts[0].shape, 0).astype(F32)
    exhausted = jnp.full(parts[0].shape, -jnp.inf, F32)
    out = []
    for r in range(count):
        mx = jnp.max(parts[0], axis=0, keepdims=True)
        first = jnp.min(jnp.where(parts[0] == mx, which, float(SUBLANES)), axis=0, keepdims=True)
        pop = which == first
        out.append(mx)
        for k in range(min(count - r - 1, nparts)):
            parts[k] = jnp.where(pop, parts[k + 1] if k + 1 < nparts else exhausted, parts[k])
    return jnp.concatenate(out, axis=0)


def _route_kernel(x_ref, gf_ref, whi_ref, wlo_ref, sk1_ref, sk2_ref, ht_ref, r2_ref, e2_ref, n1_ref, c1_ref):
    hf = _rms(x_ref[...], gf_ref[...])
    ht_ref[...] = jnp.transpose(hf).astype(BF16)
    h_hi = hf.astype(BF16)
    h_lo = (hf - h_hi.astype(F32)).astype(BF16)
    qry = (jnp.dot(h_hi, whi_ref[...], preferred_element_type=F32)
           + jnp.dot(h_hi, wlo_ref[...], preferred_element_type=F32)
           + jnp.dot(h_lo, whi_ref[...], preferred_element_type=F32))
    nt = (((1,), (1,)), ((), ()))
    nk = P_TOPK + 1
    t = qry.shape[0]
    for head in range(P_HEADS):
        q1 = qry[:, head * P_QDIM:head * P_QDIM + P_HALF]
        q2 = qry[:, head * P_QDIM + P_HALF:(head + 1) * P_QDIM]
        s1 = lax.dot_general(sk1_ref[...], q1, nt, precision=HIGHEST, preferred_element_type=F32)
        s2 = lax.dot_general(sk2_ref[...], q2, nt, precision=HIGHEST, preferred_element_type=F32)
        v1 = _top_values_tiled(s1, nk)
        v2 = _top_values_tiled(s2, nk)
        slabs = [v1[a:a + 1, :] + v2[:nk // (a + 1), :] for a in range(nk)]
        pad = CAND_ROWS - sum(sl.shape[0] for sl in slabs)
        cand = jnp.concatenate(slabs + [jnp.full((pad, t), -jnp.inf, F32)], axis=0)
        top = _top_values_tiled(cand, nk)
        m1 = v1[0:1, :]
        m2 = v2[0:1, :]
        z = jnp.sum(jnp.exp(top[:P_TOPK, :] - (m1 + m2)), axis=0, keepdims=True)
        tau = 0.5 * (top[P_TOPK - 1:P_TOPK, :] + top[P_TOPK:nk, :])
        thr = tau - s1
        n1 = jnp.zeros_like(s1)
        rank2 = jnp.zeros_like(s2)
        for b in range(P_TOPK):
            n1 = n1 + jnp.where(v2[b:b + 1, :] >= thr, 1.0, 0.0)
            rank2 = rank2 + jnp.where(v2[b:b + 1, :] > s2, 1.0, 0.0)
        r2_ref[head] = rank2.astype(BF16)
        e2_ref[head] = jnp.exp(s2 - m2).astype(BF16)
        n1_ref[head] = n1
        c1_ref[head] = jnp.exp(s1 - m1) * (1.0 / z)


def _route(x, g_ffn, w_pq, sub_keys1, sub_keys2):
    n = x.shape[0]
    w_hi = w_pq.astype(BF16)
    w_lo = (w_pq - w_hi.astype(F32)).astype(BF16)
    tm = ROUTE_TILE
    keys = pl.BlockSpec((P_HEADS, P_NKEYS, tm), lambda i: (0, 0, i))
    tab = lambda dt: jax.ShapeDtypeStruct((P_HEADS, P_NKEYS, n), dt)
    return pl.pallas_call(
        _route_kernel,
        grid=(n // tm,),
        in_specs=[pl.BlockSpec((tm, D_MODEL), lambda i: (i, 0)), _full((1, D_MODEL)),
                  _full((D_MODEL, P_HEADS * P_QDIM)), _full((D_MODEL, P_HEADS * P_QDIM)),
                  _full((P_NKEYS, P_HALF)), _full((P_NKEYS, P_HALF))],
        out_specs=[pl.BlockSpec((D_MODEL, tm), lambda i: (0, i)), keys, keys, keys, keys],
        out_shape=[jax.ShapeDtypeStruct((D_MODEL, n), BF16), tab(BF16), tab(BF16), tab(F32), tab(F32)],
        compiler_params=_params("parallel"),
        name="peer_route",
    )(x, g_ffn.reshape(1, -1), w_hi, w_lo, sub_keys1, sub_keys2)


def _peer_kernel(ht_ref, x_ref, u_ref, vt_ref, r2_ref, e2_ref, n1_ref, c1_ref, y_ref, acc_ref):
    j = pl.program_id(1)

    @pl.when(j == 0)
    def _():
        acc_ref[...] = jnp.zeros_like(acc_ref)

    t = ht_ref.shape[1]
    n_i1 = PEER_EXPERT_TILE // P_NKEYS
    zero = jnp.zeros((), BF16)
    a = jnp.dot(u_ref[...], ht_ref[...], preferred_element_type=F32)
    act = (a * (1.0 + lax.erf(a * (1.0 / math.sqrt(2.0))))).astype(BF16)
    gates = []
    for r in range(n_i1):
        g = jnp.zeros((P_NKEYS, t), BF16)
        for head in range(P_HEADS):
            n1 = n1_ref[head, pl.ds(j * n_i1 + r, 1), :].astype(BF16)
            c1 = (0.5 * c1_ref[head, pl.ds(j * n_i1 + r, 1), :]).astype(BF16)
            g = g + jnp.where(r2_ref[head] < n1, e2_ref[head], zero) * c1
        gates.append(g)
    ga = act * jnp.concatenate(gates, axis=0)
    acc_ref[...] += jnp.dot(vt_ref[0], ga, preferred_element_type=F32)

    @pl.when(j == pl.num_programs(1) - 1)
    def _():
        y_ref[...] = x_ref[...] + jnp.transpose(acc_ref[...])


def _peer(ht, x, r2, e2, n1, c1, u, vt, tt):
    n = x.shape[0]
    et = PEER_EXPERT_TILE
    ne = u.shape[0] // et
    once = pl.Buffered(1)
    keys = pl.BlockSpec((P_HEADS, P_NKEYS, tt), lambda i, j: (0, 0, i))
    return pl.pallas_call(
        _peer_kernel,
        grid=(n // tt, ne),
        in_specs=[pl.BlockSpec((D_MODEL, tt), lambda i, j: (0, i)),
                  pl.BlockSpec((tt, D_MODEL), lambda i, j: (i, 0), pipeline_mode=once),
                  pl.BlockSpec((et, D_MODEL), lambda i, j: (j, 0)),
                  pl.BlockSpec((1, D_MODEL, et), lambda i, j: (j, 0, 0)),
                  keys, keys, keys, keys],
        out_specs=pl.BlockSpec((tt, D_MODEL), lambda i, j: (i, 0), pipeline_mode=once),
        out_shape=jax.ShapeDtypeStruct((n, D_MODEL), F32),
        scratch_shapes=[pltpu.VMEM((D_MODEL, tt), F32)],
        compiler_params=_params("parallel", "arbitrary"),
        name="peer_experts",
    )(ht, x, u, vt, r2, e2, n1, c1)


def _layer(xp, xs, cache_k, cache_v, state, cmem_k, cmem_v, mem_prompt, past_len,
           g_mix, w_in, g_q, g_k, rel_bias, w_gate2, b_gate2, g_gla_out, w_out,
           g_cross, g_mem, w_cq, w_ck, w_cv, g_cq, g_ck, w_co, g_ffn, w_pq, sub_keys1, sub_keys2, expert_u, expert_v):
    bp, sp, _ = xp.shape
    bs, ts, _ = xs.shape
    xp = xp.reshape(bp * sp, D_MODEL)
    xs = xs.reshape(bs * ts, D_MODEL)
    u = expert_u.astype(BF16)
    ne = expert_v.shape[0] // PEER_EXPERT_TILE
    vt = jnp.swapaxes(expert_v.reshape(ne, PEER_EXPERT_TILE, D_MODEL), 1, 2).astype(BF16)
    heads = lambda a, b, t: a.reshape(b, t, A_HEADS, A_HEAD_DIM)

    def peer(x, peer_tile):
        ht, r2, e2, n1, c1 = _route(x, g_ffn, w_pq, sub_keys1, sub_keys2)
        return _peer(ht, x, r2, e2, n1, c1, u, vt, peer_tile)

    qa, ka_f, ka_b, va_f, va_b, qg, kg, vg, la, sr = _mixer_in(xp, g_mix, w_in, g_q, g_k, w_gate2, b_gate2)
    oa = _attn_prompt(qa, ka_b, va_b, rel_bias, bp, sp)
    og, gla_p = _gla(qg, kg, vg, la, jnp.zeros((bp, G_HEADS, G_DK, G_DV), F32), bp, sp, GLA_TILE)
    xp = _mixer_out(xp, oa, og, sr, g_gla_out, w_out)
    mk, mv = _mem_kv(mem_prompt, g_mem, w_ck, w_cv, g_ck)
    xp = peer(_cross(xp, mk, mv, g_cross, w_cq, g_cq, w_co, bp, sp, TOKEN_TILE), PEER_TOKEN_TILE)
    keep = min(BAND, sp)
    outs_p = (heads(ka_f, bp, sp)[:, -keep:], heads(va_f, bp, sp)[:, -keep:], gla_p, mk, mv)

    qa, ka_f, ka_b, va_f, va_b, qg, kg, vg, la, sr = _mixer_in(xs, g_mix, w_in, g_q, g_k, w_gate2, b_gate2)
    oa = _attn_sample(qa, ka_b, va_b, cache_k, cache_v, rel_bias, bs, ts, past_len)
    og, gla_s = _gla(qg, kg, vg, la, state, bs, ts, ts)
    xs = _mixer_out(xs, oa, og, sr, g_gla_out, w_out)
    xs = _cross_streams(xs, cmem_k.reshape(bs, N_MEM, D_MODEL), cmem_v.reshape(bs, N_MEM, D_MODEL),
                        g_cross, w_cq, g_cq, w_co, bs, ts)
    xs = peer(xs, bs * ts)
    outs_s = (heads(ka_f, bs, ts), heads(va_f, bs, ts), gla_s)

    return (xp.reshape(bp, sp, D_MODEL), xs.reshape(bs, ts, D_MODEL)) + outs_p + outs_s


def kernel(x_prompt, x_sample, cache_att_k, cache_att_v, state_gla, cache_mem_k, cache_mem_v, mem_prompt, g_mix, w_in, g_q, g_k, rel_bias, w_gate2, b_gate2, g_gla_out, w_out, g_cross, g_mem, w_cq, w_ck, w_cv, g_cq, g_ck, w_co, g_ffn, w_pq, sub_keys1, sub_keys2, expert_u, expert_v):
    depth = w_in.shape[0]
    past_len = 4096
    xp, xs = x_prompt, x_sample
    per_layer = []
    for l in range(depth):
        outs = _layer(xp, xs, cache_att_k[l], cache_att_v[l], state_gla[l], cache_mem_k[l], cache_mem_v[l],
                      mem_prompt, past_len,
                      g_mix[l], w_in[l], g_q[l], g_k[l], rel_bias[l], w_gate2[l], b_gate2[l], g_gla_out[l], w_out[l],
                      g_cross[l], g_mem[l], w_cq[l], w_ck[l], w_cv[l], g_cq[l], g_ck[l], w_co[l],
                      g_ffn[l], w_pq[l], sub_keys1[l], sub_keys2[l], expert_u[l], expert_v[l])
        xp, xs = outs[0], outs[1]
        per_layer.append(outs[2:])
    stacked = [jnp.stack([p[i] for p in per_layer]) for i in range(8)]
    return (xp, xs) + tuple(stacked)
```

```python
import functools
import math

import jax
import jax.numpy as jnp
from jax import lax
from jax.experimental import pallas as pl
from jax.experimental.pallas import tpu as pltpu

F32 = jnp.float32
BF16 = jnp.bfloat16
HIGHEST = lax.Precision.HIGHEST

D_MODEL = 1024
CHUNK = 64
N_PAST_CHUNKS = 8
BAND = (N_PAST_CHUNKS + 1) * CHUNK
A_HEADS = 8
A_HEAD_DIM = 64
A_WIDTH = A_HEADS * A_HEAD_DIM
REL_CLIP = 128
G_HEADS = 4
G_DK = 64
G_DV = 128
G_KW = G_HEADS * G_DK
G_WIDTH = G_HEADS * G_DV
G_GATE_RANK = 16
G_GATE_NORM = 16.0
G_BLOCK = 16
N_MEM = 256
X_HEADS = 4
X_HEAD_DIM = D_MODEL // X_HEADS
P_HEADS = 8
P_NKEYS = 128
P_QDIM = 256
P_HALF = P_QDIM // 2
P_TOPK = 16
EPS = 1e-6
NEG = -1e30

LANES = 128
SUBLANES = 8
VMEM_LIMIT = 56 * 1024 * 1024

TOKEN_TILE = 512
ROUTE_TILE = 256
GLA_TILE = 256
PEER_TOKEN_TILE = 1024
PEER_EXPERT_TILE = 1024


def _params(*sem):
    return pltpu.CompilerParams(dimension_semantics=sem, vmem_limit_bytes=VMEM_LIMIT)


def _full(shape):
    return pl.BlockSpec(shape, lambda *_: (0,) * len(shape))


def _rms(x, g):
    return x * lax.rsqrt(jnp.mean(x * x, axis=-1, keepdims=True) + EPS) * g


def _seg_mean_sq(z, seg_avg):
    sq = z * z
    hi = sq.astype(BF16)
    lo = (sq - hi.astype(F32)).astype(BF16)
    return (jnp.dot(hi, seg_avg, preferred_element_type=F32)
            + jnp.dot(lo, seg_avg, preferred_element_type=F32))


def _seg_avg_matrix(width, seg):
    r = jnp.arange(width) // seg
    return jnp.where(r[:, None] == r[None, :], 1.0 / seg, 0.0).astype(BF16)


def _mixer_in_kernel(x_ref, gmix_ref, w_ref, gq_ref, gk_ref, avg_ref, wg2_ref, bg2_ref,
                     qa_ref, kaf_ref, kab_ref, vaf_ref, vab_ref, qg_ref, kg_ref, vg_ref, la_ref, sr_ref):
    h = _rms(x_ref[...], gmix_ref[...]).astype(BF16)
    z = jnp.dot(h, w_ref[...], preferred_element_type=F32)
    o = 0
    qa = z[:, o:o + A_WIDTH]; o += A_WIDTH
    ka = z[:, o:o + A_WIDTH]; o += A_WIDTH
    va = z[:, o:o + A_WIDTH]; o += A_WIDTH
    qg = z[:, o:o + G_KW]; o += G_KW
    kg = z[:, o:o + G_KW]; o += G_KW
    vg = z[:, o:o + G_WIDTH]; o += G_WIDTH
    r = z[:, o:o + G_WIDTH]; o += G_WIDTH
    gl = z[:, o:o + LANES]
    avg = avg_ref[...]
    qa = qa * lax.rsqrt(_seg_mean_sq(qa, avg) + EPS) * gq_ref[...]
    ka = ka * lax.rsqrt(_seg_mean_sq(ka, avg) + EPS) * gk_ref[...]
    qa_ref[...] = (qa * (A_HEAD_DIM ** -0.5)).astype(BF16)
    kaf_ref[...] = ka
    kab_ref[...] = ka.astype(BF16)
    vaf_ref[...] = va
    vab_ref[...] = va.astype(BF16)
    qg_ref[...] = qg * (G_DK ** -0.5)
    kg_ref[...] = kg
    vg_ref[...] = vg
    pre = jnp.dot(gl, wg2_ref[...], precision=HIGHEST, preferred_element_type=F32) + bg2_ref[...]
    la_ref[...] = (jnp.minimum(pre, 0.0) - jnp.log1p(jnp.exp(-jnp.abs(pre)))) / G_GATE_NORM
    sr_ref[...] = r * jax.nn.sigmoid(r)


def _mixer_in(x, g_mix, w_in, g_q, g_k, w_gate2, b_gate2):
    n = x.shape[0]
    c_gl = 3 * A_WIDTH + 2 * G_KW + G_WIDTH
    w_cat = jnp.concatenate(
        [w_in[:, :c_gl], w_in[:, c_gl + G_GATE_RANK:],
         jnp.pad(w_in[:, c_gl:c_gl + G_GATE_RANK], ((0, 0), (0, LANES - G_GATE_RANK)))], axis=1).astype(BF16)
    wg2 = jnp.pad(w_gate2, ((0, LANES - G_GATE_RANK), (0, 0)))
    wcols = w_cat.shape[1]
    tm = TOKEN_TILE
    row = lambda w: pl.BlockSpec((tm, w), lambda i: (i, 0))
    outs = [(A_WIDTH, BF16), (A_WIDTH, F32), (A_WIDTH, BF16), (A_WIDTH, F32), (A_WIDTH, BF16),
            (G_KW, F32), (G_KW, F32), (G_WIDTH, F32), (G_KW, F32), (G_WIDTH, F32)]
    return pl.pallas_call(
        _mixer_in_kernel,
        grid=(n // tm,),
        in_specs=[row(D_MODEL), _full((1, D_MODEL)), _full((D_MODEL, wcols)), _full((1, A_WIDTH)),
                  _full((1, A_WIDTH)), _full((A_WIDTH, A_WIDTH)), _full((LANES, G_KW)), _full((1, G_KW))],
        out_specs=[row(w) for w, _ in outs],
        out_shape=[jax.ShapeDtypeStruct((n, w), dt) for w, dt in outs],
        compiler_params=_params("parallel"),
        name="mixer_in",
    )(x, g_mix.reshape(1, -1), w_cat, jnp.tile(g_q, A_HEADS).reshape(1, -1), jnp.tile(g_k, A_HEADS).reshape(1, -1),
      _seg_avg_matrix(A_WIDTH, A_HEAD_DIM), wg2, b_gate2.reshape(1, -1))


def _attend(q, segments):
    tq = q.shape[0]
    lane = lax.broadcasted_iota(jnp.int32, (tq, LANES), 1)
    outs = []
    for pair in range(A_HEADS // 2):
        sl = slice(pair * LANES, (pair + 1) * LANES)
        q2 = q[:, sl]
        halves = []
        for half in range(2):
            head = 2 * pair + half
            mine = (lane < A_HEAD_DIM) if half == 0 else (lane >= A_HEAD_DIM)
            qm = jnp.where(mine, q2, jnp.zeros_like(q2))
            scores = [lax.dot_general(qm, k[:, sl], (((1,), (1,)), ((), ())), preferred_element_type=F32) + bias(head)
                      for k, _, bias in segments]
            m = functools.reduce(jnp.maximum, [jnp.max(s, axis=-1, keepdims=True) for s in scores])
            es = [jnp.exp(s - m) for s in scores]
            inv = 1.0 / functools.reduce(jnp.add, [jnp.sum(e, axis=-1, keepdims=True) for e in es])
            o = functools.reduce(jnp.add, [
                jnp.dot((e * inv).astype(BF16), v[:, sl], preferred_element_type=F32)
                for e, (_, v, _) in zip(es, segments)])
            halves.append(o)
        outs.append(jnp.where(lane < A_HEAD_DIM, halves[0], halves[1]))
    return jnp.concatenate(outs, axis=1)


ATT_GROUP = 4
ATT_ROWS = ATT_GROUP * CHUNK
ATT_WINDOW = (N_PAST_CHUNKS + ATT_GROUP) * CHUNK
ATT_CASES = -(-N_PAST_CHUNKS // ATT_GROUP) + 1
BIAS_WRAP = 1024


def _attn_prompt_kernel(q_ref, k_ref, v_ref, tb_ref, o_ref, bias_ref):
    g = pl.program_id(1)
    first = jnp.maximum(g * ATT_GROUP - N_PAST_CHUNKS, 0)
    start = pl.multiple_of(first * CHUNK, CHUNK)

    for case in range(ATT_CASES):
        @pl.when(g == case)
        def _(case=case):
            off = min(case * ATT_GROUP, N_PAST_CHUNKS) * CHUNK
            q_chunk = (off + lax.broadcasted_iota(jnp.int32, (ATT_ROWS, ATT_WINDOW), 0)) // CHUNK
            k_chunk = lax.broadcasted_iota(jnp.int32, (ATT_ROWS, ATT_WINDOW), 1) // CHUNK
            valid = (k_chunk <= q_chunk) & (k_chunk >= q_chunk - N_PAST_CHUNKS)
            for head in range(A_HEADS):
                table = jnp.broadcast_to(tb_ref[head:head + 1, :], (ATT_ROWS, BIAS_WRAP))
                rolled = pltpu.roll(table, off, 1, stride=1, stride_axis=0)
                bias_ref[head] = jnp.where(valid, rolled[:, :ATT_WINDOW], NEG)

    kb = k_ref[0, pl.ds(start, ATT_WINDOW), :]
    vb = v_ref[0, pl.ds(start, ATT_WINDOW), :]
    o_ref[...] = _attend(q_ref[...], [(kb, vb, lambda head: bias_ref[head])]).astype(o_ref.dtype)


def _wrapped_bias_table(rel_bias):
    u = jnp.arange(BIAS_WRAP)
    dist = jnp.where(u < CHUNK, -u, BIAS_WRAP - u)
    return rel_bias[:, jnp.clip(dist, -REL_CLIP, REL_CLIP) + REL_CLIP].astype(F32)


def _attn_prompt(qa, ka, va, rel_bias, batch, seq):
    ng = seq // ATT_ROWS
    assert seq % ATT_ROWS == 0 and ng >= ATT_CASES and seq >= ATT_WINDOW
    kv = pl.BlockSpec((1, seq, A_WIDTH), lambda b, g: (b, 0, 0))
    return pl.pallas_call(
        _attn_prompt_kernel,
        grid=(batch, ng),
        in_specs=[pl.BlockSpec((ATT_ROWS, A_WIDTH), lambda b, g: (b * ng + g, 0)), kv, kv,
                  _full((A_HEADS, BIAS_WRAP))],
        out_specs=pl.BlockSpec((ATT_ROWS, A_WIDTH), lambda b, g: (b * ng + g, 0)),
        out_shape=jax.ShapeDtypeStruct((batch * seq, A_WIDTH), BF16),
        scratch_shapes=[pltpu.VMEM((A_HEADS, ATT_ROWS, ATT_WINDOW), F32)],
        compiler_params=_params("arbitrary", "arbitrary"),
        name="attn_prompt",
    )(qa, ka.reshape(batch, seq, A_WIDTH), va.reshape(batch, seq, A_WIDTH), _wrapped_bias_table(rel_bias))


SAMPLE_STREAMS = 4


def _attn_sample_kernel(q_ref, kn_ref, vn_ref, kc_ref, vc_ref, tb_ref, o_ref, *, t, first_valid):
    r = kc_ref.shape[1]
    valid_c = lax.broadcasted_iota(jnp.int32, (t, r), 1) >= first_valid
    bias_c, bias_n = [], []
    for head in range(A_HEADS):
        table = jnp.broadcast_to(tb_ref[head:head + 1, :], (t, BIAS_WRAP))
        bias_c.append(jnp.where(valid_c, pltpu.roll(table, r % BIAS_WRAP, 1, stride=1, stride_axis=0)[:, :r], NEG))
        bias_n.append(pltpu.roll(table, 0, 1, stride=1, stride_axis=0)[:, :t])
    for stream in range(kc_ref.shape[0]):
        rows = slice(stream * t, (stream + 1) * t)
        segs = [(kc_ref[stream].astype(BF16), vc_ref[stream].astype(BF16), lambda head: bias_c[head]),
                (kn_ref[rows, :], vn_ref[rows, :], lambda head: bias_n[head])]
        o_ref[rows, :] = _attend(q_ref[rows, :], segs).astype(o_ref.dtype)


def _attn_sample(qa, ka, va, cache_k, cache_v, rel_bias, batch, t, past_len):
    r = cache_k.shape[1]
    q_chunk = past_len // CHUNK
    assert (past_len + t - 1) // CHUNK == q_chunk and r <= BAND
    first_valid = max(0, (q_chunk - N_PAST_CHUNKS) * CHUNK - (past_len - r))
    sb = math.gcd(batch, SAMPLE_STREAMS)
    tok = pl.BlockSpec((sb * t, A_WIDTH), lambda b: (b, 0))
    cache = pl.BlockSpec((sb, r, A_WIDTH), lambda b: (b, 0, 0))
    return pl.pallas_call(
        functools.partial(_attn_sample_kernel, t=t, first_valid=first_valid),
        grid=(batch // sb,),
        in_specs=[tok, tok, tok, cache, cache, _full((A_HEADS, BIAS_WRAP))],
        out_specs=tok,
        out_shape=jax.ShapeDtypeStruct((batch * t, A_WIDTH), BF16),
        compiler_params=_params("parallel"),
        name="attn_sample",
    )(qa, ka, va, cache_k.reshape(batch, r, A_WIDTH), cache_v.reshape(batch, r, A_WIDTH),
      _wrapped_bias_table(rel_bias))


def _gla_kernel(q_ref, k_ref, v_ref, la_ref, *rest):
    s0_ref = rest[0] if len(rest) == 7 else None
    tri_ref, expand_ref, bdmask_ref, o_ref, sfin_ref, state_ref = rest[-6:]
    tb = q_ref.shape[0]
    nb = tb // G_BLOCK

    @pl.when(pl.program_id(1) == 0)
    def _():
        state_ref[...] = jnp.zeros_like(state_ref) if s0_ref is None else s0_ref[0]

    q = q_ref[...]
    k = k_ref[...]
    v = v_ref[...]
    cum = jnp.dot(tri_ref[...], la_ref[...], precision=HIGHEST, preferred_element_type=F32)
    q3 = q.reshape(nb, G_BLOCK, G_KW)
    k3 = k.reshape(nb, G_BLOCK, G_KW)
    v3 = v.reshape(nb, G_BLOCK, G_WIDTH)
    cum3 = cum.reshape(nb, G_BLOCK, G_KW)
    row = lax.broadcasted_iota(jnp.int32, (nb, G_BLOCK, G_KW), 1)
    expand = expand_ref[...]

    intra = jnp.zeros((nb, G_BLOCK, G_WIDTH), F32)
    for s in range(G_BLOCK):
        k_s = jnp.broadcast_to(k3[:, s:s + 1, :], (nb, G_BLOCK, G_KW))
        c_s = jnp.broadcast_to(cum3[:, s:s + 1, :], (nb, G_BLOCK, G_KW))
        v_s = jnp.broadcast_to(v3[:, s:s + 1, :], (nb, G_BLOCK, G_WIDTH))
        prod = q3 * k_s * jnp.exp(jnp.where(row >= s, cum3 - c_s, -jnp.inf))
        att = jnp.dot(prod.reshape(tb, G_KW).astype(BF16), expand, preferred_element_type=F32)
        intra = intra + att.reshape(nb, G_BLOCK, G_WIDTH) * v_s
    o_ref[...] = intra.reshape(tb, G_WIDTH)

    last3 = jnp.broadcast_to(cum3[:, G_BLOCK - 1:G_BLOCK, :], (nb, G_BLOCK, G_KW))
    q_dec = (q3 * jnp.exp(cum3)).reshape(tb, G_KW).astype(BF16)
    k_dec = (k3 * jnp.exp(last3 - cum3)).reshape(tb, G_KW).astype(BF16)
    carry = jnp.exp(last3).reshape(tb, G_KW)
    vb = v.astype(BF16)
    bdmask = bdmask_ref[...]
    for j in range(nb):
        rows = slice(j * G_BLOCK, (j + 1) * G_BLOCK)
        state = state_ref[...]
        o_ref[rows, :] += lax.dot_general(q_dec[rows], state.astype(BF16), (((1,), (1,)), ((), ())),
                                          preferred_element_type=F32)
        upd = lax.dot_general(vb[rows], k_dec[rows], (((0,), (0,)), ((), ())), preferred_element_type=F32)
        state_ref[...] = carry[j * G_BLOCK:j * G_BLOCK + 1, :] * state + upd * bdmask

    @pl.when(pl.program_id(1) == pl.num_programs(1) - 1)
    def _():
        sfin_ref[0] = state_ref[...]


def _gla(qg, kg, vg, la, s0, batch, t, tile):
    nt = t // tile
    hsel = jnp.arange(G_KW) // G_DK
    vsel = jnp.arange(G_WIDTH) // G_DV
    same = hsel[:, None] == vsel[None, :]
    expand = same.astype(BF16)
    bdmask = same.T.astype(F32)
    state_in = []
    if s0 is not None:
        s0_t = jnp.swapaxes(s0, 2, 3).reshape(batch, G_WIDTH, G_DK)
        state_in = [jnp.where(same.T[None], jnp.tile(s0_t, (1, 1, G_HEADS)), 0.0)]
    rr = jnp.arange(tile)
    tri = ((rr[:, None] // G_BLOCK == rr[None, :] // G_BLOCK) & (rr[None, :] <= rr[:, None])).astype(F32)
    tok = lambda w: pl.BlockSpec((tile, w), lambda b, i: (b * nt + i, 0))
    st = pl.BlockSpec((1, G_WIDTH, G_KW), lambda b, i: (b, 0, 0))
    o, s_bd = pl.pallas_call(
        _gla_kernel,
        grid=(batch, nt),
        in_specs=[tok(G_KW), tok(G_KW), tok(G_WIDTH), tok(G_KW)] + [st] * len(state_in)
        + [_full((tile, tile)), _full((G_KW, G_WIDTH)), _full((G_WIDTH, G_KW))],
        out_specs=[tok(G_WIDTH), st],
        out_shape=[jax.ShapeDtypeStruct((batch * t, G_WIDTH), F32),
                   jax.ShapeDtypeStruct((batch, G_WIDTH, G_KW), F32)],
        scratch_shapes=[pltpu.VMEM((G_WIDTH, G_KW), F32)],
        compiler_params=_params("parallel", "arbitrary"),
        name="gla",
    )(qg, kg, vg, la, *state_in, tri, expand, bdmask)
    s_fin = jnp.stack([s_bd[:, h * G_DV:(h + 1) * G_DV, h * G_DK:(h + 1) * G_DK] for h in range(G_HEADS)], axis=1)
    return o, jnp.swapaxes(s_fin, 2, 3)


def _mixer_out_kernel(x_ref, oa_ref, og_ref, sr_ref, gg_ref, avg_ref, wa_ref, wg_ref, y_ref):
    og = og_ref[...]
    og = og * lax.rsqrt(_seg_mean_sq(og, avg_ref[...]) + EPS) * gg_ref[...] * sr_ref[...]
    y = jnp.dot(oa_ref[...], wa_ref[...], preferred_element_type=F32)
    y = y + jnp.dot(og.astype(BF16), wg_ref[...], preferred_element_type=F32)
    y_ref[...] = x_ref[...] + y


def _mixer_out(x, oa, og, sr, g_gla_out, w_out):
    n = x.shape[0]
    tm = TOKEN_TILE
    row = lambda w: pl.BlockSpec((tm, w), lambda i: (i, 0))
    w_out = w_out.astype(BF16)
    return pl.pallas_call(
        _mixer_out_kernel,
        grid=(n // tm,),
        in_specs=[row(D_MODEL), row(A_WIDTH), row(G_WIDTH), row(G_WIDTH), _full((1, G_WIDTH)),
                  _full((G_WIDTH, G_WIDTH)), _full((A_WIDTH, D_MODEL)), _full((G_WIDTH, D_MODEL))],
        out_specs=row(D_MODEL),
        out_shape=jax.ShapeDtypeStruct((n, D_MODEL), F32),
        compiler_params=_params("parallel"),
        name="mixer_out",
    )(x, oa, og, sr, jnp.tile(g_gla_out, G_HEADS).reshape(1, -1), _seg_avg_matrix(G_WIDTH, G_DV),
      w_out[:A_WIDTH], w_out[A_WIDTH:])


def _head_rms(z, g_ref, scale=1.0):
    outs = []
    for head in range(X_HEADS):
        sl = slice(head * X_HEAD_DIM, (head + 1) * X_HEAD_DIM)
        zh = z[:, sl]
        outs.append(zh * (lax.rsqrt(jnp.mean(zh * zh, axis=-1, keepdims=True) + EPS) * scale) * g_ref[:, sl])
    return outs


def _mem_kv_kernel(m_ref, gm_ref, wk_ref, wv_ref, gk_ref, k_ref, v_ref, kb_ref, vb_ref):
    m = _rms(m_ref[...], gm_ref[...]).astype(BF16)
    k = _head_rms(jnp.dot(m, wk_ref[...], preferred_element_type=F32), gk_ref)
    v = jnp.dot(m, wv_ref[...], preferred_element_type=F32)
    for head in range(X_HEADS):
        k_ref[:, head, :] = k[head]
        v_ref[:, head, :] = v[:, head * X_HEAD_DIM:(head + 1) * X_HEAD_DIM]
    kb_ref[...] = jnp.concatenate(k, axis=1).astype(BF16)
    vb_ref[...] = v.astype(BF16)


def _mem_kv(mem, g_mem, w_ck, w_cv, g_ck):
    batch = mem.shape[0]
    sq = _full((D_MODEL, D_MODEL))
    vec = _full((1, D_MODEL))
    out = pl.BlockSpec((None, N_MEM, X_HEADS, X_HEAD_DIM), lambda i: (i, 0, 0, 0))
    flat = pl.BlockSpec((None, N_MEM, D_MODEL), lambda i: (i, 0, 0))
    return pl.pallas_call(
        _mem_kv_kernel,
        grid=(batch,),
        in_specs=[flat, vec, sq, sq, vec],
        out_specs=[out, out, flat, flat],
        out_shape=[jax.ShapeDtypeStruct((batch, N_MEM, X_HEADS, X_HEAD_DIM), F32)] * 2
        + [jax.ShapeDtypeStruct((batch, N_MEM, D_MODEL), BF16)] * 2,
        compiler_params=_params("parallel"),
        name="mem_kv",
    )(mem, g_mem.reshape(1, -1), w_ck.astype(BF16), w_cv.astype(BF16), jnp.tile(g_ck, X_HEADS).reshape(1, -1))


def _mixer_out_cross_kernel(x_ref, oa_ref, og_ref, sr_ref, gg_ref, avg_ref, wa_ref, wg_ref,
                            gc_ref, wq_ref, gq_ref, mk_ref, mv_ref, wo_ref, y_ref):
    og = og_ref[...]
    og = og * lax.rsqrt(_seg_mean_sq(og, avg_ref[...]) + EPS) * gg_ref[...] * sr_ref[...]
    x = x_ref[...] + jnp.dot(oa_ref[...], wa_ref[...], preferred_element_type=F32)
    x = x + jnp.dot(og.astype(BF16), wg_ref[...], preferred_element_type=F32)
    h = _rms(x, gc_ref[...]).astype(BF16)
    q = _head_rms(jnp.dot(h, wq_ref[...], preferred_element_type=F32), gq_ref, X_HEAD_DIM ** -0.5)
    outs = []
    for head in range(X_HEADS):
        sl = slice(head * X_HEAD_DIM, (head + 1) * X_HEAD_DIM)
        s = lax.dot_general(q[head].astype(BF16), mk_ref[:, sl], (((1,), (1,)), ((), ())),
                            preferred_element_type=F32)
        e = jnp.exp(s - jnp.max(s, axis=-1, keepdims=True))
        p = (e * (1.0 / jnp.sum(e, axis=-1, keepdims=True))).astype(BF16)
        outs.append(jnp.dot(p, mv_ref[:, sl], preferred_element_type=F32))
    o = jnp.concatenate(outs, axis=1).astype(BF16)
    y_ref[...] = x + jnp.dot(o, wo_ref[...], preferred_element_type=F32)


def _cross_streams_kernel(x_ref, gc_ref, wq_ref, gq_ref, mk_ref, mv_ref, wo_ref, y_ref, q_scr, o_scr, *, t):
    b = pl.program_id(0)

    @pl.when(b == 0)
    def _():
        h = _rms(x_ref[...], gc_ref[...]).astype(BF16)
        q = _head_rms(jnp.dot(h, wq_ref[...], preferred_element_type=F32), gq_ref, X_HEAD_DIM ** -0.5)
        q_scr[...] = jnp.concatenate(q, axis=1).astype(BF16)

    rows = pl.ds(pl.multiple_of(b * t, t), t)
    q = q_scr[rows, :]
    outs = []
    for head in range(X_HEADS):
        sl = slice(head * X_HEAD_DIM, (head + 1) * X_HEAD_DIM)
        s = lax.dot_general(q[:, sl], mk_ref[0, :, sl].astype(BF16), (((1,), (1,)), ((), ())),
                            preferred_element_type=F32)
        e = jnp.exp(s - jnp.max(s, axis=-1, keepdims=True))
        p = (e * (1.0 / jnp.sum(e, axis=-1, keepdims=True))).astype(BF16)
        outs.append(jnp.dot(p, mv_ref[0, :, sl].astype(BF16), preferred_element_type=F32))
    o_scr[rows, :] = jnp.concatenate(outs, axis=1).astype(BF16)

    @pl.when(b == pl.num_programs(0) - 1)
    def _():
        y_ref[...] = x_ref[...] + jnp.dot(o_scr[...], wo_ref[...], preferred_element_type=F32)


def _cross_streams(x, mem_k, mem_v, g_cross, w_cq, g_cq, w_co, batch, t):
    n = batch * t
    whole = _full((n, D_MODEL))
    mem = pl.BlockSpec((1, N_MEM, D_MODEL), lambda b: (b, 0, 0))
    sq = _full((D_MODEL, D_MODEL))
    vec = _full((1, D_MODEL))
    return pl.pallas_call(
        functools.partial(_cross_streams_kernel, t=t),
        grid=(batch,),
        in_specs=[whole, vec, sq, vec, mem, mem, sq],
        out_specs=whole,
        out_shape=jax.ShapeDtypeStruct((n, D_MODEL), F32),
        scratch_shapes=[pltpu.VMEM((n, D_MODEL), BF16), pltpu.VMEM((n, D_MODEL), BF16)],
        compiler_params=_params("arbitrary"),
        name="cross_streams",
    )(x, g_cross.reshape(1, -1), w_cq.astype(BF16), jnp.tile(g_cq, X_HEADS).reshape(1, -1),
      mem_k, mem_v, w_co.astype(BF16))


def _mixer_out_cross(x, oa, og, sr, g_gla_out, w_out, mem_k, mem_v, g_cross, w_cq, g_cq, w_co, batch, t, tile):
    nt = t // tile
    row = lambda w: pl.BlockSpec((tile, w), lambda b, i: (b * nt + i, 0))
    mem = pl.BlockSpec((None, N_MEM, D_MODEL), lambda b, i: (b, 0, 0))
    sq = _full((D_MODEL, D_MODEL))
    vec = _full((1, D_MODEL))
    w_out = w_out.astype(BF16)
    return pl.pallas_call(
        _mixer_out_cross_kernel,
        grid=(batch, nt),
        in_specs=[row(D_MODEL), row(A_WIDTH), row(G_WIDTH), row(G_WIDTH), _full((1, G_WIDTH)),
                  _full((G_WIDTH, G_WIDTH)), _full((A_WIDTH, D_MODEL)), _full((G_WIDTH, D_MODEL)),
                  vec, sq, vec, mem, mem, sq],
        out_specs=row(D_MODEL),
        out_shape=jax.ShapeDtypeStruct((batch * t, D_MODEL), F32),
        compiler_params=_params("parallel", "parallel"),
        name="mixer_out_cross",
    )(x, oa, og, sr, jnp.tile(g_gla_out, G_HEADS).reshape(1, -1), _seg_avg_matrix(G_WIDTH, G_DV),
      w_out[:A_WIDTH], w_out[A_WIDTH:], g_cross.reshape(1, -1), w_cq.astype(BF16),
      jnp.tile(g_cq, X_HEADS).reshape(1, -1), mem_k, mem_v, w_co.astype(BF16))


CAND_ROWS = 64


def _sorting_network(n):
    pairs = []
    p = 1
    while p < n:
        k = p
        while k >= 1:
            for j in range(k % p, n - k, 2 * k):
                for i in range(min(k, n - j - k)):
                    if (i + j) // (2 * p) == (i + j + k) // (2 * p):
                        pairs.append((i + j, i + j + k))
            k //= 2
        p *= 2
    return pairs


def _top_values_tiled(s, count):
    nparts = s.shape[0] // SUBLANES
    parts = [s[SUBLANES * k:SUBLANES * (k + 1), :] for k in range(nparts)]
    for i, j in _sorting_network(nparts):
        parts[i], parts[j] = jnp.maximum(parts[i], parts[j]), jnp.minimum(parts[i], parts[j])
    which = lax.broadcasted_iota(jnp.int32, parts[0].shape, 0).astype(F32)
    exhausted = jnp.full(parts[0].shape, -jnp.inf, F32)
    out = []
    for r in range(count):
        mx = jnp.max(parts[0], axis=0, keepdims=True)
        first = jnp.min(jnp.where(parts[0] == mx, which, float(SUBLANES)), axis=0, keepdims=True)
        pop = which == first
        out.append(mx)
        for k in range(min(count - r - 1, nparts)):
            parts[k] = jnp.where(pop, parts[k + 1] if k + 1 < nparts else exhausted, parts[k])
    return jnp.concatenate(out, axis=0)


def _route_kernel(x_ref, gf_ref, whi_ref, wlo_ref, sk1_ref, sk2_ref, ht_ref, r2_ref, e2_ref, n1_ref, c1_ref):
    hf = _rms(x_ref[...], gf_ref[...])
    ht_ref[...] = jnp.transpose(hf).astype(BF16)
    h_hi = hf.astype(BF16)
    h_lo = (hf - h_hi.astype(F32)).astype(BF16)
    qry = (jnp.dot(h_hi, whi_ref[...], preferred_element_type=F32)
           + jnp.dot(h_hi, wlo_ref[...], preferred_element_type=F32)
           + jnp.dot(h_lo, whi_ref[...], preferred_element_type=F32))
    nt = (((1,), (1,)), ((), ()))
    nk = P_TOPK + 1
    t = qry.shape[0]
    for head in range(P_HEADS):
        q1 = qry[:, head * P_QDIM:head * P_QDIM + P_HALF]
        q2 = qry[:, head * P_QDIM + P_HALF:(head + 1) * P_QDIM]
        s1 = lax.dot_general(sk1_ref[...], q1, nt, precision=HIGHEST, preferred_element_type=F32)
        s2 = lax.dot_general(sk2_ref[...], q2, nt, precision=HIGHEST, preferred_element_type=F32)
        v1 = _top_values_tiled(s1, nk)
        v2 = _top_values_tiled(s2, nk)
        slabs = [v1[a:a + 1, :] + v2[:nk // (a + 1), :] for a in range(nk)]
        pad = CAND_ROWS - sum(sl.shape[0] for sl in slabs)
        cand = jnp.concatenate(slabs + [jnp.full((pad, t), -jnp.inf, F32)], axis=0)
        top = _top_values_tiled(cand, nk)
        m1 = v1[0:1, :]
        m2 = v2[0:1, :]
        z = jnp.sum(jnp.exp(top[:P_TOPK, :] - (m1 + m2)), axis=0, keepdims=True)
        tau = 0.5 * (top[P_TOPK - 1:P_TOPK, :] + top[P_TOPK:nk, :])
        thr = tau - s1
        n1 = jnp.zeros_like(s1)
        rank2 = jnp.zeros_like(s2)
        for b in range(P_TOPK):
            n1 = n1 + jnp.where(v2[b:b + 1, :] >= thr, 1.0, 0.0)
            rank2 = rank2 + jnp.where(v2[b:b + 1, :] > s2, 1.0, 0.0)
        r2_ref[head] = rank2.astype(BF16)
        e2_ref[head] = jnp.exp(s2 - m2).astype(BF16)
        n1_ref[head] = n1
        c1_ref[head] = jnp.exp(s1 - m1) * (1.0 / z)


def _route(x, g_ffn, w_pq, sub_keys1, sub_keys2):
    n = x.shape[0]
    w_hi = w_pq.astype(BF16)
    w_lo = (w_pq - w_hi.astype(F32)).astype(BF16)
    tm = ROUTE_TILE
    keys = pl.BlockSpec((P_HEADS, P_NKEYS, tm), lambda i: (0, 0, i))
    tab = lambda dt: jax.ShapeDtypeStruct((P_HEADS, P_NKEYS, n), dt)
    return pl.pallas_call(
        _route_kernel,
        grid=(n // tm,),
        in_specs=[pl.BlockSpec((tm, D_MODEL), lambda i: (i, 0)), _full((1, D_MODEL)),
                  _full((D_MODEL, P_HEADS * P_QDIM)), _full((D_MODEL, P_HEADS * P_QDIM)),
                  _full((P_NKEYS, P_HALF)), _full((P_NKEYS, P_HALF))],
        out_specs=[pl.BlockSpec((D_MODEL, tm), lambda i: (0, i)), keys, keys, keys, keys],
        out_shape=[jax.ShapeDtypeStruct((D_MODEL, n), BF16), tab(BF16), tab(BF16), tab(F32), tab(F32)],
        compiler_params=_params("parallel"),
        name="peer_route",
    )(x, g_ffn.reshape(1, -1), w_hi, w_lo, sub_keys1, sub_keys2)


def _peer_kernel(ht_ref, x_ref, u_ref, vt_ref, r2_ref, e2_ref, n1_ref, c1_ref, y_ref, acc_ref):
    j = pl.program_id(1)

    @pl.when(j == 0)
    def _():
        acc_ref[...] = jnp.zeros_like(acc_ref)

    t = ht_ref.shape[1]
    n_i1 = PEER_EXPERT_TILE // P_NKEYS
    zero = jnp.zeros((), BF16)
    a = jnp.dot(u_ref[...], ht_ref[...], preferred_element_type=F32)
    act = (a * (1.0 + lax.erf(a * (1.0 / math.sqrt(2.0))))).astype(BF16)
    gates = []
    for r in range(n_i1):
        g = jnp.zeros((P_NKEYS, t), BF16)
        for head in range(P_HEADS):
            n1 = n1_ref[head, pl.ds(j * n_i1 + r, 1), :].astype(BF16)
            c1 = (0.5 * c1_ref[head, pl.ds(j * n_i1 + r, 1), :]).astype(BF16)
            g = g + jnp.where(r2_ref[head] < n1, e2_ref[head], zero) * c1
        gates.append(g)
    ga = act * jnp.concatenate(gates, axis=0)
    acc_ref[...] += jnp.dot(vt_ref[0], ga, preferred_element_type=F32)

    @pl.when(j == pl.num_programs(1) - 1)
    def _():
        y_ref[...] = x_ref[...] + jnp.transpose(acc_ref[...])


def _peer(ht, x, r2, e2, n1, c1, u, vt, tt):
    n = x.shape[0]
    et = PEER_EXPERT_TILE
    ne = u.shape[0] // et
    once = pl.Buffered(1)
    keys = pl.BlockSpec((P_HEADS, P_NKEYS, tt), lambda i, j: (0, 0, i))
    return pl.pallas_call(
        _peer_kernel,
        grid=(n // tt, ne),
        in_specs=[pl.BlockSpec((D_MODEL, tt), lambda i, j: (0, i)),
                  pl.BlockSpec((tt, D_MODEL), lambda i, j: (i, 0), pipeline_mode=once),
                  pl.BlockSpec((et, D_MODEL), lambda i, j: (j, 0)),
                  pl.BlockSpec((1, D_MODEL, et), lambda i, j: (j, 0, 0)),
                  keys, keys, keys, keys],
        out_specs=pl.BlockSpec((tt, D_MODEL), lambda i, j: (i, 0), pipeline_mode=once),
        out_shape=jax.ShapeDtypeStruct((n, D_MODEL), F32),
        scratch_shapes=[pltpu.VMEM((D_MODEL, tt), F32)],
        compiler_params=_params("parallel", "arbitrary"),
        name="peer_experts",
    )(ht, x, u, vt, r2, e2, n1, c1)


def _layer(xp, xs, cache_k, cache_v, state, cmem_k, cmem_v, mem_prompt, past_len,
           g_mix, w_in, g_q, g_k, rel_bias, w_gate2, b_gate2, g_gla_out, w_out,
           g_cross, g_mem, w_cq, w_ck, w_cv, g_cq, g_ck, w_co, g_ffn, w_pq, sub_keys1, sub_keys2, expert_u, expert_v):
    bp, sp, _ = xp.shape
    bs, ts, _ = xs.shape
    xp = xp.reshape(bp * sp, D_MODEL)
    xs = xs.reshape(bs * ts, D_MODEL)
    u = expert_u.astype(BF16)
    ne = expert_v.shape[0] // PEER_EXPERT_TILE
    vt = jnp.swapaxes(expert_v.reshape(ne, PEER_EXPERT_TILE, D_MODEL), 1, 2).astype(BF16)
    heads = lambda a, b, t: a.reshape(b, t, A_HEADS, A_HEAD_DIM)

    def peer(x, peer_tile):
        ht, r2, e2, n1, c1 = _route(x, g_ffn, w_pq, sub_keys1, sub_keys2)
        return _peer(ht, x, r2, e2, n1, c1, u, vt, peer_tile)

    qa, ka_f, ka_b, va_f, va_b, qg, kg, vg, la, sr = _mixer_in(xp, g_mix, w_in, g_q, g_k, w_gate2, b_gate2)
    oa = _attn_prompt(qa, ka_b, va_b, rel_bias, bp, sp)
    og, gla_p = _gla(qg, kg, vg, la, None, bp, sp, GLA_TILE)
    mk, mv, mk_b, mv_b = _mem_kv(mem_prompt, g_mem, w_ck, w_cv, g_ck)
    xp = peer(_mixer_out_cross(xp, oa, og, sr, g_gla_out, w_out, mk_b, mv_b, g_cross, w_cq, g_cq, w_co,
                               bp, sp, TOKEN_TILE), PEER_TOKEN_TILE)
    keep = min(BAND, sp)
    outs_p = (heads(ka_f, bp, sp)[:, -keep:], heads(va_f, bp, sp)[:, -keep:], gla_p, mk, mv)

    qa, ka_f, ka_b, va_f, va_b, qg, kg, vg, la, sr = _mixer_in(xs, g_mix, w_in, g_q, g_k, w_gate2, b_gate2)
    oa = _attn_sample(qa, ka_b, va_b, cache_k, cache_v, rel_bias, bs, ts, past_len)
    og, gla_s = _gla(qg, kg, vg, la, state, bs, ts, ts)
    xs = _mixer_out(xs, oa, og, sr, g_gla_out, w_out)
    xs = _cross_streams(xs, cmem_k.reshape(bs, N_MEM, D_MODEL), cmem_v.reshape(bs, N_MEM, D_MODEL),
                        g_cross, w_cq, g_cq, w_co, bs, ts)
    xs = peer(xs, bs * ts)
    outs_s = (heads(ka_f, bs, ts), heads(va_f, bs, ts), gla_s)

    return (xp.reshape(bp, sp, D_MODEL), xs.reshape(bs, ts, D_MODEL)) + outs_p + outs_s


def kernel(x_prompt, x_sample, cache_att_k, cache_att_v, state_gla, cache_mem_k, cache_mem_v, mem_prompt, g_mix, w_in, g_q, g_k, rel_bias, w_gate2, b_gate2, g_gla_out, w_out, g_cross, g_mem, w_cq, w_ck, w_cv, g_cq, g_ck, w_co, g_ffn, w_pq, sub_keys1, sub_keys2, expert_u, expert_v):
    depth = w_in.shape[0]
    past_len = 4096
    xp, xs = x_prompt, x_sample
    per_layer = []
    for l in range(depth):
        outs = _layer(xp, xs, cache_att_k[l], cache_att_v[l], state_gla[l], cache_mem_k[l], cache_mem_v[l],
                      mem_prompt, past_len,
                      g_mix[l], w_in[l], g_q[l], g_k[l], rel_bias[l], w_gate2[l], b_gate2[l], g_gla_out[l], w_out[l],
                      g_cross[l], g_mem[l], w_cq[l], w_ck[l], w_cv[l], g_cq[l], g_ck[l], w_co[l],
                      g_ffn[l], w_pq[l], sub_keys1[l], sub_keys2[l], expert_u[l], expert_v[l])
        xp, xs = outs[0], outs[1]
        per_layer.append(outs[2:])
    stacked = [jnp.stack([p[i] for p in per_layer]) for i in range(8)]
    return (xp, xs) + tuple(stacked)
```

```python
import functools
import math

import jax
import jax.numpy as jnp
from jax import lax
from jax.experimental import pallas as pl
from jax.experimental.pallas import tpu as pltpu

F32 = jnp.float32
BF16 = jnp.bfloat16
HIGHEST = lax.Precision.HIGHEST

D_MODEL = 1024
CHUNK = 64
N_PAST_CHUNKS = 8
BAND = (N_PAST_CHUNKS + 1) * CHUNK
A_HEADS = 8
A_HEAD_DIM = 64
A_WIDTH = A_HEADS * A_HEAD_DIM
REL_CLIP = 128
G_HEADS = 4
G_DK = 64
G_DV = 128
G_KW = G_HEADS * G_DK
G_WIDTH = G_HEADS * G_DV
G_GATE_RANK = 16
G_GATE_NORM = 16.0
G_BLOCK = 16
N_MEM = 256
X_HEADS = 4
X_HEAD_DIM = D_MODEL // X_HEADS
P_HEADS = 8
P_NKEYS = 128
P_QDIM = 256
P_HALF = P_QDIM // 2
P_TOPK = 16
EPS = 1e-6
NEG = -1e30

LANES = 128
SUBLANES = 8
VMEM_LIMIT = 56 * 1024 * 1024

TOKEN_TILE = 512
ROUTE_TILE = 256
GLA_TILE = 256
PEER_TOKEN_TILE = 1024
PEER_EXPERT_TILE = 1024


def _params(*sem):
    return pltpu.CompilerParams(dimension_semantics=sem, vmem_limit_bytes=VMEM_LIMIT)


def _full(shape):
    return pl.BlockSpec(shape, lambda *_: (0,) * len(shape))


def _rms(x, g):
    return x * lax.rsqrt(jnp.mean(x * x, axis=-1, keepdims=True) + EPS) * g


def _seg_mean_sq(z, seg_avg):
    sq = z * z
    hi = sq.astype(BF16)
    lo = (sq - hi.astype(F32)).astype(BF16)
    return (jnp.dot(hi, seg_avg, preferred_element_type=F32)
            + jnp.dot(lo, seg_avg, preferred_element_type=F32))


def _seg_avg_matrix(width, seg):
    r = jnp.arange(width) // seg
    return jnp.where(r[:, None] == r[None, :], 1.0 / seg, 0.0).astype(BF16)


def _mixer_in_kernel(x_ref, gmix_ref, w_ref, gq_ref, gk_ref, avg_ref, wg2_ref, bg2_ref,
                     qa_ref, kaf_ref, kab_ref, vaf_ref, vab_ref, qg_ref, kg_ref, vg_ref, la_ref, sr_ref):
    h = _rms(x_ref[...], gmix_ref[...]).astype(BF16)
    z = jnp.dot(h, w_ref[...], preferred_element_type=F32)
    o = 0
    qa = z[:, o:o + A_WIDTH]; o += A_WIDTH
    ka = z[:, o:o + A_WIDTH]; o += A_WIDTH
    va = z[:, o:o + A_WIDTH]; o += A_WIDTH
    qg = z[:, o:o + G_KW]; o += G_KW
    kg = z[:, o:o + G_KW]; o += G_KW
    vg = z[:, o:o + G_WIDTH]; o += G_WIDTH
    r = z[:, o:o + G_WIDTH]; o += G_WIDTH
    gl = z[:, o:o + LANES]
    avg = avg_ref[...]
    qa = qa * lax.rsqrt(_seg_mean_sq(qa, avg) + EPS) * gq_ref[...]
    ka = ka * lax.rsqrt(_seg_mean_sq(ka, avg) + EPS) * gk_ref[...]
    qa_ref[...] = (qa * (A_HEAD_DIM ** -0.5)).astype(BF16)
    kaf_ref[...] = ka
    kab_ref[...] = ka.astype(BF16)
    vaf_ref[...] = va
    vab_ref[...] = va.astype(BF16)
    qg_ref[...] = qg * (G_DK ** -0.5)
    kg_ref[...] = kg
    vg_ref[...] = vg
    pre = jnp.dot(gl, wg2_ref[...], precision=HIGHEST, preferred_element_type=F32) + bg2_ref[...]
    la_ref[...] = (jnp.minimum(pre, 0.0) - jnp.log1p(jnp.exp(-jnp.abs(pre)))) / G_GATE_NORM
    sr_ref[...] = r * jax.nn.sigmoid(r)


def _mixer_in(x, g_mix, w_in, g_q, g_k, w_gate2, b_gate2):
    n = x.shape[0]
    c_gl = 3 * A_WIDTH + 2 * G_KW + G_WIDTH
    w_cat = jnp.concatenate(
        [w_in[:, :c_gl], w_in[:, c_gl + G_GATE_RANK:],
         jnp.pad(w_in[:, c_gl:c_gl + G_GATE_RANK], ((0, 0), (0, LANES - G_GATE_RANK)))], axis=1).astype(BF16)
    wg2 = jnp.pad(w_gate2, ((0, LANES - G_GATE_RANK), (0, 0)))
    wcols = w_cat.shape[1]
    tm = TOKEN_TILE
    row = lambda w: pl.BlockSpec((tm, w), lambda i: (i, 0))
    outs = [(A_WIDTH, BF16), (A_WIDTH, F32), (A_WIDTH, BF16), (A_WIDTH, F32), (A_WIDTH, BF16),
            (G_KW, F32), (G_KW, F32), (G_WIDTH, F32), (G_KW, F32), (G_WIDTH, F32)]
    return pl.pallas_call(
        _mixer_in_kernel,
        grid=(n // tm,),
        in_specs=[row(D_MODEL), _full((1, D_MODEL)), _full((D_MODEL, wcols)), _full((1, A_WIDTH)),
                  _full((1, A_WIDTH)), _full((A_WIDTH, A_WIDTH)), _full((LANES, G_KW)), _full((1, G_KW))],
        out_specs=[row(w) for w, _ in outs],
        out_shape=[jax.ShapeDtypeStruct((n, w), dt) for w, dt in outs],
        compiler_params=_params("parallel"),
        name="mixer_in",
    )(x, g_mix.reshape(1, -1), w_cat, jnp.tile(g_q, A_HEADS).reshape(1, -1), jnp.tile(g_k, A_HEADS).reshape(1, -1),
      _seg_avg_matrix(A_WIDTH, A_HEAD_DIM), wg2, b_gate2.reshape(1, -1))


def _attend(q, segments):
    tq = q.shape[0]
    lane = lax.broadcasted_iota(jnp.int32, (tq, LANES), 1)
    outs = []
    for pair in range(A_HEADS // 2):
        sl = slice(pair * LANES, (pair + 1) * LANES)
        q2 = q[:, sl]
        halves = []
        for half in range(2):
            head = 2 * pair + half
            mine = (lane < A_HEAD_DIM) if half == 0 else (lane >= A_HEAD_DIM)
            qm = jnp.where(mine, q2, jnp.zeros_like(q2))
            scores = [lax.dot_general(qm, k[:, sl], (((1,), (1,)), ((), ())), preferred_element_type=F32) + bias(head)
                      for k, _, bias in segments]
            m = functools.reduce(jnp.maximum, [jnp.max(s, axis=-1, keepdims=True) for s in scores])
            es = [jnp.exp(s - m) for s in scores]
            inv = 1.0 / functools.reduce(jnp.add, [jnp.sum(e, axis=-1, keepdims=True) for e in es])
            o = functools.reduce(jnp.add, [
                jnp.dot(e.astype(BF16), v[:, sl], preferred_element_type=F32)
                for e, (_, v, _) in zip(es, segments)])
            halves.append(o * inv)
        outs.append(jnp.where(lane < A_HEAD_DIM, halves[0], halves[1]))
    return jnp.concatenate(outs, axis=1)


ATT_GROUP = 4
ATT_ROWS = ATT_GROUP * CHUNK
ATT_WINDOW = (N_PAST_CHUNKS + ATT_GROUP) * CHUNK
ATT_CASES = -(-N_PAST_CHUNKS // ATT_GROUP) + 1
BIAS_WRAP = 1024


def _attn_prompt_kernel(q_ref, k_ref, v_ref, tb_ref, o_ref, bias_ref):
    g = pl.program_id(1)
    first = jnp.maximum(g * ATT_GROUP - N_PAST_CHUNKS, 0)
    start = pl.multiple_of(first * CHUNK, CHUNK)

    for case in range(ATT_CASES):
        @pl.when(g == case)
        def _(case=case):
            off = min(case * ATT_GROUP, N_PAST_CHUNKS) * CHUNK
            q_chunk = (off + lax.broadcasted_iota(jnp.int32, (ATT_ROWS, ATT_WINDOW), 0)) // CHUNK
            k_chunk = lax.broadcasted_iota(jnp.int32, (ATT_ROWS, ATT_WINDOW), 1) // CHUNK
            valid = (k_chunk <= q_chunk) & (k_chunk >= q_chunk - N_PAST_CHUNKS)
            for head in range(A_HEADS):
                table = jnp.broadcast_to(tb_ref[head:head + 1, :], (ATT_ROWS, BIAS_WRAP))
                rolled = pltpu.roll(table, off, 1, stride=1, stride_axis=0)
                bias_ref[head] = jnp.where(valid, rolled[:, :ATT_WINDOW], NEG)

    kb = k_ref[0, pl.ds(start, ATT_WINDOW), :]
    vb = v_ref[0, pl.ds(start, ATT_WINDOW), :]
    o_ref[...] = _attend(q_ref[...], [(kb, vb, lambda head: bias_ref[head])]).astype(o_ref.dtype)


def _wrapped_bias_table(rel_bias):
    u = jnp.arange(BIAS_WRAP)
    dist = jnp.where(u < CHUNK, -u, BIAS_WRAP - u)
    return rel_bias[:, jnp.clip(dist, -REL_CLIP, REL_CLIP) + REL_CLIP].astype(F32)


def _attn_prompt(qa, ka, va, rel_bias, batch, seq):
    ng = seq // ATT_ROWS
    assert seq % ATT_ROWS == 0 and ng >= ATT_CASES and seq >= ATT_WINDOW
    kv = pl.BlockSpec((1, seq, A_WIDTH), lambda b, g: (b, 0, 0))
    return pl.pallas_call(
        _attn_prompt_kernel,
        grid=(batch, ng),
        in_specs=[pl.BlockSpec((ATT_ROWS, A_WIDTH), lambda b, g: (b * ng + g, 0)), kv, kv,
                  _full((A_HEADS, BIAS_WRAP))],
        out_specs=pl.BlockSpec((ATT_ROWS, A_WIDTH), lambda b, g: (b * ng + g, 0)),
        out_shape=jax.ShapeDtypeStruct((batch * seq, A_WIDTH), BF16),
        scratch_shapes=[pltpu.VMEM((A_HEADS, ATT_ROWS, ATT_WINDOW), F32)],
        compiler_params=_params("arbitrary", "arbitrary"),
        name="attn_prompt",
    )(qa, ka.reshape(batch, seq, A_WIDTH), va.reshape(batch, seq, A_WIDTH), _wrapped_bias_table(rel_bias))


SAMPLE_STREAMS = 4


def _attn_sample_kernel(q_ref, kn_ref, vn_ref, kc_ref, vc_ref, tb_ref, o_ref, *, t, first_valid):
    r = kc_ref.shape[1]
    valid_c = lax.broadcasted_iota(jnp.int32, (t, r), 1) >= first_valid
    bias_c, bias_n = [], []
    for head in range(A_HEADS):
        table = jnp.broadcast_to(tb_ref[head:head + 1, :], (t, BIAS_WRAP))
        bias_c.append(jnp.where(valid_c, pltpu.roll(table, r % BIAS_WRAP, 1, stride=1, stride_axis=0)[:, :r], NEG))
        bias_n.append(pltpu.roll(table, 0, 1, stride=1, stride_axis=0)[:, :t])
    for stream in range(kc_ref.shape[0]):
        rows = slice(stream * t, (stream + 1) * t)
        segs = [(kc_ref[stream].astype(BF16), vc_ref[stream].astype(BF16), lambda head: bias_c[head]),
                (kn_ref[rows, :], vn_ref[rows, :], lambda head: bias_n[head])]
        o_ref[rows, :] = _attend(q_ref[rows, :], segs).astype(o_ref.dtype)


def _attn_sample(qa, ka, va, cache_k, cache_v, rel_bias, batch, t, past_len):
    r = cache_k.shape[1]
    q_chunk = past_len // CHUNK
    assert (past_len + t - 1) // CHUNK == q_chunk and r <= BAND
    first_valid = max(0, (q_chunk - N_PAST_CHUNKS) * CHUNK - (past_len - r))
    sb = math.gcd(batch, SAMPLE_STREAMS)
    tok = pl.BlockSpec((sb * t, A_WIDTH), lambda b: (b, 0))
    cache = pl.BlockSpec((sb, r, A_WIDTH), lambda b: (b, 0, 0))
    return pl.pallas_call(
        functools.partial(_attn_sample_kernel, t=t, first_valid=first_valid),
        grid=(batch // sb,),
        in_specs=[tok, tok, tok, cache, cache, _full((A_HEADS, BIAS_WRAP))],
        out_specs=tok,
        out_shape=jax.ShapeDtypeStruct((batch * t, A_WIDTH), BF16),
        compiler_params=_params("parallel"),
        name="attn_sample",
    )(qa, ka, va, cache_k.reshape(batch, r, A_WIDTH), cache_v.reshape(batch, r, A_WIDTH),
      _wrapped_bias_table(rel_bias))


def _gla_kernel(q_ref, k_ref, v_ref, la_ref, *rest):
    s0_ref = rest[0] if len(rest) == 7 else None
    tri_ref, expand_ref, bdmask_ref, o_ref, sfin_ref, state_ref = rest[-6:]
    tb = q_ref.shape[0]
    nb = tb // G_BLOCK

    @pl.when(pl.program_id(1) == 0)
    def _():
        state_ref[...] = jnp.zeros_like(state_ref) if s0_ref is None else s0_ref[0]

    q = q_ref[...]
    k = k_ref[...]
    v = v_ref[...]
    cum = jnp.dot(tri_ref[...], la_ref[...], precision=HIGHEST, preferred_element_type=F32)
    q3 = q.reshape(nb, G_BLOCK, G_KW)
    k3 = k.reshape(nb, G_BLOCK, G_KW)
    v3 = v.reshape(nb, G_BLOCK, G_WIDTH)
    cum3 = cum.reshape(nb, G_BLOCK, G_KW)
    row = lax.broadcasted_iota(jnp.int32, (nb, G_BLOCK, G_KW), 1)
    expand = expand_ref[...]

    intra = jnp.zeros((nb, G_BLOCK, G_WIDTH), F32)
    for s in range(G_BLOCK):
        k_s = jnp.broadcast_to(k3[:, s:s + 1, :], (nb, G_BLOCK, G_KW))
        c_s = jnp.broadcast_to(cum3[:, s:s + 1, :], (nb, G_BLOCK, G_KW))
        v_s = jnp.broadcast_to(v3[:, s:s + 1, :], (nb, G_BLOCK, G_WIDTH))
        prod = q3 * k_s * jnp.exp(jnp.where(row >= s, cum3 - c_s, -jnp.inf))
        att = jnp.dot(prod.reshape(tb, G_KW).astype(BF16), expand, preferred_element_type=F32)
        intra = intra + att.reshape(nb, G_BLOCK, G_WIDTH) * v_s
    o_ref[...] = intra.reshape(tb, G_WIDTH)

    last3 = jnp.broadcast_to(cum3[:, G_BLOCK - 1:G_BLOCK, :], (nb, G_BLOCK, G_KW))
    q_dec = (q3 * jnp.exp(cum3)).reshape(tb, G_KW).astype(BF16)
    k_dec = (k3 * jnp.exp(last3 - cum3)).reshape(tb, G_KW).astype(BF16)
    carry = jnp.exp(last3).reshape(tb, G_KW)
    vb = v.astype(BF16)
    bdmask = bdmask_ref[...]
    for j in range(nb):
        rows = slice(j * G_BLOCK, (j + 1) * G_BLOCK)
        state = state_ref[...]
        o_ref[rows, :] += lax.dot_general(q_dec[rows], state.astype(BF16), (((1,), (1,)), ((), ())),
                                          preferred_element_type=F32)
        upd = lax.dot_general(vb[rows], k_dec[rows], (((0,), (0,)), ((), ())), preferred_element_type=F32)
        state_ref[...] = carry[j * G_BLOCK:j * G_BLOCK + 1, :] * state + upd * bdmask

    @pl.when(pl.program_id(1) == pl.num_programs(1) - 1)
    def _():
        sfin_ref[0] = state_ref[...]


def _gla(qg, kg, vg, la, s0, batch, t, tile):
    nt = t // tile
    hsel = jnp.arange(G_KW) // G_DK
    vsel = jnp.arange(G_WIDTH) // G_DV
    same = hsel[:, None] == vsel[None, :]
    expand = same.astype(BF16)
    bdmask = same.T.astype(F32)
    state_in = []
    if s0 is not None:
        s0_t = jnp.swapaxes(s0, 2, 3).reshape(batch, G_WIDTH, G_DK)
        state_in = [jnp.where(same.T[None], jnp.tile(s0_t, (1, 1, G_HEADS)), 0.0)]
    rr = jnp.arange(tile)
    tri = ((rr[:, None] // G_BLOCK == rr[None, :] // G_BLOCK) & (rr[None, :] <= rr[:, None])).astype(F32)
    tok = lambda w: pl.BlockSpec((tile, w), lambda b, i: (b * nt + i, 0))
    st = pl.BlockSpec((1, G_WIDTH, G_KW), lambda b, i: (b, 0, 0))
    o, s_bd = pl.pallas_call(
        _gla_kernel,
        grid=(batch, nt),
        in_specs=[tok(G_KW), tok(G_KW), tok(G_WIDTH), tok(G_KW)] + [st] * len(state_in)
        + [_full((tile, tile)), _full((G_KW, G_WIDTH)), _full((G_WIDTH, G_KW))],
        out_specs=[tok(G_WIDTH), st],
        out_shape=[jax.ShapeDtypeStruct((batch * t, G_WIDTH), F32),
                   jax.ShapeDtypeStruct((batch, G_WIDTH, G_KW), F32)],
        scratch_shapes=[pltpu.VMEM((G_WIDTH, G_KW), F32)],
        compiler_params=_params("parallel", "arbitrary"),
        name="gla",
    )(qg, kg, vg, la, *state_in, tri, expand, bdmask)
    s_fin = jnp.stack([s_bd[:, h * G_DV:(h + 1) * G_DV, h * G_DK:(h + 1) * G_DK] for h in range(G_HEADS)], axis=1)
    return o, jnp.swapaxes(s_fin, 2, 3)


def _mixer_out_kernel(x_ref, oa_ref, og_ref, sr_ref, gg_ref, avg_ref, wa_ref, wg_ref, y_ref):
    og = og_ref[...]
    og = og * lax.rsqrt(_seg_mean_sq(og, avg_ref[...]) + EPS) * gg_ref[...] * sr_ref[...]
    y = jnp.dot(oa_ref[...], wa_ref[...], preferred_element_type=F32)
    y = y + jnp.dot(og.astype(BF16), wg_ref[...], preferred_element_type=F32)
    y_ref[...] = x_ref[...] + y


def _mixer_out(x, oa, og, sr, g_gla_out, w_out):
    n = x.shape[0]
    tm = TOKEN_TILE
    row = lambda w: pl.BlockSpec((tm, w), lambda i: (i, 0))
    w_out = w_out.astype(BF16)
    return pl.pallas_call(
        _mixer_out_kernel,
        grid=(n // tm,),
        in_specs=[row(D_MODEL), row(A_WIDTH), row(G_WIDTH), row(G_WIDTH), _full((1, G_WIDTH)),
                  _full((G_WIDTH, G_WIDTH)), _full((A_WIDTH, D_MODEL)), _full((G_WIDTH, D_MODEL))],
        out_specs=row(D_MODEL),
        out_shape=jax.ShapeDtypeStruct((n, D_MODEL), F32),
        compiler_params=_params("parallel"),
        name="mixer_out",
    )(x, oa, og, sr, jnp.tile(g_gla_out, G_HEADS).reshape(1, -1), _seg_avg_matrix(G_WIDTH, G_DV),
      w_out[:A_WIDTH], w_out[A_WIDTH:])


def _head_rms(z, g_ref, scale=1.0):
    outs = []
    for head in range(X_HEADS):
        sl = slice(head * X_HEAD_DIM, (head + 1) * X_HEAD_DIM)
        zh = z[:, sl]
        outs.append(zh * (lax.rsqrt(jnp.mean(zh * zh, axis=-1, keepdims=True) + EPS) * scale) * g_ref[:, sl])
    return outs


def _mem_kv_kernel(m_ref, gm_ref, wk_ref, wv_ref, gk_ref, k_ref, v_ref, kb_ref, vb_ref):
    m = _rms(m_ref[...], gm_ref[...]).astype(BF16)
    k = _head_rms(jnp.dot(m, wk_ref[...], preferred_element_type=F32), gk_ref)
    v = jnp.dot(m, wv_ref[...], preferred_element_type=F32)
    for head in range(X_HEADS):
        k_ref[:, head, :] = k[head]
        v_ref[:, head, :] = v[:, head * X_HEAD_DIM:(head + 1) * X_HEAD_DIM]
    kb_ref[...] = jnp.concatenate(k, axis=1).astype(BF16)
    vb_ref[...] = v.astype(BF16)


def _mem_kv(mem, g_mem, w_ck, w_cv, g_ck):
    batch = mem.shape[0]
    sq = _full((D_MODEL, D_MODEL))
    vec = _full((1, D_MODEL))
    out = pl.BlockSpec((None, N_MEM, X_HEADS, X_HEAD_DIM), lambda i: (i, 0, 0, 0))
    flat = pl.BlockSpec((None, N_MEM, D_MODEL), lambda i: (i, 0, 0))
    return pl.pallas_call(
        _mem_kv_kernel,
        grid=(batch,),
        in_specs=[flat, vec, sq, sq, vec],
        out_specs=[out, out, flat, flat],
        out_shape=[jax.ShapeDtypeStruct((batch, N_MEM, X_HEADS, X_HEAD_DIM), F32)] * 2
        + [jax.ShapeDtypeStruct((batch, N_MEM, D_MODEL), BF16)] * 2,
        compiler_params=_params("parallel"),
        name="mem_kv",
    )(mem, g_mem.reshape(1, -1), w_ck.astype(BF16), w_cv.astype(BF16), jnp.tile(g_ck, X_HEADS).reshape(1, -1))


def _mixer_out_cross_kernel(x_ref, oa_ref, og_ref, sr_ref, gg_ref, avg_ref, wa_ref, wg_ref,
                            gc_ref, wq_ref, gq_ref, mk_ref, mv_ref, wo_ref, y_ref):
    og = og_ref[...]
    og = og * lax.rsqrt(_seg_mean_sq(og, avg_ref[...]) + EPS) * gg_ref[...] * sr_ref[...]
    x = x_ref[...] + jnp.dot(oa_ref[...], wa_ref[...], preferred_element_type=F32)
    x = x + jnp.dot(og.astype(BF16), wg_ref[...], preferred_element_type=F32)
    h = _rms(x, gc_ref[...]).astype(BF16)
    q = _head_rms(jnp.dot(h, wq_ref[...], preferred_element_type=F32), gq_ref, X_HEAD_DIM ** -0.5)
    outs = []
    for head in range(X_HEADS):
        sl = slice(head * X_HEAD_DIM, (head + 1) * X_HEAD_DIM)
        s = lax.dot_general(q[head].astype(BF16), mk_ref[:, sl], (((1,), (1,)), ((), ())),
                            preferred_element_type=F32)
        e = jnp.exp(s - jnp.max(s, axis=-1, keepdims=True))
        p = (e * (1.0 / jnp.sum(e, axis=-1, keepdims=True))).astype(BF16)
        outs.append(jnp.dot(p, mv_ref[:, sl], preferred_element_type=F32))
    o = jnp.concatenate(outs, axis=1).astype(BF16)
    y_ref[...] = x + jnp.dot(o, wo_ref[...], preferred_element_type=F32)


def _cross_streams_kernel(x_ref, gc_ref, wq_ref, gq_ref, mk_ref, mv_ref, wo_ref, y_ref, q_scr, o_scr, *, t):
    b = pl.program_id(0)

    @pl.when(b == 0)
    def _():
        h = _rms(x_ref[...], gc_ref[...]).astype(BF16)
        q = _head_rms(jnp.dot(h, wq_ref[...], preferred_element_type=F32), gq_ref, X_HEAD_DIM ** -0.5)
        q_scr[...] = jnp.concatenate(q, axis=1).astype(BF16)

    rows = pl.ds(pl.multiple_of(b * t, t), t)
    q = q_scr[rows, :]
    outs = []
    for head in range(X_HEADS):
        sl = slice(head * X_HEAD_DIM, (head + 1) * X_HEAD_DIM)
        s = lax.dot_general(q[:, sl], mk_ref[0, :, sl].astype(BF16), (((1,), (1,)), ((), ())),
                            preferred_element_type=F32)
        e = jnp.exp(s - jnp.max(s, axis=-1, keepdims=True))
        p = (e * (1.0 / jnp.sum(e, axis=-1, keepdims=True))).astype(BF16)
        outs.append(jnp.dot(p, mv_ref[0, :, sl].astype(BF16), preferred_element_type=F32))
    o_scr[rows, :] = jnp.concatenate(outs, axis=1).astype(BF16)

    @pl.when(b == pl.num_programs(0) - 1)
    def _():
        y_ref[...] = x_ref[...] + jnp.dot(o_scr[...], wo_ref[...], preferred_element_type=F32)


def _cross_streams(x, mem_k, mem_v, g_cross, w_cq, g_cq, w_co, batch, t):
    n = batch * t
    whole = _full((n, D_MODEL))
    mem = pl.BlockSpec((1, N_MEM, D_MODEL), lambda b: (b, 0, 0))
    sq = _full((D_MODEL, D_MODEL))
    vec = _full((1, D_MODEL))
    return pl.pallas_call(
        functools.partial(_cross_streams_kernel, t=t),
        grid=(batch,),
        in_specs=[whole, vec, sq, vec, mem, mem, sq],
        out_specs=whole,
        out_shape=jax.ShapeDtypeStruct((n, D_MODEL), F32),
        scratch_shapes=[pltpu.VMEM((n, D_MODEL), BF16), pltpu.VMEM((n, D_MODEL), BF16)],
        compiler_params=_params("arbitrary"),
        name="cross_streams",
    )(x, g_cross.reshape(1, -1), w_cq.astype(BF16), jnp.tile(g_cq, X_HEADS).reshape(1, -1),
      mem_k, mem_v, w_co.astype(BF16))


def _mixer_out_cross(x, oa, og, sr, g_gla_out, w_out, mem_k, mem_v, g_cross, w_cq, g_cq, w_co, batch, t, tile):
    nt = t // tile
    row = lambda w: pl.BlockSpec((tile, w), lambda b, i: (b * nt + i, 0))
    mem = pl.BlockSpec((None, N_MEM, D_MODEL), lambda b, i: (b, 0, 0))
    sq = _full((D_MODEL, D_MODEL))
    vec = _full((1, D_MODEL))
    w_out = w_out.astype(BF16)
    return pl.pallas_call(
        _mixer_out_cross_kernel,
        grid=(batch, nt),
        in_specs=[row(D_MODEL), row(A_WIDTH), row(G_WIDTH), row(G_WIDTH), _full((1, G_WIDTH)),
                  _full((G_WIDTH, G_WIDTH)), _full((A_WIDTH, D_MODEL)), _full((G_WIDTH, D_MODEL)),
                  vec, sq, vec, mem, mem, sq],
        out_specs=row(D_MODEL),
        out_shape=jax.ShapeDtypeStruct((batch * t, D_MODEL), F32),
        compiler_params=_params("parallel", "parallel"),
        name="mixer_out_cross",
    )(x, oa, og, sr, jnp.tile(g_gla_out, G_HEADS).reshape(1, -1), _seg_avg_matrix(G_WIDTH, G_DV),
      w_out[:A_WIDTH], w_out[A_WIDTH:], g_cross.reshape(1, -1), w_cq.astype(BF16),
      jnp.tile(g_cq, X_HEADS).reshape(1, -1), mem_k, mem_v, w_co.astype(BF16))


CAND_ROWS = 64


def _sorting_network(n):
    pairs = []
    p = 1
    while p < n:
        k = p
        while k >= 1:
            for j in range(k % p, n - k, 2 * k):
                for i in range(min(k, n - j - k)):
                    if (i + j) // (2 * p) == (i + j + k) // (2 * p):
                        pairs.append((i + j, i + j + k))
            k //= 2
        p *= 2
    return pairs


def _top_values_tiled(s, count):
    nparts = s.shape[0] // SUBLANES
    parts = [s[SUBLANES * k:SUBLANES * (k + 1), :] for k in range(nparts)]
    for i, j in _sorting_network(nparts):
        parts[i], parts[j] = jnp.maximum(parts[i], parts[j]), jnp.minimum(parts[i], parts[j])
    which = lax.broadcasted_iota(jnp.int32, parts[0].shape, 0).astype(F32)
    exhausted = jnp.full(parts[0].shape, -jnp.inf, F32)
    out = []
    for r in range(count):
        mx = jnp.max(parts[0], axis=0, keepdims=True)
        first = jnp.min(jnp.where(parts[0] == mx, which, float(SUBLANES)), axis=0, keepdims=True)
        pop = which == first
        out.append(mx)
        for k in range(min(count - r - 1, nparts)):
            parts[k] = jnp.where(pop, parts[k + 1] if k + 1 < nparts else exhausted, parts[k])
    return jnp.concatenate(out, axis=0)


def _route_kernel(x_ref, gf_ref, whi_ref, wlo_ref, sk1_ref, sk2_ref, ht_ref, r2_ref, e2_ref, n1_ref, c1_ref):
    hf = _rms(x_ref[...], gf_ref[...])
    ht_ref[...] = jnp.transpose(hf).astype(BF16)
    h_hi = hf.astype(BF16)
    h_lo = (hf - h_hi.astype(F32)).astype(BF16)
    qry = (jnp.dot(h_hi, whi_ref[...], preferred_element_type=F32)
           + jnp.dot(h_hi, wlo_ref[...], preferred_element_type=F32)
           + jnp.dot(h_lo, whi_ref[...], preferred_element_type=F32))
    nt = (((1,), (1,)), ((), ()))
    nk = P_TOPK + 1
    t = qry.shape[0]
    for head in range(P_HEADS):
        q1 = qry[:, head * P_QDIM:head * P_QDIM + P_HALF]
        q2 = qry[:, head * P_QDIM + P_HALF:(head + 1) * P_QDIM]
        s1 = lax.dot_general(sk1_ref[...], q1, nt, precision=HIGHEST, preferred_element_type=F32)
        s2 = lax.dot_general(sk2_ref[...], q2, nt, precision=HIGHEST, preferred_element_type=F32)
        v1 = _top_values_tiled(s1, nk)
        v2 = _top_values_tiled(s2, nk)
        slabs = [v1[a:a + 1, :] + v2[:nk // (a + 1), :] for a in range(nk)]
        pad = CAND_ROWS - sum(sl.shape[0] for sl in slabs)
        cand = jnp.concatenate(slabs + [jnp.full((pad, t), -jnp.inf, F32)], axis=0)
        top = _top_values_tiled(cand, nk)
        m1 = v1[0:1, :]
        m2 = v2[0:1, :]
        z = jnp.sum(jnp.exp(top[:P_TOPK, :] - (m1 + m2)), axis=0, keepdims=True)
        tau = 0.5 * (top[P_TOPK - 1:P_TOPK, :] + top[P_TOPK:nk, :])
        thr = tau - s1
        n1 = jnp.zeros_like(s1)
        rank2 = jnp.zeros_like(s2)
        for b in range(P_TOPK):
            n1 = n1 + jnp.where(v2[b:b + 1, :] >= thr, 1.0, 0.0)
            rank2 = rank2 + jnp.where(v2[b:b + 1, :] > s2, 1.0, 0.0)
        r2_ref[head] = rank2.astype(BF16)
        e2_ref[head] = jnp.exp(s2 - m2).astype(BF16)
        n1_ref[head] = n1
        c1_ref[head] = jnp.exp(s1 - m1) * (1.0 / z)


def _route(x, g_ffn, w_pq, sub_keys1, sub_keys2):
    n = x.shape[0]
    w_hi = w_pq.astype(BF16)
    w_lo = (w_pq - w_hi.astype(F32)).astype(BF16)
    tm = ROUTE_TILE
    keys = pl.BlockSpec((P_HEADS, P_NKEYS, tm), lambda i: (0, 0, i))
    tab = lambda dt: jax.ShapeDtypeStruct((P_HEADS, P_NKEYS, n), dt)
    return pl.pallas_call(
        _route_kernel,
        grid=(n // tm,),
        in_specs=[pl.BlockSpec((tm, D_MODEL), lambda i: (i, 0)), _full((1, D_MODEL)),
                  _full((D_MODEL, P_HEADS * P_QDIM)), _full((D_MODEL, P_HEADS * P_QDIM)),
                  _full((P_NKEYS, P_HALF)), _full((P_NKEYS, P_HALF))],
        out_specs=[pl.BlockSpec((D_MODEL, tm), lambda i: (0, i)), keys, keys, keys, keys],
        out_shape=[jax.ShapeDtypeStruct((D_MODEL, n), BF16), tab(BF16), tab(BF16), tab(F32), tab(F32)],
        compiler_params=_params("parallel"),
        name="peer_route",
    )(x, g_ffn.reshape(1, -1), w_hi, w_lo, sub_keys1, sub_keys2)


def _peer_kernel(ht_ref, x_ref, u_ref, vt_ref, r2_ref, e2_ref, n1_ref, c1_ref, y_ref, acc_ref):
    j = pl.program_id(1)

    @pl.when(j == 0)
    def _():
        acc_ref[...] = jnp.zeros_like(acc_ref)

    t = ht_ref.shape[1]
    n_i1 = PEER_EXPERT_TILE // P_NKEYS
    zero = jnp.zeros((), BF16)
    a = jnp.dot(u_ref[...], ht_ref[...], preferred_element_type=F32)
    ab = a.astype(BF16)
    act = ab * (1.0 + lax.erf(ab * (1.0 / math.sqrt(2.0))))
    gates = []
    for r in range(n_i1):
        g = jnp.zeros((P_NKEYS, t), BF16)
        for head in range(P_HEADS):
            n1 = n1_ref[head, pl.ds(j * n_i1 + r, 1), :].astype(BF16)
            c1 = (0.5 * c1_ref[head, pl.ds(j * n_i1 + r, 1), :]).astype(BF16)
            g = g + jnp.where(r2_ref[head] < n1, e2_ref[head], zero) * c1
        gates.append(g)
    ga = act * jnp.concatenate(gates, axis=0)
    acc_ref[...] += jnp.dot(vt_ref[0], ga, preferred_element_type=F32)

    @pl.when(j == pl.num_programs(1) - 1)
    def _():
        y_ref[...] = x_ref[...] + jnp.transpose(acc_ref[...])


def _peer(ht, x, r2, e2, n1, c1, u, vt, tt):
    n = x.shape[0]
    et = PEER_EXPERT_TILE
    ne = u.shape[0] // et
    once = pl.Buffered(1)
    keys = pl.BlockSpec((P_HEADS, P_NKEYS, tt), lambda i, j: (0, 0, i))
    return pl.pallas_call(
        _peer_kernel,
        grid=(n // tt, ne),
        in_specs=[pl.BlockSpec((D_MODEL, tt), lambda i, j: (0, i)),
                  pl.BlockSpec((tt, D_MODEL), lambda i, j: (i, 0), pipeline_mode=once),
                  pl.BlockSpec((et, D_MODEL), lambda i, j: (j, 0)),
                  pl.BlockSpec((1, D_MODEL, et), lambda i, j: (j, 0, 0)),
                  keys, keys, keys, keys],
        out_specs=pl.BlockSpec((tt, D_MODEL), lambda i, j: (i, 0), pipeline_mode=once),
        out_shape=jax.ShapeDtypeStruct((n, D_MODEL), F32),
        scratch_shapes=[pltpu.VMEM((D_MODEL, tt), F32)],
        compiler_params=_params("parallel", "arbitrary"),
        name="peer_experts",
    )(ht, x, u, vt, r2, e2, n1, c1)


def _layer(xp, xs, cache_k, cache_v, state, cmem_k, cmem_v, mem_prompt, past_len,
           g_mix, w_in, g_q, g_k, rel_bias, w_gate2, b_gate2, g_gla_out, w_out,
           g_cross, g_mem, w_cq, w_ck, w_cv, g_cq, g_ck, w_co, g_ffn, w_pq, sub_keys1, sub_keys2, expert_u, expert_v):
    bp, sp, _ = xp.shape
    bs, ts, _ = xs.shape
    xp = xp.reshape(bp * sp, D_MODEL)
    xs = xs.reshape(bs * ts, D_MODEL)
    u = expert_u.astype(BF16)
    ne = expert_v.shape[0] // PEER_EXPERT_TILE
    vt = jnp.swapaxes(expert_v.reshape(ne, PEER_EXPERT_TILE, D_MODEL), 1, 2).astype(BF16)
    heads = lambda a, b, t: a.reshape(b, t, A_HEADS, A_HEAD_DIM)

    def peer(x, peer_tile):
        ht, r2, e2, n1, c1 = _route(x, g_ffn, w_pq, sub_keys1, sub_keys2)
        return _peer(ht, x, r2, e2, n1, c1, u, vt, peer_tile)

    qa, ka_f, ka_b, va_f, va_b, qg, kg, vg, la, sr = _mixer_in(xp, g_mix, w_in, g_q, g_k, w_gate2, b_gate2)
    oa = _attn_prompt(qa, ka_b, va_b, rel_bias, bp, sp)
    og, gla_p = _gla(qg, kg, vg, la, None, bp, sp, GLA_TILE)
    mk, mv, mk_b, mv_b = _mem_kv(mem_prompt, g_mem, w_ck, w_cv, g_ck)
    xp = peer(_mixer_out_cross(xp, oa, og, sr, g_gla_out, w_out, mk_b, mv_b, g_cross, w_cq, g_cq, w_co,
                               bp, sp, TOKEN_TILE), PEER_TOKEN_TILE)
    keep = min(BAND, sp)
    outs_p = (heads(ka_f, bp, sp)[:, -keep:], heads(va_f, bp, sp)[:, -keep:], gla_p, mk, mv)

    qa, ka_f, ka_b, va_f, va_b, qg, kg, vg, la, sr = _mixer_in(xs, g_mix, w_in, g_q, g_k, w_gate2, b_gate2)
    oa = _attn_sample(qa, ka_b, va_b, cache_k, cache_v, rel_bias, bs, ts, past_len)
    og, gla_s = _gla(qg, kg, vg, la, state, bs, ts, ts)
    xs = _mixer_out(xs, oa, og, sr, g_gla_out, w_out)
    xs = _cross_streams(xs, cmem_k.reshape(bs, N_MEM, D_MODEL), cmem_v.reshape(bs, N_MEM, D_MODEL),
                        g_cross, w_cq, g_cq, w_co, bs, ts)
    xs = peer(xs, bs * ts)
    outs_s = (heads(ka_f, bs, ts), heads(va_f, bs, ts), gla_s)

    return (xp.reshape(bp, sp, D_MODEL), xs.reshape(bs, ts, D_MODEL)) + outs_p + outs_s


def kernel(x_prompt, x_sample, cache_att_k, cache_att_v, state_gla, cache_mem_k, cache_mem_v, mem_prompt, g_mix, w_in, g_q, g_k, rel_bias, w_gate2, b_gate2, g_gla_out, w_out, g_cross, g_mem, w_cq, w_ck, w_cv, g_cq, g_ck, w_co, g_ffn, w_pq, sub_keys1, sub_keys2, expert_u, expert_v):
    depth = w_in.shape[0]
    past_len = 4096
    xp, xs = x_prompt, x_sample
    per_layer = []
    for l in range(depth):
        outs = _layer(xp, xs, cache_att_k[l], cache_att_v[l], state_gla[l], cache_mem_k[l], cache_mem_v[l],
                      mem_prompt, past_len,
                      g_mix[l], w_in[l], g_q[l], g_k[l], rel_bias[l], w_gate2[l], b_gate2[l], g_gla_out[l], w_out[l],
                      g_cross[l], g_mem[l], w_cq[l], w_ck[l], w_cv[l], g_cq[l], g_ck[l], w_co[l],
                      g_ffn[l], w_pq[l], sub_keys1[l], sub_keys2[l], expert_u[l], expert_v[l])
        xp, xs = outs[0], outs[1]
        per_layer.append(outs[2:])
    stacked = [jnp.stack([p[i] for p in per_layer]) for i in range(8)]
    return (xp, xs) + tuple(stacked)
```

```python
import functools
import math

import jax
import jax.numpy as jnp
from jax import lax
from jax.experimental import pallas as pl
from jax.experimental.pallas import tpu as pltpu

F32 = jnp.float32
BF16 = jnp.bfloat16
HIGHEST = lax.Precision.HIGHEST

D_MODEL = 1024
CHUNK = 64
N_PAST_CHUNKS = 8
BAND = (N_PAST_CHUNKS + 1) * CHUNK
A_HEADS = 8
A_HEAD_DIM = 64
A_WIDTH = A_HEADS * A_HEAD_DIM
REL_CLIP = 128
G_HEADS = 4
G_DK = 64
G_DV = 128
G_KW = G_HEADS * G_DK
G_WIDTH = G_HEADS * G_DV
G_GATE_RANK = 16
G_GATE_NORM = 16.0
G_BLOCK = 16
N_MEM = 256
X_HEADS = 4
X_HEAD_DIM = D_MODEL // X_HEADS
P_HEADS = 8
P_NKEYS = 128
P_QDIM = 256
P_HALF = P_QDIM // 2
P_TOPK = 16
EPS = 1e-6
NEG = -1e30
LOG2E = math.log2(math.e)

LANES = 128
SUBLANES = 8
VMEM_LIMIT = 56 * 1024 * 1024

TOKEN_TILE = 512
ROUTE_TILE = 256
GLA_TILE = 256
PEER_TOKEN_TILE = 1024
PEER_EXPERT_TILE = 1024


def _params(*sem):
    return pltpu.CompilerParams(dimension_semantics=sem, vmem_limit_bytes=VMEM_LIMIT)


def _full(shape):
    return pl.BlockSpec(shape, lambda *_: (0,) * len(shape))


def _rms(x, g):
    return x * lax.rsqrt(jnp.mean(x * x, axis=-1, keepdims=True) + EPS) * g


def _seg_mean_sq(z, seg_avg):
    sq = z * z
    hi = sq.astype(BF16)
    lo = (sq - hi.astype(F32)).astype(BF16)
    return (jnp.dot(hi, seg_avg, preferred_element_type=F32)
            + jnp.dot(lo, seg_avg, preferred_element_type=F32))


def _seg_avg_matrix(width, seg):
    r = jnp.arange(width) // seg
    return jnp.where(r[:, None] == r[None, :], 1.0 / seg, 0.0).astype(BF16)


def _mixer_in_kernel(x_ref, gmix_ref, w_ref, gq_ref, gk_ref, avg_ref, wg2_ref, bg2_ref,
                     qa_ref, kaf_ref, kab_ref, vaf_ref, vab_ref, qg_ref, kg_ref, vg_ref, la_ref, sr_ref):
    h = _rms(x_ref[...], gmix_ref[...]).astype(BF16)
    z = jnp.dot(h, w_ref[...], preferred_element_type=F32)
    o = 0
    qa = z[:, o:o + A_WIDTH]; o += A_WIDTH
    ka = z[:, o:o + A_WIDTH]; o += A_WIDTH
    va = z[:, o:o + A_WIDTH]; o += A_WIDTH
    qg = z[:, o:o + G_KW]; o += G_KW
    kg = z[:, o:o + G_KW]; o += G_KW
    vg = z[:, o:o + G_WIDTH]; o += G_WIDTH
    r = z[:, o:o + G_WIDTH]; o += G_WIDTH
    gl = z[:, o:o + LANES]
    avg = avg_ref[...]
    qa = qa * lax.rsqrt(_seg_mean_sq(qa, avg) + EPS) * gq_ref[...]
    ka = ka * lax.rsqrt(_seg_mean_sq(ka, avg) + EPS) * gk_ref[...]
    qa_ref[...] = (qa * (A_HEAD_DIM ** -0.5 * LOG2E)).astype(BF16)
    kaf_ref[...] = ka
    kab_ref[...] = ka.astype(BF16)
    vaf_ref[...] = va
    vab_ref[...] = va.astype(BF16)
    qg_ref[...] = qg * (G_DK ** -0.5)
    kg_ref[...] = kg
    vg_ref[...] = vg
    pre = jnp.dot(gl, wg2_ref[...], precision=HIGHEST, preferred_element_type=F32) + bg2_ref[...]
    la_ref[...] = (jnp.minimum(pre, 0.0) - jnp.log1p(jnp.exp(-jnp.abs(pre)))) / G_GATE_NORM
    sr_ref[...] = r * jax.nn.sigmoid(r)


def _mixer_in(x, g_mix, w_in, g_q, g_k, w_gate2, b_gate2):
    n = x.shape[0]
    c_gl = 3 * A_WIDTH + 2 * G_KW + G_WIDTH
    w_cat = jnp.concatenate(
        [w_in[:, :c_gl], w_in[:, c_gl + G_GATE_RANK:],
         jnp.pad(w_in[:, c_gl:c_gl + G_GATE_RANK], ((0, 0), (0, LANES - G_GATE_RANK)))], axis=1).astype(BF16)
    wg2 = jnp.pad(w_gate2, ((0, LANES - G_GATE_RANK), (0, 0)))
    wcols = w_cat.shape[1]
    tm = TOKEN_TILE
    row = lambda w: pl.BlockSpec((tm, w), lambda i: (i, 0))
    outs = [(A_WIDTH, BF16), (A_WIDTH, F32), (A_WIDTH, BF16), (A_WIDTH, F32), (A_WIDTH, BF16),
            (G_KW, F32), (G_KW, F32), (G_WIDTH, F32), (G_KW, F32), (G_WIDTH, F32)]
    return pl.pallas_call(
        _mixer_in_kernel,
        grid=(n // tm,),
        in_specs=[row(D_MODEL), _full((1, D_MODEL)), _full((D_MODEL, wcols)), _full((1, A_WIDTH)),
                  _full((1, A_WIDTH)), _full((A_WIDTH, A_WIDTH)), _full((LANES, G_KW)), _full((1, G_KW))],
        out_specs=[row(w) for w, _ in outs],
        out_shape=[jax.ShapeDtypeStruct((n, w), dt) for w, dt in outs],
        compiler_params=_params("parallel"),
        name="mixer_in",
    )(x, g_mix.reshape(1, -1), w_cat, jnp.tile(g_q, A_HEADS).reshape(1, -1), jnp.tile(g_k, A_HEADS).reshape(1, -1),
      _seg_avg_matrix(A_WIDTH, A_HEAD_DIM), wg2, b_gate2.reshape(1, -1))


def _attend(q, segments):
    tq = q.shape[0]
    lane = lax.broadcasted_iota(jnp.int32, (tq, LANES), 1)
    outs = []
    for pair in range(A_HEADS // 2):
        sl = slice(pair * LANES, (pair + 1) * LANES)
        q2 = q[:, sl]
        halves = []
        for half in range(2):
            head = 2 * pair + half
            mine = (lane < A_HEAD_DIM) if half == 0 else (lane >= A_HEAD_DIM)
            qm = jnp.where(mine, q2, jnp.zeros_like(q2))
            scores = [lax.dot_general(qm, k[:, sl], (((1,), (1,)), ((), ())), preferred_element_type=F32) + bias(head)
                      for k, _, bias in segments]
            m = functools.reduce(jnp.maximum, [jnp.max(s, axis=-1, keepdims=True) for s in scores])
            es = [jnp.exp2(s - m) for s in scores]
            inv = 1.0 / functools.reduce(jnp.add, [jnp.sum(e, axis=-1, keepdims=True) for e in es])
            o = functools.reduce(jnp.add, [
                jnp.dot(e.astype(BF16), v[:, sl], preferred_element_type=F32)
                for e, (_, v, _) in zip(es, segments)])
            halves.append(o * inv)
        outs.append(jnp.where(lane < A_HEAD_DIM, halves[0], halves[1]))
    return jnp.concatenate(outs, axis=1)


ATT_GROUP = 4
ATT_ROWS = ATT_GROUP * CHUNK
ATT_WINDOW = (N_PAST_CHUNKS + ATT_GROUP) * CHUNK
ATT_CASES = -(-N_PAST_CHUNKS // ATT_GROUP) + 1
BIAS_WRAP = 1024


def _attn_prompt_kernel(q_ref, k_ref, v_ref, tb_ref, o_ref, bias_ref):
    g = pl.program_id(1)
    first = jnp.maximum(g * ATT_GROUP - N_PAST_CHUNKS, 0)
    start = pl.multiple_of(first * CHUNK, CHUNK)

    for case in range(ATT_CASES):
        @pl.when(g == case)
        def _(case=case):
            off = min(case * ATT_GROUP, N_PAST_CHUNKS) * CHUNK
            q_chunk = (off + lax.broadcasted_iota(jnp.int32, (ATT_ROWS, ATT_WINDOW), 0)) // CHUNK
            k_chunk = lax.broadcasted_iota(jnp.int32, (ATT_ROWS, ATT_WINDOW), 1) // CHUNK
            valid = (k_chunk <= q_chunk) & (k_chunk >= q_chunk - N_PAST_CHUNKS)
            for head in range(A_HEADS):
                table = jnp.broadcast_to(tb_ref[head:head + 1, :], (ATT_ROWS, BIAS_WRAP))
                rolled = pltpu.roll(table, off, 1, stride=1, stride_axis=0)
                bias_ref[head] = jnp.where(valid, rolled[:, :ATT_WINDOW], NEG)

    kb = k_ref[0, pl.ds(start, ATT_WINDOW), :]
    vb = v_ref[0, pl.ds(start, ATT_WINDOW), :]
    o_ref[...] = _attend(q_ref[...], [(kb, vb, lambda head: bias_ref[head])]).astype(o_ref.dtype)


def _wrapped_bias_table(rel_bias):
    u = jnp.arange(BIAS_WRAP)
    dist = jnp.where(u < CHUNK, -u, BIAS_WRAP - u)
    return rel_bias[:, jnp.clip(dist, -REL_CLIP, REL_CLIP) + REL_CLIP].astype(F32) * LOG2E


def _attn_prompt(qa, ka, va, rel_bias, batch, seq):
    ng = seq // ATT_ROWS
    assert seq % ATT_ROWS == 0 and ng >= ATT_CASES and seq >= ATT_WINDOW
    kv = pl.BlockSpec((1, seq, A_WIDTH), lambda b, g: (b, 0, 0))
    return pl.pallas_call(
        _attn_prompt_kernel,
        grid=(batch, ng),
        in_specs=[pl.BlockSpec((ATT_ROWS, A_WIDTH), lambda b, g: (b * ng + g, 0)), kv, kv,
                  _full((A_HEADS, BIAS_WRAP))],
        out_specs=pl.BlockSpec((ATT_ROWS, A_WIDTH), lambda b, g: (b * ng + g, 0)),
        out_shape=jax.ShapeDtypeStruct((batch * seq, A_WIDTH), BF16),
        scratch_shapes=[pltpu.VMEM((A_HEADS, ATT_ROWS, ATT_WINDOW), F32)],
        compiler_params=_params("arbitrary", "arbitrary"),
        name="attn_prompt",
    )(qa, ka.reshape(batch, seq, A_WIDTH), va.reshape(batch, seq, A_WIDTH), _wrapped_bias_table(rel_bias))


SAMPLE_STREAMS = 4


def _attn_sample_kernel(q_ref, kn_ref, vn_ref, kc_ref, vc_ref, tb_ref, o_ref, *, t, first_valid):
    r = kc_ref.shape[1]
    valid_c = lax.broadcasted_iota(jnp.int32, (t, r), 1) >= first_valid
    bias_c, bias_n = [], []
    for head in range(A_HEADS):
        table = jnp.broadcast_to(tb_ref[head:head + 1, :], (t, BIAS_WRAP))
        bias_c.append(jnp.where(valid_c, pltpu.roll(table, r % BIAS_WRAP, 1, stride=1, stride_axis=0)[:, :r], NEG))
        bias_n.append(pltpu.roll(table, 0, 1, stride=1, stride_axis=0)[:, :t])
    for stream in range(kc_ref.shape[0]):
        rows = slice(stream * t, (stream + 1) * t)
        segs = [(kc_ref[stream].astype(BF16), vc_ref[stream].astype(BF16), lambda head: bias_c[head]),
                (kn_ref[rows, :], vn_ref[rows, :], lambda head: bias_n[head])]
        o_ref[rows, :] = _attend(q_ref[rows, :], segs).astype(o_ref.dtype)


def _attn_sample(qa, ka, va, cache_k, cache_v, rel_bias, batch, t, past_len):
    r = cache_k.shape[1]
    q_chunk = past_len // CHUNK
    assert (past_len + t - 1) // CHUNK == q_chunk and r <= BAND
    first_valid = max(0, (q_chunk - N_PAST_CHUNKS) * CHUNK - (past_len - r))
    sb = math.gcd(batch, SAMPLE_STREAMS)
    tok = pl.BlockSpec((sb * t, A_WIDTH), lambda b: (b, 0))
    cache = pl.BlockSpec((sb, r, A_WIDTH), lambda b: (b, 0, 0))
    return pl.pallas_call(
        functools.partial(_attn_sample_kernel, t=t, first_valid=first_valid),
        grid=(batch // sb,),
        in_specs=[tok, tok, tok, cache, cache, _full((A_HEADS, BIAS_WRAP))],
        out_specs=tok,
        out_shape=jax.ShapeDtypeStruct((batch * t, A_WIDTH), BF16),
        compiler_params=_params("parallel"),
        name="attn_sample",
    )(qa, ka, va, cache_k.reshape(batch, r, A_WIDTH), cache_v.reshape(batch, r, A_WIDTH),
      _wrapped_bias_table(rel_bias))


def _gla_kernel(q_ref, k_ref, v_ref, la_ref, *rest):
    s0_ref = rest[0] if len(rest) == 7 else None
    tri_ref, expand_ref, bdmask_ref, o_ref, sfin_ref, state_ref = rest[-6:]
    tb = q_ref.shape[0]
    nb = tb // G_BLOCK

    @pl.when(pl.program_id(1) == 0)
    def _():
        state_ref[...] = jnp.zeros_like(state_ref) if s0_ref is None else s0_ref[0]

    q = q_ref[...]
    k = k_ref[...]
    v = v_ref[...]
    cum = jnp.dot(tri_ref[...], la_ref[...], precision=HIGHEST, preferred_element_type=F32)
    q3 = q.reshape(nb, G_BLOCK, G_KW)
    k3 = k.reshape(nb, G_BLOCK, G_KW)
    v3 = v.reshape(nb, G_BLOCK, G_WIDTH)
    cum3 = cum.reshape(nb, G_BLOCK, G_KW)
    row = lax.broadcasted_iota(jnp.int32, (nb, G_BLOCK, G_KW), 1)
    expand = expand_ref[...]

    intra = jnp.zeros((nb, G_BLOCK, G_WIDTH), F32)
    for s in range(G_BLOCK):
        k_s = jnp.broadcast_to(k3[:, s:s + 1, :], (nb, G_BLOCK, G_KW))
        c_s = jnp.broadcast_to(cum3[:, s:s + 1, :], (nb, G_BLOCK, G_KW))
        v_s = jnp.broadcast_to(v3[:, s:s + 1, :], (nb, G_BLOCK, G_WIDTH))
        prod = q3 * k_s * jnp.exp(jnp.where(row >= s, cum3 - c_s, -jnp.inf))
        att = jnp.dot(prod.reshape(tb, G_KW).astype(BF16), expand, preferred_element_type=F32)
        intra = intra + att.reshape(nb, G_BLOCK, G_WIDTH) * v_s
    o_ref[...] = intra.reshape(tb, G_WIDTH)

    last3 = jnp.broadcast_to(cum3[:, G_BLOCK - 1:G_BLOCK, :], (nb, G_BLOCK, G_KW))
    q_dec = (q3 * jnp.exp(cum3)).reshape(tb, G_KW).astype(BF16)
    k_dec = (k3 * jnp.exp(last3 - cum3)).reshape(tb, G_KW).astype(BF16)
    carry = jnp.exp(last3).reshape(tb, G_KW)
    vb = v.astype(BF16)
    bdmask = bdmask_ref[...]
    for j in range(nb):
        rows = slice(j * G_BLOCK, (j + 1) * G_BLOCK)
        state = state_ref[...]
        o_ref[rows, :] += lax.dot_general(q_dec[rows], state.astype(BF16), (((1,), (1,)), ((), ())),
                                          preferred_element_type=F32)
        upd = lax.dot_general(vb[rows], k_dec[rows], (((0,), (0,)), ((), ())), preferred_element_type=F32)
        state_ref[...] = carry[j * G_BLOCK:j * G_BLOCK + 1, :] * state + upd * bdmask

    @pl.when(pl.program_id(1) == pl.num_programs(1) - 1)
    def _():
        sfin_ref[0] = state_ref[...]


def _gla(qg, kg, vg, la, s0, batch, t, tile):
    nt = t // tile
    hsel = jnp.arange(G_KW) // G_DK
    vsel = jnp.arange(G_WIDTH) // G_DV
    same = hsel[:, None] == vsel[None, :]
    expand = same.astype(BF16)
    bdmask = same.T.astype(F32)
    state_in = []
    if s0 is not None:
        s0_t = jnp.swapaxes(s0, 2, 3).reshape(batch, G_WIDTH, G_DK)
        state_in = [jnp.where(same.T[None], jnp.tile(s0_t, (1, 1, G_HEADS)), 0.0)]
    rr = jnp.arange(tile)
    tri = ((rr[:, None] // G_BLOCK == rr[None, :] // G_BLOCK) & (rr[None, :] <= rr[:, None])).astype(F32)
    tok = lambda w: pl.BlockSpec((tile, w), lambda b, i: (b * nt + i, 0))
    st = pl.BlockSpec((1, G_WIDTH, G_KW), lambda b, i: (b, 0, 0))
    o, s_bd = pl.pallas_call(
        _gla_kernel,
        grid=(batch, nt),
        in_specs=[tok(G_KW), tok(G_KW), tok(G_WIDTH), tok(G_KW)] + [st] * len(state_in)
        + [_full((tile, tile)), _full((G_KW, G_WIDTH)), _full((G_WIDTH, G_KW))],
        out_specs=[tok(G_WIDTH), st],
        out_shape=[jax.ShapeDtypeStruct((batch * t, G_WIDTH), F32),
                   jax.ShapeDtypeStruct((batch, G_WIDTH, G_KW), F32)],
        scratch_shapes=[pltpu.VMEM((G_WIDTH, G_KW), F32)],
        compiler_params=_params("parallel", "arbitrary"),
        name="gla",
    )(qg, kg, vg, la, *state_in, tri, expand, bdmask)
    s_fin = jnp.stack([s_bd[:, h * G_DV:(h + 1) * G_DV, h * G_DK:(h + 1) * G_DK] for h in range(G_HEADS)], axis=1)
    return o, jnp.swapaxes(s_fin, 2, 3)


def _mixer_out_kernel(x_ref, oa_ref, og_ref, sr_ref, gg_ref, avg_ref, wa_ref, wg_ref, y_ref):
    og = og_ref[...]
    og = og * lax.rsqrt(_seg_mean_sq(og, avg_ref[...]) + EPS) * gg_ref[...] * sr_ref[...]
    y = jnp.dot(oa_ref[...], wa_ref[...], preferred_element_type=F32)
    y = y + jnp.dot(og.astype(BF16), wg_ref[...], preferred_element_type=F32)
    y_ref[...] = x_ref[...] + y


def _mixer_out(x, oa, og, sr, g_gla_out, w_out):
    n = x.shape[0]
    tm = TOKEN_TILE
    row = lambda w: pl.BlockSpec((tm, w), lambda i: (i, 0))
    w_out = w_out.astype(BF16)
    return pl.pallas_call(
        _mixer_out_kernel,
        grid=(n // tm,),
        in_specs=[row(D_MODEL), row(A_WIDTH), row(G_WIDTH), row(G_WIDTH), _full((1, G_WIDTH)),
                  _full((G_WIDTH, G_WIDTH)), _full((A_WIDTH, D_MODEL)), _full((G_WIDTH, D_MODEL))],
        out_specs=row(D_MODEL),
        out_shape=jax.ShapeDtypeStruct((n, D_MODEL), F32),
        compiler_params=_params("parallel"),
        name="mixer_out",
    )(x, oa, og, sr, jnp.tile(g_gla_out, G_HEADS).reshape(1, -1), _seg_avg_matrix(G_WIDTH, G_DV),
      w_out[:A_WIDTH], w_out[A_WIDTH:])


def _head_rms(z, g_ref, scale=1.0):
    outs = []
    for head in range(X_HEADS):
        sl = slice(head * X_HEAD_DIM, (head + 1) * X_HEAD_DIM)
        zh = z[:, sl]
        outs.append(zh * (lax.rsqrt(jnp.mean(zh * zh, axis=-1, keepdims=True) + EPS) * scale) * g_ref[:, sl])
    return outs


def _mem_kv_kernel(m_ref, gm_ref, wk_ref, wv_ref, gk_ref, k_ref, v_ref, kb_ref, vb_ref):
    m = _rms(m_ref[...], gm_ref[...]).astype(BF16)
    k = _head_rms(jnp.dot(m, wk_ref[...], preferred_element_type=F32), gk_ref)
    v = jnp.dot(m, wv_ref[...], preferred_element_type=F32)
    for head in range(X_HEADS):
        k_ref[:, head, :] = k[head]
        v_ref[:, head, :] = v[:, head * X_HEAD_DIM:(head + 1) * X_HEAD_DIM]
    kb_ref[...] = jnp.concatenate(k, axis=1).astype(BF16)
    vb_ref[...] = v.astype(BF16)


def _mem_kv(mem, g_mem, w_ck, w_cv, g_ck):
    batch = mem.shape[0]
    sq = _full((D_MODEL, D_MODEL))
    vec = _full((1, D_MODEL))
    out = pl.BlockSpec((None, N_MEM, X_HEADS, X_HEAD_DIM), lambda i: (i, 0, 0, 0))
    flat = pl.BlockSpec((None, N_MEM, D_MODEL), lambda i: (i, 0, 0))
    return pl.pallas_call(
        _mem_kv_kernel,
        grid=(batch,),
        in_specs=[flat, vec, sq, sq, vec],
        out_specs=[out, out, flat, flat],
        out_shape=[jax.ShapeDtypeStruct((batch, N_MEM, X_HEADS, X_HEAD_DIM), F32)] * 2
        + [jax.ShapeDtypeStruct((batch, N_MEM, D_MODEL), BF16)] * 2,
        compiler_params=_params("parallel"),
        name="mem_kv",
    )(mem, g_mem.reshape(1, -1), w_ck.astype(BF16), w_cv.astype(BF16), jnp.tile(g_ck, X_HEADS).reshape(1, -1))


def _mixer_out_cross_kernel(x_ref, oa_ref, og_ref, sr_ref, gg_ref, avg_ref, wa_ref, wg_ref,
                            gc_ref, wq_ref, gq_ref, mk_ref, mv_ref, wo_ref, y_ref):
    og = og_ref[...]
    og = og * lax.rsqrt(_seg_mean_sq(og, avg_ref[...]) + EPS) * gg_ref[...] * sr_ref[...]
    x = x_ref[...] + jnp.dot(oa_ref[...], wa_ref[...], preferred_element_type=F32)
    x = x + jnp.dot(og.astype(BF16), wg_ref[...], preferred_element_type=F32)
    h = _rms(x, gc_ref[...]).astype(BF16)
    q = _head_rms(jnp.dot(h, wq_ref[...], preferred_element_type=F32), gq_ref, X_HEAD_DIM ** -0.5)
    outs = []
    for head in range(X_HEADS):
        sl = slice(head * X_HEAD_DIM, (head + 1) * X_HEAD_DIM)
        s = lax.dot_general(q[head].astype(BF16), mk_ref[:, sl], (((1,), (1,)), ((), ())),
                            preferred_element_type=F32)
        e = jnp.exp(s - jnp.max(s, axis=-1, keepdims=True))
        p = (e * (1.0 / jnp.sum(e, axis=-1, keepdims=True))).astype(BF16)
        outs.append(jnp.dot(p, mv_ref[:, sl], preferred_element_type=F32))
    o = jnp.concatenate(outs, axis=1).astype(BF16)
    y_ref[...] = x + jnp.dot(o, wo_ref[...], preferred_element_type=F32)


def _cross_streams_kernel(x_ref, gc_ref, wq_ref, gq_ref, mk_ref, mv_ref, wo_ref, y_ref, q_scr, o_scr, *, t):
    b = pl.program_id(0)

    @pl.when(b == 0)
    def _():
        h = _rms(x_ref[...], gc_ref[...]).astype(BF16)
        q = _head_rms(jnp.dot(h, wq_ref[...], preferred_element_type=F32), gq_ref, X_HEAD_DIM ** -0.5)
        q_scr[...] = jnp.concatenate(q, axis=1).astype(BF16)

    rows = pl.ds(pl.multiple_of(b * t, t), t)
    q = q_scr[rows, :]
    outs = []
    for head in range(X_HEADS):
        sl = slice(head * X_HEAD_DIM, (head + 1) * X_HEAD_DIM)
        s = lax.dot_general(q[:, sl], mk_ref[0, :, sl].astype(BF16), (((1,), (1,)), ((), ())),
                            preferred_element_type=F32)
        e = jnp.exp(s - jnp.max(s, axis=-1, keepdims=True))
        p = (e * (1.0 / jnp.sum(e, axis=-1, keepdims=True))).astype(BF16)
        outs.append(jnp.dot(p, mv_ref[0, :, sl].astype(BF16), preferred_element_type=F32))
    o_scr[rows, :] = jnp.concatenate(outs, axis=1).astype(BF16)

    @pl.when(b == pl.num_programs(0) - 1)
    def _():
        y_ref[...] = x_ref[...] + jnp.dot(o_scr[...], wo_ref[...], preferred_element_type=F32)


def _cross_streams(x, mem_k, mem_v, g_cross, w_cq, g_cq, w_co, batch, t):
    n = batch * t
    whole = _full((n, D_MODEL))
    mem = pl.BlockSpec((1, N_MEM, D_MODEL), lambda b: (b, 0, 0))
    sq = _full((D_MODEL, D_MODEL))
    vec = _full((1, D_MODEL))
    return pl.pallas_call(
        functools.partial(_cross_streams_kernel, t=t),
        grid=(batch,),
        in_specs=[whole, vec, sq, vec, mem, mem, sq],
        out_specs=whole,
        out_shape=jax.ShapeDtypeStruct((n, D_MODEL), F32),
        scratch_shapes=[pltpu.VMEM((n, D_MODEL), BF16), pltpu.VMEM((n, D_MODEL), BF16)],
        compiler_params=_params("arbitrary"),
        name="cross_streams",
    )(x, g_cross.reshape(1, -1), w_cq.astype(BF16), jnp.tile(g_cq, X_HEADS).reshape(1, -1),
      mem_k, mem_v, w_co.astype(BF16))


def _mixer_out_cross(x, oa, og, sr, g_gla_out, w_out, mem_k, mem_v, g_cross, w_cq, g_cq, w_co, batch, t, tile):
    nt = t // tile
    row = lambda w: pl.BlockSpec((tile, w), lambda b, i: (b * nt + i, 0))
    mem = pl.BlockSpec((None, N_MEM, D_MODEL), lambda b, i: (b, 0, 0))
    sq = _full((D_MODEL, D_MODEL))
    vec = _full((1, D_MODEL))
    w_out = w_out.astype(BF16)
    return pl.pallas_call(
        _mixer_out_cross_kernel,
        grid=(batch, nt),
        in_specs=[row(D_MODEL), row(A_WIDTH), row(G_WIDTH), row(G_WIDTH), _full((1, G_WIDTH)),
                  _full((G_WIDTH, G_WIDTH)), _full((A_WIDTH, D_MODEL)), _full((G_WIDTH, D_MODEL)),
                  vec, sq, vec, mem, mem, sq],
        out_specs=row(D_MODEL),
        out_shape=jax.ShapeDtypeStruct((batch * t, D_MODEL), F32),
        compiler_params=_params("parallel", "parallel"),
        name="mixer_out_cross",
    )(x, oa, og, sr, jnp.tile(g_gla_out, G_HEADS).reshape(1, -1), _seg_avg_matrix(G_WIDTH, G_DV),
      w_out[:A_WIDTH], w_out[A_WIDTH:], g_cross.reshape(1, -1), w_cq.astype(BF16),
      jnp.tile(g_cq, X_HEADS).reshape(1, -1), mem_k, mem_v, w_co.astype(BF16))


CAND_ROWS = 64


def _sorting_network(n):
    pairs = []
    p = 1
    while p < n:
        k = p
        while k >= 1:
            for j in range(k % p, n - k, 2 * k):
                for i in range(min(k, n - j - k)):
                    if (i + j) // (2 * p) == (i + j + k) // (2 * p):
                        pairs.append((i + j, i + j + k))
            k //= 2
        p *= 2
    return pairs


def _top_values_tiled(s, count):
    nparts = s.shape[0] // SUBLANES
    parts = [s[SUBLANES * k:SUBLANES * (k + 1), :] for k in range(nparts)]
    for i, j in _sorting_network(nparts):
        parts[i], parts[j] = jnp.maximum(parts[i], parts[j]), jnp.minimum(parts[i], parts[j])
    which = lax.broadcasted_iota(jnp.int32, parts[0].shape, 0).astype(F32)
    exhausted = jnp.full(parts[0].shape, -jnp.inf, F32)
    out = []
    for r in range(count):
        mx = jnp.max(parts[0], axis=0, keepdims=True)
        first = jnp.min(jnp.where(parts[0] == mx, which, float(SUBLANES)), axis=0, keepdims=True)
        pop = which == first
        out.append(mx)
        for k in range(min(count - r - 1, nparts)):
            parts[k] = jnp.where(pop, parts[k + 1] if k + 1 < nparts else exhausted, parts[k])
    return jnp.concatenate(out, axis=0)


def _route_kernel(x_ref, gf_ref, whi_ref, wlo_ref, sk1_ref, sk2_ref, ht_ref, r2_ref, e2_ref, n1_ref, c1_ref):
    hf = _rms(x_ref[...], gf_ref[...])
    ht_ref[...] = jnp.transpose(hf).astype(BF16)
    h_hi = hf.astype(BF16)
    h_lo = (hf - h_hi.astype(F32)).astype(BF16)
    qry = (jnp.dot(h_hi, whi_ref[...], preferred_element_type=F32)
           + jnp.dot(h_hi, wlo_ref[...], preferred_element_type=F32)
           + jnp.dot(h_lo, whi_ref[...], preferred_element_type=F32))
    nt = (((1,), (1,)), ((), ()))
    nk = P_TOPK + 1
    t = qry.shape[0]
    for head in range(P_HEADS):
        q1 = qry[:, head * P_QDIM:head * P_QDIM + P_HALF]
        q2 = qry[:, head * P_QDIM + P_HALF:(head + 1) * P_QDIM]
        s1 = lax.dot_general(sk1_ref[...], q1, nt, precision=HIGHEST, preferred_element_type=F32)
        s2 = lax.dot_general(sk2_ref[...], q2, nt, precision=HIGHEST, preferred_element_type=F32)
        v1 = _top_values_tiled(s1, nk)
        v2 = _top_values_tiled(s2, nk)
        slabs = [v1[a:a + 1, :] + v2[:nk // (a + 1), :] for a in range(nk)]
        pad = CAND_ROWS - sum(sl.shape[0] for sl in slabs)
        cand = jnp.concatenate(slabs + [jnp.full((pad, t), -jnp.inf, F32)], axis=0)
        top = _top_values_tiled(cand, nk)
        m1 = v1[0:1, :]
        m2 = v2[0:1, :]
        z = jnp.sum(jnp.exp(top[:P_TOPK, :] - (m1 + m2)), axis=0, keepdims=True)
        tau = 0.5 * (top[P_TOPK - 1:P_TOPK, :] + top[P_TOPK:nk, :])
        thr = tau - s1
        n1 = jnp.zeros_like(s1)
        rank2 = jnp.zeros_like(s2)
        for b in range(P_TOPK):
            n1 = n1 + jnp.where(v2[b:b + 1, :] >= thr, 1.0, 0.0)
            rank2 = rank2 + jnp.where(v2[b:b + 1, :] > s2, 1.0, 0.0)
        r2_ref[head] = rank2.astype(BF16)
        e2_ref[head] = jnp.exp(s2 - m2).astype(BF16)
        n1_ref[head] = n1
        c1_ref[head] = jnp.exp(s1 - m1) * (1.0 / z)


def _route(x, g_ffn, w_pq, sub_keys1, sub_keys2):
    n = x.shape[0]
    w_hi = w_pq.astype(BF16)
    w_lo = (w_pq - w_hi.astype(F32)).astype(BF16)
    tm = ROUTE_TILE
    keys = pl.BlockSpec((P_HEADS, P_NKEYS, tm), lambda i: (0, 0, i))
    tab = lambda dt: jax.ShapeDtypeStruct((P_HEADS, P_NKEYS, n), dt)
    return pl.pallas_call(
        _route_kernel,
        grid=(n // tm,),
        in_specs=[pl.BlockSpec((tm, D_MODEL), lambda i: (i, 0)), _full((1, D_MODEL)),
                  _full((D_MODEL, P_HEADS * P_QDIM)), _full((D_MODEL, P_HEADS * P_QDIM)),
                  _full((P_NKEYS, P_HALF)), _full((P_NKEYS, P_HALF))],
        out_specs=[pl.BlockSpec((D_MODEL, tm), lambda i: (0, i)), keys, keys, keys, keys],
        out_shape=[jax.ShapeDtypeStruct((D_MODEL, n), BF16), tab(BF16), tab(BF16), tab(F32), tab(F32)],
        compiler_params=_params("parallel"),
        name="peer_route",
    )(x, g_ffn.reshape(1, -1), w_hi, w_lo, sub_keys1, sub_keys2)


def _peer_kernel(ht_ref, x_ref, u_ref, vt_ref, r2_ref, e2_ref, n1_ref, c1_ref, y_ref, acc_ref):
    j = pl.program_id(1)

    @pl.when(j == 0)
    def _():
        acc_ref[...] = jnp.zeros_like(acc_ref)

    t = ht_ref.shape[1]
    n_i1 = PEER_EXPERT_TILE // P_NKEYS
    zero = jnp.zeros((), BF16)
    def token_row(row):
        return jnp.tile(jnp.broadcast_to(row, (2 * SUBLANES, t)).astype(BF16), (P_NKEYS // (2 * SUBLANES), 1))

    a = jnp.dot(u_ref[...], ht_ref[...], preferred_element_type=F32)
    ab = a.astype(BF16)
    act = ab * (1.0 + lax.erf(ab * (1.0 / math.sqrt(2.0))))
    gates = []
    for r in range(n_i1):
        g = jnp.zeros((P_NKEYS, t), BF16)
        for head in range(P_HEADS):
            n1 = token_row(n1_ref[head, pl.ds(j * n_i1 + r, 1), :])
            c1 = token_row(0.5 * c1_ref[head, pl.ds(j * n_i1 + r, 1), :])
            g = g + jnp.where(r2_ref[head] < n1, e2_ref[head], zero) * c1
        gates.append(g)
    ga = act * jnp.concatenate(gates, axis=0)
    acc_ref[...] += jnp.dot(vt_ref[0], ga, preferred_element_type=F32)

    @pl.when(j == pl.num_programs(1) - 1)
    def _():
        y_ref[...] = x_ref[...] + jnp.transpose(acc_ref[...])


def _peer(ht, x, r2, e2, n1, c1, u, vt, tt):
    n = x.shape[0]
    et = PEER_EXPERT_TILE
    ne = u.shape[0] // et
    once = pl.Buffered(1)
    keys = pl.BlockSpec((P_HEADS, P_NKEYS, tt), lambda i, j: (0, 0, i))
    return pl.pallas_call(
        _peer_kernel,
        grid=(n // tt, ne),
        in_specs=[pl.BlockSpec((D_MODEL, tt), lambda i, j: (0, i)),
                  pl.BlockSpec((tt, D_MODEL), lambda i, j: (i, 0), pipeline_mode=once),
                  pl.BlockSpec((et, D_MODEL), lambda i, j: (j, 0)),
                  pl.BlockSpec((1, D_MODEL, et), lambda i, j: (j, 0, 0)),
                  keys, keys, keys, keys],
        out_specs=pl.BlockSpec((tt, D_MODEL), lambda i, j: (i, 0), pipeline_mode=once),
        out_shape=jax.ShapeDtypeStruct((n, D_MODEL), F32),
        scratch_shapes=[pltpu.VMEM((D_MODEL, tt), F32)],
        compiler_params=_params("parallel", "arbitrary"),
        name="peer_experts",
    )(ht, x, u, vt, r2, e2, n1, c1)


def _layer(xp, xs, cache_k, cache_v, state, cmem_k, cmem_v, mem_prompt, past_len,
           g_mix, w_in, g_q, g_k, rel_bias, w_gate2, b_gate2, g_gla_out, w_out,
           g_cross, g_mem, w_cq, w_ck, w_cv, g_cq, g_ck, w_co, g_ffn, w_pq, sub_keys1, sub_keys2, expert_u, expert_v):
    bp, sp, _ = xp.shape
    bs, ts, _ = xs.shape
    xp = xp.reshape(bp * sp, D_MODEL)
    xs = xs.reshape(bs * ts, D_MODEL)
    u = expert_u.astype(BF16)
    ne = expert_v.shape[0] // PEER_EXPERT_TILE
    vt = jnp.swapaxes(expert_v.reshape(ne, PEER_EXPERT_TILE, D_MODEL), 1, 2).astype(BF16)
    heads = lambda a, b, t: a.reshape(b, t, A_HEADS, A_HEAD_DIM)

    def peer(x, peer_tile):
        ht, r2, e2, n1, c1 = _route(x, g_ffn, w_pq, sub_keys1, sub_keys2)
        return _peer(ht, x, r2, e2, n1, c1, u, vt, peer_tile)

    qa, ka_f, ka_b, va_f, va_b, qg, kg, vg, la, sr = _mixer_in(xp, g_mix, w_in, g_q, g_k, w_gate2, b_gate2)
    oa = _attn_prompt(qa, ka_b, va_b, rel_bias, bp, sp)
    og, gla_p = _gla(qg, kg, vg, la, None, bp, sp, GLA_TILE)
    mk, mv, mk_b, mv_b = _mem_kv(mem_prompt, g_mem, w_ck, w_cv, g_ck)
    xp = peer(_mixer_out_cross(xp, oa, og, sr, g_gla_out, w_out, mk_b, mv_b, g_cross, w_cq, g_cq, w_co,
                               bp, sp, TOKEN_TILE), PEER_TOKEN_TILE)
    keep = min(BAND, sp)
    outs_p = (heads(ka_f, bp, sp)[:, -keep:], heads(va_f, bp, sp)[:, -keep:], gla_p, mk, mv)

    qa, ka_f, ka_b, va_f, va_b, qg, kg, vg, la, sr = _mixer_in(xs, g_mix, w_in, g_q, g_k, w_gate2, b_gate2)
    oa = _attn_sample(qa, ka_b, va_b, cache_k, cache_v, rel_bias, bs, ts, past_len)
    og, gla_s = _gla(qg, kg, vg, la, state, bs, ts, ts)
    xs = _mixer_out(xs, oa, og, sr, g_gla_out, w_out)
    xs = _cross_streams(xs, cmem_k.reshape(bs, N_MEM, D_MODEL), cmem_v.reshape(bs, N_MEM, D_MODEL),
                        g_cross, w_cq, g_cq, w_co, bs, ts)
    xs = peer(xs, bs * ts)
    outs_s = (heads(ka_f, bs, ts), heads(va_f, bs, ts), gla_s)

    return (xp.reshape(bp, sp, D_MODEL), xs.reshape(bs, ts, D_MODEL)) + outs_p + outs_s


def kernel(x_prompt, x_sample, cache_att_k, cache_att_v, state_gla, cache_mem_k, cache_mem_v, mem_prompt, g_mix, w_in, g_q, g_k, rel_bias, w_gate2, b_gate2, g_gla_out, w_out, g_cross, g_mem, w_cq, w_ck, w_cv, g_cq, g_ck, w_co, g_ffn, w_pq, sub_keys1, sub_keys2, expert_u, expert_v):
    depth = w_in.shape[0]
    past_len = 4096
    xp, xs = x_prompt, x_sample
    per_layer = []
    for l in range(depth):
        outs = _layer(xp, xs, cache_att_k[l], cache_att_v[l], state_gla[l], cache_mem_k[l], cache_mem_v[l],
                      mem_prompt, past_len,
                      g_mix[l], w_in[l], g_q[l], g_k[l], rel_bias[l], w_gate2[l], b_gate2[l], g_gla_out[l], w_out[l],
                      g_cross[l], g_mem[l], w_cq[l], w_ck[l], w_cv[l], g_cq[l], g_ck[l], w_co[l],
                      g_ffn[l], w_pq[l], sub_keys1[l], sub_keys2[l], expert_u[l], expert_v[l])
        xp, xs = outs[0], outs[1]
        per_layer.append(outs[2:])
    stacked = [jnp.stack([p[i] for p in per_layer]) for i in range(8)]
    return (xp, xs) + tuple(stacked)
```

```python
import functools
import math

import jax
import jax.numpy as jnp
from jax import lax
from jax.experimental import pallas as pl
from jax.experimental.pallas import tpu as pltpu

F32 = jnp.float32
BF16 = jnp.bfloat16
HIGHEST = lax.Precision.HIGHEST

D_MODEL = 1024
CHUNK = 64
N_PAST_CHUNKS = 8
BAND = (N_PAST_CHUNKS + 1) * CHUNK
A_HEADS = 8
A_HEAD_DIM = 64
A_WIDTH = A_HEADS * A_HEAD_DIM
REL_CLIP = 128
G_HEADS = 4
G_DK = 64
G_DV = 128
G_KW = G_HEADS * G_DK
G_WIDTH = G_HEADS * G_DV
G_GATE_RANK = 16
G_GATE_NORM = 16.0
G_BLOCK = 16
N_MEM = 256
X_HEADS = 4
X_HEAD_DIM = D_MODEL // X_HEADS
P_HEADS = 8
P_NKEYS = 128
P_QDIM = 256
P_HALF = P_QDIM // 2
P_TOPK = 16
EPS = 1e-6
NEG = -1e30
LOG2E = math.log2(math.e)

LANES = 128
SUBLANES = 8
VMEM_LIMIT = 56 * 1024 * 1024

TOKEN_TILE = 512
ROUTE_TILE = 256
GLA_TILE = 256
PEER_TOKEN_TILE = 1024
PEER_EXPERT_TILE = 1024


def _params(*sem):
    return pltpu.CompilerParams(dimension_semantics=sem, vmem_limit_bytes=VMEM_LIMIT)


def _full(shape):
    return pl.BlockSpec(shape, lambda *_: (0,) * len(shape))


def _rms(x, g):
    return x * lax.rsqrt(jnp.mean(x * x, axis=-1, keepdims=True) + EPS) * g


def _seg_mean_sq(z, seg_avg):
    sq = z * z
    hi = sq.astype(BF16)
    lo = (sq - hi.astype(F32)).astype(BF16)
    return (jnp.dot(hi, seg_avg, preferred_element_type=F32)
            + jnp.dot(lo, seg_avg, preferred_element_type=F32))


def _seg_avg_matrix(width, seg):
    r = jnp.arange(width) // seg
    return jnp.where(r[:, None] == r[None, :], 1.0 / seg, 0.0).astype(BF16)


def _mixer_in_kernel(x_ref, gmix_ref, w_ref, gq_ref, gk_ref, avg_ref, wg2_ref, bg2_ref,
                     qa_ref, kaf_ref, kab_ref, vaf_ref, vab_ref, qg_ref, kg_ref, vg_ref, la_ref, sr_ref):
    h = _rms(x_ref[...], gmix_ref[...]).astype(BF16)
    z = jnp.dot(h, w_ref[...], preferred_element_type=F32)
    o = 0
    qa = z[:, o:o + A_WIDTH]; o += A_WIDTH
    ka = z[:, o:o + A_WIDTH]; o += A_WIDTH
    va = z[:, o:o + A_WIDTH]; o += A_WIDTH
    qg = z[:, o:o + G_KW]; o += G_KW
    kg = z[:, o:o + G_KW]; o += G_KW
    vg = z[:, o:o + G_WIDTH]; o += G_WIDTH
    r = z[:, o:o + G_WIDTH]; o += G_WIDTH
    gl = z[:, o:o + LANES]
    avg = avg_ref[...]
    qa = qa * lax.rsqrt(_seg_mean_sq(qa, avg) + EPS) * gq_ref[...]
    ka = ka * lax.rsqrt(_seg_mean_sq(ka, avg) + EPS) * gk_ref[...]
    qa_ref[...] = (qa * (A_HEAD_DIM ** -0.5 * LOG2E)).astype(BF16)
    kaf_ref[...] = ka
    kab_ref[...] = ka.astype(BF16)
    vaf_ref[...] = va
    vab_ref[...] = va.astype(BF16)
    qg_ref[...] = qg * (G_DK ** -0.5)
    kg_ref[...] = kg
    vg_ref[...] = vg
    pre = jnp.dot(gl, wg2_ref[...], precision=HIGHEST, preferred_element_type=F32) + bg2_ref[...]
    la_ref[...] = (jnp.minimum(pre, 0.0) - jnp.log1p(jnp.exp(-jnp.abs(pre)))) / G_GATE_NORM
    sr_ref[...] = r * jax.nn.sigmoid(r)


def _mixer_in(x, g_mix, w_in, g_q, g_k, w_gate2, b_gate2):
    n = x.shape[0]
    c_gl = 3 * A_WIDTH + 2 * G_KW + G_WIDTH
    w_cat = jnp.concatenate(
        [w_in[:, :c_gl], w_in[:, c_gl + G_GATE_RANK:],
         jnp.pad(w_in[:, c_gl:c_gl + G_GATE_RANK], ((0, 0), (0, LANES - G_GATE_RANK)))], axis=1).astype(BF16)
    wg2 = jnp.pad(w_gate2, ((0, LANES - G_GATE_RANK), (0, 0)))
    wcols = w_cat.shape[1]
    tm = TOKEN_TILE
    row = lambda w: pl.BlockSpec((tm, w), lambda i: (i, 0))
    outs = [(A_WIDTH, BF16), (A_WIDTH, F32), (A_WIDTH, BF16), (A_WIDTH, F32), (A_WIDTH, BF16),
            (G_KW, F32), (G_KW, F32), (G_WIDTH, F32), (G_KW, F32), (G_WIDTH, F32)]
    return pl.pallas_call(
        _mixer_in_kernel,
        grid=(n // tm,),
        in_specs=[row(D_MODEL), _full((1, D_MODEL)), _full((D_MODEL, wcols)), _full((1, A_WIDTH)),
                  _full((1, A_WIDTH)), _full((A_WIDTH, A_WIDTH)), _full((LANES, G_KW)), _full((1, G_KW))],
        out_specs=[row(w) for w, _ in outs],
        out_shape=[jax.ShapeDtypeStruct((n, w), dt) for w, dt in outs],
        compiler_params=_params("parallel"),
        name="mixer_in",
    )(x, g_mix.reshape(1, -1), w_cat, jnp.tile(g_q, A_HEADS).reshape(1, -1), jnp.tile(g_k, A_HEADS).reshape(1, -1),
      _seg_avg_matrix(A_WIDTH, A_HEAD_DIM), wg2, b_gate2.reshape(1, -1))


def _attend(q, segments):
    tq = q.shape[0]
    lane = lax.broadcasted_iota(jnp.int32, (tq, LANES), 1)
    outs = []
    for pair in range(A_HEADS // 2):
        sl = slice(pair * LANES, (pair + 1) * LANES)
        q2 = q[:, sl]
        halves = []
        for half in range(2):
            head = 2 * pair + half
            mine = (lane < A_HEAD_DIM) if half == 0 else (lane >= A_HEAD_DIM)
            qm = jnp.where(mine, q2, jnp.zeros_like(q2))
            scores = [lax.dot_general(qm, k[:, sl], (((1,), (1,)), ((), ())), preferred_element_type=F32) + bias(head)
                      for k, _, bias in segments]
            m = functools.reduce(jnp.maximum, [jnp.max(s, axis=-1, keepdims=True) for s in scores])
            es = [jnp.exp2(s - m) for s in scores]
            inv = 1.0 / functools.reduce(jnp.add, [jnp.sum(e, axis=-1, keepdims=True) for e in es])
            o = functools.reduce(jnp.add, [
                jnp.dot(e.astype(BF16), v[:, sl], preferred_element_type=F32)
                for e, (_, v, _) in zip(es, segments)])
            halves.append(o * inv)
        outs.append(jnp.where(lane < A_HEAD_DIM, halves[0], halves[1]))
    return jnp.concatenate(outs, axis=1)


ATT_GROUP = 4
ATT_ROWS = ATT_GROUP * CHUNK
ATT_WINDOW = (N_PAST_CHUNKS + ATT_GROUP) * CHUNK
ATT_CASES = -(-N_PAST_CHUNKS // ATT_GROUP) + 1
BIAS_WRAP = 1024


def _attn_prompt_kernel(q_ref, k_ref, v_ref, tb_ref, o_ref, bias_ref):
    g = pl.program_id(1)
    first = jnp.maximum(g * ATT_GROUP - N_PAST_CHUNKS, 0)
    start = pl.multiple_of(first * CHUNK, CHUNK)

    for case in range(ATT_CASES):
        @pl.when(g == case)
        def _(case=case):
            off = min(case * ATT_GROUP, N_PAST_CHUNKS) * CHUNK
            q_chunk = (off + lax.broadcasted_iota(jnp.int32, (ATT_ROWS, ATT_WINDOW), 0)) // CHUNK
            k_chunk = lax.broadcasted_iota(jnp.int32, (ATT_ROWS, ATT_WINDOW), 1) // CHUNK
            valid = (k_chunk <= q_chunk) & (k_chunk >= q_chunk - N_PAST_CHUNKS)
            for head in range(A_HEADS):
                table = jnp.broadcast_to(tb_ref[head:head + 1, :], (ATT_ROWS, BIAS_WRAP))
                rolled = pltpu.roll(table, off, 1, stride=1, stride_axis=0)
                bias_ref[head] = jnp.where(valid, rolled[:, :ATT_WINDOW], NEG)

    kb = k_ref[0, pl.ds(start, ATT_WINDOW), :]
    vb = v_ref[0, pl.ds(start, ATT_WINDOW), :]
    o_ref[...] = _attend(q_ref[...], [(kb, vb, lambda head: bias_ref[head])]).astype(o_ref.dtype)


def _wrapped_bias_table(rel_bias):
    u = jnp.arange(BIAS_WRAP)
    dist = jnp.where(u < CHUNK, -u, BIAS_WRAP - u)
    return rel_bias[:, jnp.clip(dist, -REL_CLIP, REL_CLIP) + REL_CLIP].astype(F32) * LOG2E


def _attn_prompt(qa, ka, va, rel_bias, batch, seq):
    ng = seq // ATT_ROWS
    assert seq % ATT_ROWS == 0 and ng >= ATT_CASES and seq >= ATT_WINDOW
    kv = pl.BlockSpec((1, seq, A_WIDTH), lambda b, g: (b, 0, 0))
    return pl.pallas_call(
        _attn_prompt_kernel,
        grid=(batch, ng),
        in_specs=[pl.BlockSpec((ATT_ROWS, A_WIDTH), lambda b, g: (b * ng + g, 0)), kv, kv,
                  _full((A_HEADS, BIAS_WRAP))],
        out_specs=pl.BlockSpec((ATT_ROWS, A_WIDTH), lambda b, g: (b * ng + g, 0)),
        out_shape=jax.ShapeDtypeStruct((batch * seq, A_WIDTH), BF16),
        scratch_shapes=[pltpu.VMEM((A_HEADS, ATT_ROWS, ATT_WINDOW), F32)],
        compiler_params=_params("arbitrary", "arbitrary"),
        name="attn_prompt",
    )(qa, ka.reshape(batch, seq, A_WIDTH), va.reshape(batch, seq, A_WIDTH), _wrapped_bias_table(rel_bias))


SAMPLE_STREAMS = 4


def _attn_sample_kernel(q_ref, kn_ref, vn_ref, kc_ref, vc_ref, tb_ref, o_ref, *, t, first_valid):
    r = kc_ref.shape[1]
    valid_c = lax.broadcasted_iota(jnp.int32, (t, r), 1) >= first_valid
    bias_c, bias_n = [], []
    for head in range(A_HEADS):
        table = jnp.broadcast_to(tb_ref[head:head + 1, :], (t, BIAS_WRAP))
        bias_c.append(jnp.where(valid_c, pltpu.roll(table, r % BIAS_WRAP, 1, stride=1, stride_axis=0)[:, :r], NEG))
        bias_n.append(pltpu.roll(table, 0, 1, stride=1, stride_axis=0)[:, :t])
    for stream in range(kc_ref.shape[0]):
        rows = slice(stream * t, (stream + 1) * t)
        segs = [(kc_ref[stream].astype(BF16), vc_ref[stream].astype(BF16), lambda head: bias_c[head]),
                (kn_ref[rows, :], vn_ref[rows, :], lambda head: bias_n[head])]
        o_ref[rows, :] = _attend(q_ref[rows, :], segs).astype(o_ref.dtype)


def _attn_sample(qa, ka, va, cache_k, cache_v, rel_bias, batch, t, past_len):
    r = cache_k.shape[1]
    q_chunk = past_len // CHUNK
    assert (past_len + t - 1) // CHUNK == q_chunk and r <= BAND
    first_valid = max(0, (q_chunk - N_PAST_CHUNKS) * CHUNK - (past_len - r))
    sb = math.gcd(batch, SAMPLE_STREAMS)
    tok = pl.BlockSpec((sb * t, A_WIDTH), lambda b: (b, 0))
    cache = pl.BlockSpec((sb, r, A_WIDTH), lambda b: (b, 0, 0))
    return pl.pallas_call(
        functools.partial(_attn_sample_kernel, t=t, first_valid=first_valid),
        grid=(batch // sb,),
        in_specs=[tok, tok, tok, cache, cache, _full((A_HEADS, BIAS_WRAP))],
        out_specs=tok,
        out_shape=jax.ShapeDtypeStruct((batch * t, A_WIDTH), BF16),
        compiler_params=_params("parallel"),
        name="attn_sample",
    )(qa, ka, va, cache_k.reshape(batch, r, A_WIDTH), cache_v.reshape(batch, r, A_WIDTH),
      _wrapped_bias_table(rel_bias))


def _gla_kernel(q_ref, k_ref, v_ref, la_ref, *rest):
    s0_ref = rest[0] if len(rest) == 7 else None
    tri_ref, expand_ref, bdmask_ref, o_ref, sfin_ref, state_ref = rest[-6:]
    tb = q_ref.shape[0]
    nb = tb // G_BLOCK

    @pl.when(pl.program_id(1) == 0)
    def _():
        state_ref[...] = jnp.zeros_like(state_ref) if s0_ref is None else s0_ref[0]

    q = q_ref[...]
    k = k_ref[...]
    v = v_ref[...]
    cum = jnp.dot(tri_ref[...], la_ref[...], precision=HIGHEST, preferred_element_type=F32)
    q3 = q.reshape(nb, G_BLOCK, G_KW)
    k3 = k.reshape(nb, G_BLOCK, G_KW)
    v3 = v.reshape(nb, G_BLOCK, G_WIDTH)
    cum3 = cum.reshape(nb, G_BLOCK, G_KW)
    row = lax.broadcasted_iota(jnp.int32, (nb, G_BLOCK, G_KW), 1)
    expand = expand_ref[...]

    intra = jnp.zeros((nb, G_BLOCK, G_WIDTH), F32)
    for s in range(G_BLOCK):
        k_s = jnp.broadcast_to(k3[:, s:s + 1, :], (nb, G_BLOCK, G_KW))
        c_s = jnp.broadcast_to(cum3[:, s:s + 1, :], (nb, G_BLOCK, G_KW))
        v_s = jnp.broadcast_to(v3[:, s:s + 1, :], (nb, G_BLOCK, G_WIDTH))
        prod = q3 * k_s * jnp.exp(jnp.where(row >= s, cum3 - c_s, -jnp.inf))
        att = jnp.dot(prod.reshape(tb, G_KW).astype(BF16), expand, preferred_element_type=F32)
        intra = intra + att.reshape(nb, G_BLOCK, G_WIDTH) * v_s
    o_ref[...] = intra.reshape(tb, G_WIDTH)

    last3 = jnp.broadcast_to(cum3[:, G_BLOCK - 1:G_BLOCK, :], (nb, G_BLOCK, G_KW))
    q_dec = (q3 * jnp.exp(cum3)).reshape(tb, G_KW).astype(BF16)
    k_dec = (k3 * jnp.exp(last3 - cum3)).reshape(tb, G_KW).astype(BF16)
    carry = jnp.exp(last3).reshape(tb, G_KW)
    vb = v.astype(BF16)
    bdmask = bdmask_ref[...]
    for j in range(nb):
        rows = slice(j * G_BLOCK, (j + 1) * G_BLOCK)
        state = state_ref[...]
        o_ref[rows, :] += lax.dot_general(q_dec[rows], state.astype(BF16), (((1,), (1,)), ((), ())),
                                          preferred_element_type=F32)
        upd = lax.dot_general(vb[rows], k_dec[rows], (((0,), (0,)), ((), ())), preferred_element_type=F32)
        state_ref[...] = carry[j * G_BLOCK:j * G_BLOCK + 1, :] * state + upd * bdmask

    @pl.when(pl.program_id(1) == pl.num_programs(1) - 1)
    def _():
        sfin_ref[0] = state_ref[...]


def _gla(qg, kg, vg, la, s0, batch, t, tile):
    nt = t // tile
    hsel = jnp.arange(G_KW) // G_DK
    vsel = jnp.arange(G_WIDTH) // G_DV
    same = hsel[:, None] == vsel[None, :]
    expand = same.astype(BF16)
    bdmask = same.T.astype(F32)
    state_in = []
    if s0 is not None:
        s0_t = jnp.swapaxes(s0, 2, 3).reshape(batch, G_WIDTH, G_DK)
        state_in = [jnp.where(same.T[None], jnp.tile(s0_t, (1, 1, G_HEADS)), 0.0)]
    rr = jnp.arange(tile)
    tri = ((rr[:, None] // G_BLOCK == rr[None, :] // G_BLOCK) & (rr[None, :] <= rr[:, None])).astype(F32)
    tok = lambda w: pl.BlockSpec((tile, w), lambda b, i: (b * nt + i, 0))
    st = pl.BlockSpec((1, G_WIDTH, G_KW), lambda b, i: (b, 0, 0))
    o, s_bd = pl.pallas_call(
        _gla_kernel,
        grid=(batch, nt),
        in_specs=[tok(G_KW), tok(G_KW), tok(G_WIDTH), tok(G_KW)] + [st] * len(state_in)
        + [_full((tile, tile)), _full((G_KW, G_WIDTH)), _full((G_WIDTH, G_KW))],
        out_specs=[tok(G_WIDTH), st],
        out_shape=[jax.ShapeDtypeStruct((batch * t, G_WIDTH), F32),
                   jax.ShapeDtypeStruct((batch, G_WIDTH, G_KW), F32)],
        scratch_shapes=[pltpu.VMEM((G_WIDTH, G_KW), F32)],
        compiler_params=_params("parallel", "arbitrary"),
        name="gla",
    )(qg, kg, vg, la, *state_in, tri, expand, bdmask)
    s_fin = jnp.stack([s_bd[:, h * G_DV:(h + 1) * G_DV, h * G_DK:(h + 1) * G_DK] for h in range(G_HEADS)], axis=1)
    return o, jnp.swapaxes(s_fin, 2, 3)


def _mixer_out_kernel(x_ref, oa_ref, og_ref, sr_ref, gg_ref, avg_ref, wa_ref, wg_ref, y_ref):
    og = og_ref[...]
    og = og * lax.rsqrt(_seg_mean_sq(og, avg_ref[...]) + EPS) * gg_ref[...] * sr_ref[...]
    y = jnp.dot(oa_ref[...], wa_ref[...], preferred_element_type=F32)
    y = y + jnp.dot(og.astype(BF16), wg_ref[...], preferred_element_type=F32)
    y_ref[...] = x_ref[...] + y


def _mixer_out(x, oa, og, sr, g_gla_out, w_out):
    n = x.shape[0]
    tm = TOKEN_TILE
    row = lambda w: pl.BlockSpec((tm, w), lambda i: (i, 0))
    w_out = w_out.astype(BF16)
    return pl.pallas_call(
        _mixer_out_kernel,
        grid=(n // tm,),
        in_specs=[row(D_MODEL), row(A_WIDTH), row(G_WIDTH), row(G_WIDTH), _full((1, G_WIDTH)),
                  _full((G_WIDTH, G_WIDTH)), _full((A_WIDTH, D_MODEL)), _full((G_WIDTH, D_MODEL))],
        out_specs=row(D_MODEL),
        out_shape=jax.ShapeDtypeStruct((n, D_MODEL), F32),
        compiler_params=_params("parallel"),
        name="mixer_out",
    )(x, oa, og, sr, jnp.tile(g_gla_out, G_HEADS).reshape(1, -1), _seg_avg_matrix(G_WIDTH, G_DV),
      w_out[:A_WIDTH], w_out[A_WIDTH:])


def _head_rms(z, g_ref, scale=1.0):
    outs = []
    for head in range(X_HEADS):
        sl = slice(head * X_HEAD_DIM, (head + 1) * X_HEAD_DIM)
        zh = z[:, sl]
        outs.append(zh * (lax.rsqrt(jnp.mean(zh * zh, axis=-1, keepdims=True) + EPS) * scale) * g_ref[:, sl])
    return outs


def _mem_kv_kernel(m_ref, gm_ref, wk_ref, wv_ref, gk_ref, k_ref, v_ref, kb_ref, vb_ref):
    m = _rms(m_ref[...], gm_ref[...]).astype(BF16)
    k = _head_rms(jnp.dot(m, wk_ref[...], preferred_element_type=F32), gk_ref)
    v = jnp.dot(m, wv_ref[...], preferred_element_type=F32)
    for head in range(X_HEADS):
        k_ref[:, head, :] = k[head]
        v_ref[:, head, :] = v[:, head * X_HEAD_DIM:(head + 1) * X_HEAD_DIM]
    kb_ref[...] = jnp.concatenate(k, axis=1).astype(BF16)
    vb_ref[...] = v.astype(BF16)


def _mem_kv(mem, g_mem, w_ck, w_cv, g_ck):
    batch = mem.shape[0]
    sq = _full((D_MODEL, D_MODEL))
    vec = _full((1, D_MODEL))
    out = pl.BlockSpec((None, N_MEM, X_HEADS, X_HEAD_DIM), lambda i: (i, 0, 0, 0))
    flat = pl.BlockSpec((None, N_MEM, D_MODEL), lambda i: (i, 0, 0))
    return pl.pallas_call(
        _mem_kv_kernel,
        grid=(batch,),
        in_specs=[flat, vec, sq, sq, vec],
        out_specs=[out, out, flat, flat],
        out_shape=[jax.ShapeDtypeStruct((batch, N_MEM, X_HEADS, X_HEAD_DIM), F32)] * 2
        + [jax.ShapeDtypeStruct((batch, N_MEM, D_MODEL), BF16)] * 2,
        compiler_params=_params("parallel"),
        name="mem_kv",
    )(mem, g_mem.reshape(1, -1), w_ck.astype(BF16), w_cv.astype(BF16), jnp.tile(g_ck, X_HEADS).reshape(1, -1))


def _mixer_out_cross_kernel(x_ref, oa_ref, og_ref, sr_ref, gg_ref, avg_ref, wa_ref, wg_ref,
                            gc_ref, wq_ref, gq_ref, mk_ref, mv_ref, wo_ref, y_ref):
    og = og_ref[...]
    og = og * lax.rsqrt(_seg_mean_sq(og, avg_ref[...]) + EPS) * gg_ref[...] * sr_ref[...]
    x = x_ref[...] + jnp.dot(oa_ref[...], wa_ref[...], preferred_element_type=F32)
    x = x + jnp.dot(og.astype(BF16), wg_ref[...], preferred_element_type=F32)
    h = _rms(x, gc_ref[...]).astype(BF16)
    q = _head_rms(jnp.dot(h, wq_ref[...], preferred_element_type=F32), gq_ref, X_HEAD_DIM ** -0.5)
    outs = []
    for head in range(X_HEADS):
        sl = slice(head * X_HEAD_DIM, (head + 1) * X_HEAD_DIM)
        s = lax.dot_general(q[head].astype(BF16), mk_ref[:, sl], (((1,), (1,)), ((), ())),
                            preferred_element_type=F32)
        e = jnp.exp(s - jnp.max(s, axis=-1, keepdims=True))
        p = (e * (1.0 / jnp.sum(e, axis=-1, keepdims=True))).astype(BF16)
        outs.append(jnp.dot(p, mv_ref[:, sl], preferred_element_type=F32))
    o = jnp.concatenate(outs, axis=1).astype(BF16)
    y_ref[...] = x + jnp.dot(o, wo_ref[...], preferred_element_type=F32)


def _cross_streams_kernel(x_ref, gc_ref, wq_ref, gq_ref, mk_ref, mv_ref, wo_ref, y_ref, q_scr, o_scr, *, t):
    b = pl.program_id(0)

    @pl.when(b == 0)
    def _():
        h = _rms(x_ref[...], gc_ref[...]).astype(BF16)
        q = _head_rms(jnp.dot(h, wq_ref[...], preferred_element_type=F32), gq_ref, X_HEAD_DIM ** -0.5)
        q_scr[...] = jnp.concatenate(q, axis=1).astype(BF16)

    rows = pl.ds(pl.multiple_of(b * t, t), t)
    q = q_scr[rows, :]
    outs = []
    for head in range(X_HEADS):
        sl = slice(head * X_HEAD_DIM, (head + 1) * X_HEAD_DIM)
        s = lax.dot_general(q[:, sl], mk_ref[0, :, sl].astype(BF16), (((1,), (1,)), ((), ())),
                            preferred_element_type=F32)
        e = jnp.exp(s - jnp.max(s, axis=-1, keepdims=True))
        p = (e * (1.0 / jnp.sum(e, axis=-1, keepdims=True))).astype(BF16)
        outs.append(jnp.dot(p, mv_ref[0, :, sl].astype(BF16), preferred_element_type=F32))
    o_scr[rows, :] = jnp.concatenate(outs, axis=1).astype(BF16)

    @pl.when(b == pl.num_programs(0) - 1)
    def _():
        y_ref[...] = x_ref[...] + jnp.dot(o_scr[...], wo_ref[...], preferred_element_type=F32)


def _cross_streams(x, mem_k, mem_v, g_cross, w_cq, g_cq, w_co, batch, t):
    n = batch * t
    whole = _full((n, D_MODEL))
    mem = pl.BlockSpec((1, N_MEM, D_MODEL), lambda b: (b, 0, 0))
    sq = _full((D_MODEL, D_MODEL))
    vec = _full((1, D_MODEL))
    return pl.pallas_call(
        functools.partial(_cross_streams_kernel, t=t),
        grid=(batch,),
        in_specs=[whole, vec, sq, vec, mem, mem, sq],
        out_specs=whole,
        out_shape=jax.ShapeDtypeStruct((n, D_MODEL), F32),
        scratch_shapes=[pltpu.VMEM((n, D_MODEL), BF16), pltpu.VMEM((n, D_MODEL), BF16)],
        compiler_params=_params("arbitrary"),
        name="cross_streams",
    )(x, g_cross.reshape(1, -1), w_cq.astype(BF16), jnp.tile(g_cq, X_HEADS).reshape(1, -1),
      mem_k, mem_v, w_co.astype(BF16))


def _mixer_out_cross(x, oa, og, sr, g_gla_out, w_out, mem_k, mem_v, g_cross, w_cq, g_cq, w_co, batch, t, tile):
    nt = t // tile
    row = lambda w: pl.BlockSpec((tile, w), lambda b, i: (b * nt + i, 0))
    mem = pl.BlockSpec((None, N_MEM, D_MODEL), lambda b, i: (b, 0, 0))
    sq = _full((D_MODEL, D_MODEL))
    vec = _full((1, D_MODEL))
    w_out = w_out.astype(BF16)
    return pl.pallas_call(
        _mixer_out_cross_kernel,
        grid=(batch, nt),
        in_specs=[row(D_MODEL), row(A_WIDTH), row(G_WIDTH), row(G_WIDTH), _full((1, G_WIDTH)),
                  _full((G_WIDTH, G_WIDTH)), _full((A_WIDTH, D_MODEL)), _full((G_WIDTH, D_MODEL)),
                  vec, sq, vec, mem, mem, sq],
        out_specs=row(D_MODEL),
        out_shape=jax.ShapeDtypeStruct((batch * t, D_MODEL), F32),
        compiler_params=_params("parallel", "parallel"),
        name="mixer_out_cross",
    )(x, oa, og, sr, jnp.tile(g_gla_out, G_HEADS).reshape(1, -1), _seg_avg_matrix(G_WIDTH, G_DV),
      w_out[:A_WIDTH], w_out[A_WIDTH:], g_cross.reshape(1, -1), w_cq.astype(BF16),
      jnp.tile(g_cq, X_HEADS).reshape(1, -1), mem_k, mem_v, w_co.astype(BF16))


CAND_ROWS = 64


def _sorting_network(n):
    pairs = []
    p = 1
    while p < n:
        k = p
        while k >= 1:
            for j in range(k % p, n - k, 2 * k):
                for i in range(min(k, n - j - k)):
                    if (i + j) // (2 * p) == (i + j + k) // (2 * p):
                        pairs.append((i + j, i + j + k))
            k //= 2
        p *= 2
    return pairs


def _top_values_tiled(s, count):
    nparts = s.shape[0] // SUBLANES
    parts = [s[SUBLANES * k:SUBLANES * (k + 1), :] for k in range(nparts)]
    for i, j in _sorting_network(nparts):
        parts[i], parts[j] = jnp.maximum(parts[i], parts[j]), jnp.minimum(parts[i], parts[j])
    which = lax.broadcasted_iota(jnp.int32, parts[0].shape, 0).astype(F32)
    exhausted = jnp.full(parts[0].shape, -jnp.inf, F32)
    out = []
    for r in range(count):
        mx = jnp.max(parts[0], axis=0, keepdims=True)
        first = jnp.min(jnp.where(parts[0] == mx, which, float(SUBLANES)), axis=0, keepdims=True)
        pop = which == first
        out.append(mx)
        for k in range(min(count - r - 1, nparts)):
            parts[k] = jnp.where(pop, parts[k + 1] if k + 1 < nparts else exhausted, parts[k])
    return jnp.concatenate(out, axis=0)


def _count_leading(desc, x, cmp):
    assert P_TOPK == 16
    row = lambda b: desc[b:b + 1, :]
    pick = lambda c, hi, lo: jnp.where(c, hi, lo)
    c8 = cmp(row(7), x)
    c4 = cmp(pick(c8, row(11), row(3)), x)
    c2 = cmp(pick(c8, pick(c4, row(13), row(9)), pick(c4, row(5), row(1))), x)
    c1 = cmp(pick(c8, pick(c4, pick(c2, row(14), row(12)), pick(c2, row(10), row(8))),
                  pick(c4, pick(c2, row(6), row(4)), pick(c2, row(2), row(0)))), x)
    c_last = cmp(row(15), x)
    count = pick(c8, 8.0, 0.0) + pick(c4, 4.0, 0.0) + pick(c2, 2.0, 0.0) + pick(c1, 1.0, 0.0)
    return count + pick(c_last, 1.0, 0.0)


def _route_kernel(x_ref, gf_ref, whi_ref, wlo_ref, sk1_ref, sk2_ref, ht_ref, r2_ref, e2_ref, n1_ref, c1_ref):
    hf = _rms(x_ref[...], gf_ref[...])
    ht_ref[...] = jnp.transpose(hf).astype(BF16)
    h_hi = hf.astype(BF16)
    h_lo = (hf - h_hi.astype(F32)).astype(BF16)
    qry = (jnp.dot(h_hi, whi_ref[...], preferred_element_type=F32)
           + jnp.dot(h_hi, wlo_ref[...], preferred_element_type=F32)
           + jnp.dot(h_lo, whi_ref[...], preferred_element_type=F32))
    nt = (((1,), (1,)), ((), ()))
    nk = P_TOPK + 1
    t = qry.shape[0]
    for head in range(P_HEADS):
        q1 = qry[:, head * P_QDIM:head * P_QDIM + P_HALF]
        q2 = qry[:, head * P_QDIM + P_HALF:(head + 1) * P_QDIM]
        s1 = lax.dot_general(sk1_ref[...], q1, nt, precision=HIGHEST, preferred_element_type=F32)
        s2 = lax.dot_general(sk2_ref[...], q2, nt, precision=HIGHEST, preferred_element_type=F32)
        v1 = _top_values_tiled(s1, nk)
        v2 = _top_values_tiled(s2, nk)
        slabs = [v1[a:a + 1, :] + v2[:nk // (a + 1), :] for a in range(nk)]
        pad = CAND_ROWS - sum(sl.shape[0] for sl in slabs)
        cand = jnp.concatenate(slabs + [jnp.full((pad, t), -jnp.inf, F32)], axis=0)
        top = _top_values_tiled(cand, nk)
        m1 = v1[0:1, :]
        m2 = v2[0:1, :]
        z = jnp.sum(jnp.exp(top[:P_TOPK, :] - (m1 + m2)), axis=0, keepdims=True)
        tau = 0.5 * (top[P_TOPK - 1:P_TOPK, :] + top[P_TOPK:nk, :])
        thr = tau - s1
        n1 = _count_leading(v2, thr, jnp.greater_equal)
        rank2 = _count_leading(v2, s2, jnp.greater)
        r2_ref[head] = rank2.astype(BF16)
        e2_ref[head] = jnp.exp(s2 - m2).astype(BF16)
        n1_ref[head] = n1
        c1_ref[head] = jnp.exp(s1 - m1) * (1.0 / z)


def _route(x, g_ffn, w_pq, sub_keys1, sub_keys2):
    n = x.shape[0]
    w_hi = w_pq.astype(BF16)
    w_lo = (w_pq - w_hi.astype(F32)).astype(BF16)
    tm = ROUTE_TILE
    keys = pl.BlockSpec((P_HEADS, P_NKEYS, tm), lambda i: (0, 0, i))
    tab = lambda dt: jax.ShapeDtypeStruct((P_HEADS, P_NKEYS, n), dt)
    return pl.pallas_call(
        _route_kernel,
        grid=(n // tm,),
        in_specs=[pl.BlockSpec((tm, D_MODEL), lambda i: (i, 0)), _full((1, D_MODEL)),
                  _full((D_MODEL, P_HEADS * P_QDIM)), _full((D_MODEL, P_HEADS * P_QDIM)),
                  _full((P_NKEYS, P_HALF)), _full((P_NKEYS, P_HALF))],
        out_specs=[pl.BlockSpec((D_MODEL, tm), lambda i: (0, i)), keys, keys, keys, keys],
        out_shape=[jax.ShapeDtypeStruct((D_MODEL, n), BF16), tab(BF16), tab(BF16), tab(F32), tab(F32)],
        compiler_params=_params("parallel"),
        name="peer_route",
    )(x, g_ffn.reshape(1, -1), w_hi, w_lo, sub_keys1, sub_keys2)


def _peer_kernel(ht_ref, x_ref, u_ref, vt_ref, r2_ref, e2_ref, n1_ref, c1_ref, y_ref, acc_ref):
    j = pl.program_id(1)

    @pl.when(j == 0)
    def _():
        acc_ref[...] = jnp.zeros_like(acc_ref)

    t = ht_ref.shape[1]
    n_i1 = PEER_EXPERT_TILE // P_NKEYS
    zero = jnp.zeros((), BF16)
    def token_row(row):
        return jnp.tile(jnp.broadcast_to(row, (2 * SUBLANES, t)).astype(BF16), (P_NKEYS // (2 * SUBLANES), 1))

    a = jnp.dot(u_ref[...], ht_ref[...], preferred_element_type=F32)
    ab = a.astype(BF16)
    act = ab * (1.0 + lax.erf(ab * (1.0 / math.sqrt(2.0))))
    gates = []
    for r in range(n_i1):
        g = jnp.zeros((P_NKEYS, t), BF16)
        for head in range(P_HEADS):
            n1 = token_row(n1_ref[head, pl.ds(j * n_i1 + r, 1), :])
            c1 = token_row(0.5 * c1_ref[head, pl.ds(j * n_i1 + r, 1), :])
            g = g + jnp.where(r2_ref[head] < n1, e2_ref[head], zero) * c1
        gates.append(g)
    ga = act * jnp.concatenate(gates, axis=0)
    acc_ref[...] += jnp.dot(vt_ref[0], ga, preferred_element_type=F32)

    @pl.when(j == pl.num_programs(1) - 1)
    def _():
        y_ref[...] = x_ref[...] + jnp.transpose(acc_ref[...])


def _peer(ht, x, r2, e2, n1, c1, u, vt, tt):
    n = x.shape[0]
    et = PEER_EXPERT_TILE
    ne = u.shape[0] // et
    once = pl.Buffered(1)
    keys = pl.BlockSpec((P_HEADS, P_NKEYS, tt), lambda i, j: (0, 0, i))
    return pl.pallas_call(
        _peer_kernel,
        grid=(n // tt, ne),
        in_specs=[pl.BlockSpec((D_MODEL, tt), lambda i, j: (0, i)),
                  pl.BlockSpec((tt, D_MODEL), lambda i, j: (i, 0), pipeline_mode=once),
                  pl.BlockSpec((et, D_MODEL), lambda i, j: (j, 0)),
                  pl.BlockSpec((1, D_MODEL, et), lambda i, j: (j, 0, 0)),
                  keys, keys, keys, keys],
        out_specs=pl.BlockSpec((tt, D_MODEL), lambda i, j: (i, 0), pipeline_mode=once),
        out_shape=jax.ShapeDtypeStruct((n, D_MODEL), F32),
        scratch_shapes=[pltpu.VMEM((D_MODEL, tt), F32)],
        compiler_params=_params("parallel", "arbitrary"),
        name="peer_experts",
    )(ht, x, u, vt, r2, e2, n1, c1)


def _layer(xp, xs, cache_k, cache_v, state, cmem_k, cmem_v, mem_prompt, past_len,
           g_mix, w_in, g_q, g_k, rel_bias, w_gate2, b_gate2, g_gla_out, w_out,
           g_cross, g_mem, w_cq, w_ck, w_cv, g_cq, g_ck, w_co, g_ffn, w_pq, sub_keys1, sub_keys2, expert_u, expert_v):
    bp, sp, _ = xp.shape
    bs, ts, _ = xs.shape
    xp = xp.reshape(bp * sp, D_MODEL)
    xs = xs.reshape(bs * ts, D_MODEL)
    u = expert_u.astype(BF16)
    ne = expert_v.shape[0] // PEER_EXPERT_TILE
    vt = jnp.swapaxes(expert_v.reshape(ne, PEER_EXPERT_TILE, D_MODEL), 1, 2).astype(BF16)
    heads = lambda a, b, t: a.reshape(b, t, A_HEADS, A_HEAD_DIM)

    def peer(x, peer_tile):
        ht, r2, e2, n1, c1 = _route(x, g_ffn, w_pq, sub_keys1, sub_keys2)
        return _peer(ht, x, r2, e2, n1, c1, u, vt, peer_tile)

    qa, ka_f, ka_b, va_f, va_b, qg, kg, vg, la, sr = _mixer_in(xp, g_mix, w_in, g_q, g_k, w_gate2, b_gate2)
    oa = _attn_prompt(qa, ka_b, va_b, rel_bias, bp, sp)
    og, gla_p = _gla(qg, kg, vg, la, None, bp, sp, GLA_TILE)
    mk, mv, mk_b, mv_b = _mem_kv(mem_prompt, g_mem, w_ck, w_cv, g_ck)
    xp = peer(_mixer_out_cross(xp, oa, og, sr, g_gla_out, w_out, mk_b, mv_b, g_cross, w_cq, g_cq, w_co,
                               bp, sp, TOKEN_TILE), PEER_TOKEN_TILE)
    keep = min(BAND, sp)
    outs_p = (heads(ka_f, bp, sp)[:, -keep:], heads(va_f, bp, sp)[:, -keep:], gla_p, mk, mv)

    qa, ka_f, ka_b, va_f, va_b, qg, kg, vg, la, sr = _mixer_in(xs, g_mix, w_in, g_q, g_k, w_gate2, b_gate2)
    oa = _attn_sample(qa, ka_b, va_b, cache_k, cache_v, rel_bias, bs, ts, past_len)
    og, gla_s = _gla(qg, kg, vg, la, state, bs, ts, ts)
    xs = _mixer_out(xs, oa, og, sr, g_gla_out, w_out)
    xs = _cross_streams(xs, cmem_k.reshape(bs, N_MEM, D_MODEL), cmem_v.reshape(bs, N_MEM, D_MODEL),
                        g_cross, w_cq, g_cq, w_co, bs, ts)
    xs = peer(xs, bs * ts)
    outs_s = (heads(ka_f, bs, ts), heads(va_f, bs, ts), gla_s)

    return (xp.reshape(bp, sp, D_MODEL), xs.reshape(bs, ts, D_MODEL)) + outs_p + outs_s


def kernel(x_prompt, x_sample, cache_att_k, cache_att_v, state_gla, cache_mem_k, cache_mem_v, mem_prompt, g_mix, w_in, g_q, g_k, rel_bias, w_gate2, b_gate2, g_gla_out, w_out, g_cross, g_mem, w_cq, w_ck, w_cv, g_cq, g_ck, w_co, g_ffn, w_pq, sub_keys1, sub_keys2, expert_u, expert_v):
    depth = w_in.shape[0]
    past_len = 4096
    xp, xs = x_prompt, x_sample
    per_layer = []
    for l in range(depth):
        outs = _layer(xp, xs, cache_att_k[l], cache_att_v[l], state_gla[l], cache_mem_k[l], cache_mem_v[l],
                      mem_prompt, past_len,
                      g_mix[l], w_in[l], g_q[l], g_k[l], rel_bias[l], w_gate2[l], b_gate2[l], g_gla_out[l], w_out[l],
                      g_cross[l], g_mem[l], w_cq[l], w_ck[l], w_cv[l], g_cq[l], g_ck[l], w_co[l],
                      g_ffn[l], w_pq[l], sub_keys1[l], sub_keys2[l], expert_u[l], expert_v[l])
        xp, xs = outs[0], outs[1]
        per_layer.append(outs[2:])
    stacked = [jnp.stack([p[i] for p in per_layer]) for i in range(8)]
    return (xp, xs) + tuple(stacked)
```

```python
import functools
import math

import jax
import jax.numpy as jnp
from jax import lax
from jax.experimental import pallas as pl
from jax.experimental.pallas import tpu as pltpu

F32 = jnp.float32
BF16 = jnp.bfloat16
HIGHEST = lax.Precision.HIGHEST

D_MODEL = 1024
CHUNK = 64
N_PAST_CHUNKS = 8
BAND = (N_PAST_CHUNKS + 1) * CHUNK
A_HEADS = 8
A_HEAD_DIM = 64
A_WIDTH = A_HEADS * A_HEAD_DIM
REL_CLIP = 128
G_HEADS = 4
G_DK = 64
G_DV = 128
G_KW = G_HEADS * G_DK
G_WIDTH = G_HEADS * G_DV
G_GATE_RANK = 16
G_GATE_NORM = 16.0
G_BLOCK = 16
N_MEM = 256
X_HEADS = 4
X_HEAD_DIM = D_MODEL // X_HEADS
P_HEADS = 8
P_NKEYS = 128
P_QDIM = 256
P_HALF = P_QDIM // 2
P_TOPK = 16
EPS = 1e-6
NEG = -1e30
LOG2E = math.log2(math.e)

LANES = 128
SUBLANES = 8
VMEM_LIMIT = 56 * 1024 * 1024

TOKEN_TILE = 512
ROUTE_TILE = 256
GLA_TILE = 256
PEER_TOKEN_TILE = 1024
PEER_EXPERT_TILE = 1024


def _params(*sem):
    return pltpu.CompilerParams(dimension_semantics=sem, vmem_limit_bytes=VMEM_LIMIT)


def _full(shape):
    return pl.BlockSpec(shape, lambda *_: (0,) * len(shape))


def _rms(x, g):
    return x * lax.rsqrt(jnp.mean(x * x, axis=-1, keepdims=True) + EPS) * g


def _seg_mean_sq(z, seg_avg):
    return jnp.dot((z * z).astype(BF16), seg_avg, preferred_element_type=F32)


def _seg_avg_matrix(width, seg):
    r = jnp.arange(width) // seg
    return jnp.where(r[:, None] == r[None, :], 1.0 / seg, 0.0).astype(BF16)


def _mixer_in_kernel(x_ref, gmix_ref, w_ref, gq_ref, gk_ref, avg_ref, wg2_ref, bg2_ref,
                     qa_ref, kaf_ref, kab_ref, vaf_ref, vab_ref, qg_ref, kg_ref, vg_ref, la_ref, sr_ref):
    h = _rms(x_ref[...], gmix_ref[...]).astype(BF16)
    z = jnp.dot(h, w_ref[...], preferred_element_type=F32)
    o = 0
    qa = z[:, o:o + A_WIDTH]; o += A_WIDTH
    ka = z[:, o:o + A_WIDTH]; o += A_WIDTH
    va = z[:, o:o + A_WIDTH]; o += A_WIDTH
    qg = z[:, o:o + G_KW]; o += G_KW
    kg = z[:, o:o + G_KW]; o += G_KW
    vg = z[:, o:o + G_WIDTH]; o += G_WIDTH
    r = z[:, o:o + G_WIDTH]; o += G_WIDTH
    gl = z[:, o:o + LANES]
    avg = avg_ref[...]
    qa = qa * lax.rsqrt(_seg_mean_sq(qa, avg) + EPS) * gq_ref[...]
    ka = ka * lax.rsqrt(_seg_mean_sq(ka, avg) + EPS) * gk_ref[...]
    qa_ref[...] = (qa * (A_HEAD_DIM ** -0.5 * LOG2E)).astype(BF16)
    kaf_ref[...] = ka
    kab_ref[...] = ka.astype(BF16)
    vaf_ref[...] = va
    vab_ref[...] = va.astype(BF16)
    qg_ref[...] = qg * (G_DK ** -0.5)
    kg_ref[...] = kg
    vg_ref[...] = vg
    pre = jnp.dot(gl, wg2_ref[...], precision=HIGHEST, preferred_element_type=F32) + bg2_ref[...]
    la_ref[...] = (jnp.minimum(pre, 0.0) - jnp.log1p(jnp.exp(-jnp.abs(pre)))) / G_GATE_NORM
    sr_ref[...] = r * jax.nn.sigmoid(r)


def _mixer_in(x, g_mix, w_in, g_q, g_k, w_gate2, b_gate2):
    n = x.shape[0]
    c_gl = 3 * A_WIDTH + 2 * G_KW + G_WIDTH
    w_cat = jnp.concatenate(
        [w_in[:, :c_gl], w_in[:, c_gl + G_GATE_RANK:],
         jnp.pad(w_in[:, c_gl:c_gl + G_GATE_RANK], ((0, 0), (0, LANES - G_GATE_RANK)))], axis=1).astype(BF16)
    wg2 = jnp.pad(w_gate2, ((0, LANES - G_GATE_RANK), (0, 0)))
    wcols = w_cat.shape[1]
    tm = TOKEN_TILE
    row = lambda w: pl.BlockSpec((tm, w), lambda i: (i, 0))
    outs = [(A_WIDTH, BF16), (A_WIDTH, F32), (A_WIDTH, BF16), (A_WIDTH, F32), (A_WIDTH, BF16),
            (G_KW, F32), (G_KW, F32), (G_WIDTH, F32), (G_KW, F32), (G_WIDTH, F32)]
    return pl.pallas_call(
        _mixer_in_kernel,
        grid=(n // tm,),
        in_specs=[row(D_MODEL), _full((1, D_MODEL)), _full((D_MODEL, wcols)), _full((1, A_WIDTH)),
                  _full((1, A_WIDTH)), _full((A_WIDTH, A_WIDTH)), _full((LANES, G_KW)), _full((1, G_KW))],
        out_specs=[row(w) for w, _ in outs],
        out_shape=[jax.ShapeDtypeStruct((n, w), dt) for w, dt in outs],
        compiler_params=_params("parallel"),
        name="mixer_in",
    )(x, g_mix.reshape(1, -1), w_cat, jnp.tile(g_q, A_HEADS).reshape(1, -1), jnp.tile(g_k, A_HEADS).reshape(1, -1),
      _seg_avg_matrix(A_WIDTH, A_HEAD_DIM), wg2, b_gate2.reshape(1, -1))


def _attend(q, segments):
    tq = q.shape[0]
    lane = lax.broadcasted_iota(jnp.int32, (tq, LANES), 1)
    outs = []
    for pair in range(A_HEADS // 2):
        sl = slice(pair * LANES, (pair + 1) * LANES)
        q2 = q[:, sl]
        halves = []
        for half in range(2):
            head = 2 * pair + half
            mine = (lane < A_HEAD_DIM) if half == 0 else (lane >= A_HEAD_DIM)
            qm = jnp.where(mine, q2, jnp.zeros_like(q2))
            scores = [lax.dot_general(qm, k[:, sl], (((1,), (1,)), ((), ())), preferred_element_type=F32) + bias(head)
                      for k, _, bias in segments]
            m = functools.reduce(jnp.maximum, [jnp.max(s, axis=-1, keepdims=True) for s in scores])
            es = [jnp.exp2(s - m) for s in scores]
            inv = 1.0 / functools.reduce(jnp.add, [jnp.sum(e, axis=-1, keepdims=True) for e in es])
            o = functools.reduce(jnp.add, [
                jnp.dot(e.astype(BF16), v[:, sl], preferred_element_type=F32)
                for e, (_, v, _) in zip(es, segments)])
            halves.append(o * inv)
        outs.append(jnp.where(lane < A_HEAD_DIM, halves[0], halves[1]))
    return jnp.concatenate(outs, axis=1)


ATT_GROUP = 4
ATT_ROWS = ATT_GROUP * CHUNK
ATT_WINDOW = (N_PAST_CHUNKS + ATT_GROUP) * CHUNK
ATT_CASES = -(-N_PAST_CHUNKS // ATT_GROUP) + 1
BIAS_WRAP = 1024


def _attn_prompt_kernel(q_ref, k_ref, v_ref, tb_ref, o_ref, bias_ref):
    g = pl.program_id(1)
    first = jnp.maximum(g * ATT_GROUP - N_PAST_CHUNKS, 0)
    start = pl.multiple_of(first * CHUNK, CHUNK)

    for case in range(ATT_CASES):
        @pl.when(g == case)
        def _(case=case):
            off = min(case * ATT_GROUP, N_PAST_CHUNKS) * CHUNK
            q_chunk = (off + lax.broadcasted_iota(jnp.int32, (ATT_ROWS, ATT_WINDOW), 0)) // CHUNK
            k_chunk = lax.broadcasted_iota(jnp.int32, (ATT_ROWS, ATT_WINDOW), 1) // CHUNK
            valid = (k_chunk <= q_chunk) & (k_chunk >= q_chunk - N_PAST_CHUNKS)
            for head in range(A_HEADS):
                table = jnp.broadcast_to(tb_ref[head:head + 1, :], (ATT_ROWS, BIAS_WRAP))
                rolled = pltpu.roll(table, off, 1, stride=1, stride_axis=0)
                bias_ref[head] = jnp.where(valid, rolled[:, :ATT_WINDOW], NEG)

    kb = k_ref[0, pl.ds(start, ATT_WINDOW), :]
    vb = v_ref[0, pl.ds(start, ATT_WINDOW), :]
    o_ref[...] = _attend(q_ref[...], [(kb, vb, lambda head: bias_ref[head])]).astype(o_ref.dtype)


def _wrapped_bias_table(rel_bias):
    u = jnp.arange(BIAS_WRAP)
    dist = jnp.where(u < CHUNK, -u, BIAS_WRAP - u)
    return rel_bias[:, jnp.clip(dist, -REL_CLIP, REL_CLIP) + REL_CLIP].astype(F32) * LOG2E


def _attn_prompt(qa, ka, va, rel_bias, batch, seq):
    ng = seq // ATT_ROWS
    assert seq % ATT_ROWS == 0 and ng >= ATT_CASES and seq >= ATT_WINDOW
    kv = pl.BlockSpec((1, seq, A_WIDTH), lambda b, g: (b, 0, 0))
    return pl.pallas_call(
        _attn_prompt_kernel,
        grid=(batch, ng),
        in_specs=[pl.BlockSpec((ATT_ROWS, A_WIDTH), lambda b, g: (b * ng + g, 0)), kv, kv,
                  _full((A_HEADS, BIAS_WRAP))],
        out_specs=pl.BlockSpec((ATT_ROWS, A_WIDTH), lambda b, g: (b * ng + g, 0)),
        out_shape=jax.ShapeDtypeStruct((batch * seq, A_WIDTH), BF16),
        scratch_shapes=[pltpu.VMEM((A_HEADS, ATT_ROWS, ATT_WINDOW), F32)],
        compiler_params=_params("arbitrary", "arbitrary"),
        name="attn_prompt",
    )(qa, ka.reshape(batch, seq, A_WIDTH), va.reshape(batch, seq, A_WIDTH), _wrapped_bias_table(rel_bias))


SAMPLE_STREAMS = 4


def _attn_sample_kernel(q_ref, kn_ref, vn_ref, kc_ref, vc_ref, tb_ref, o_ref, *, t, first_valid):
    r = kc_ref.shape[1]
    valid_c = lax.broadcasted_iota(jnp.int32, (t, r), 1) >= first_valid
    bias_c, bias_n = [], []
    for head in range(A_HEADS):
        table = jnp.broadcast_to(tb_ref[head:head + 1, :], (t, BIAS_WRAP))
        bias_c.append(jnp.where(valid_c, pltpu.roll(table, r % BIAS_WRAP, 1, stride=1, stride_axis=0)[:, :r], NEG))
        bias_n.append(pltpu.roll(table, 0, 1, stride=1, stride_axis=0)[:, :t])
    for stream in range(kc_ref.shape[0]):
        rows = slice(stream * t, (stream + 1) * t)
        segs = [(kc_ref[stream].astype(BF16), vc_ref[stream].astype(BF16), lambda head: bias_c[head]),
                (kn_ref[rows, :], vn_ref[rows, :], lambda head: bias_n[head])]
        o_ref[rows, :] = _attend(q_ref[rows, :], segs).astype(o_ref.dtype)


def _attn_sample(qa, ka, va, cache_k, cache_v, rel_bias, batch, t, past_len):
    r = cache_k.shape[1]
    q_chunk = past_len // CHUNK
    assert (past_len + t - 1) // CHUNK == q_chunk and r <= BAND
    first_valid = max(0, (q_chunk - N_PAST_CHUNKS) * CHUNK - (past_len - r))
    sb = math.gcd(batch, SAMPLE_STREAMS)
    tok = pl.BlockSpec((sb * t, A_WIDTH), lambda b: (b, 0))
    cache = pl.BlockSpec((sb, r, A_WIDTH), lambda b: (b, 0, 0))
    return pl.pallas_call(
        functools.partial(_attn_sample_kernel, t=t, first_valid=first_valid),
        grid=(batch // sb,),
        in_specs=[tok, tok, tok, cache, cache, _full((A_HEADS, BIAS_WRAP))],
        out_specs=tok,
        out_shape=jax.ShapeDtypeStruct((batch * t, A_WIDTH), BF16),
        compiler_params=_params("parallel"),
        name="attn_sample",
    )(qa, ka, va, cache_k.reshape(batch, r, A_WIDTH), cache_v.reshape(batch, r, A_WIDTH),
      _wrapped_bias_table(rel_bias))


def _gla_kernel(q_ref, k_ref, v_ref, la_ref, *rest):
    s0_ref = rest[0] if len(rest) == 7 else None
    tri_ref, expand_ref, bdmask_ref, o_ref, sfin_ref, state_ref = rest[-6:]
    tb = q_ref.shape[0]
    nb = tb // G_BLOCK

    @pl.when(pl.program_id(1) == 0)
    def _():
        state_ref[...] = jnp.zeros_like(state_ref) if s0_ref is None else s0_ref[0]

    q = q_ref[...]
    k = k_ref[...]
    v = v_ref[...]
    cum = jnp.dot(tri_ref[...], la_ref[...], precision=HIGHEST, preferred_element_type=F32)
    q3 = q.reshape(nb, G_BLOCK, G_KW)
    k3 = k.reshape(nb, G_BLOCK, G_KW)
    v3 = v.reshape(nb, G_BLOCK, G_WIDTH)
    cum3 = cum.reshape(nb, G_BLOCK, G_KW)
    row = lax.broadcasted_iota(jnp.int32, (nb, G_BLOCK, G_KW), 1)
    expand = expand_ref[...]

    intra = jnp.zeros((nb, G_BLOCK, G_WIDTH), F32)
    for s in range(G_BLOCK):
        k_s = jnp.broadcast_to(k3[:, s:s + 1, :], (nb, G_BLOCK, G_KW))
        c_s = jnp.broadcast_to(cum3[:, s:s + 1, :], (nb, G_BLOCK, G_KW))
        v_s = jnp.broadcast_to(v3[:, s:s + 1, :], (nb, G_BLOCK, G_WIDTH))
        prod = q3 * k_s * jnp.exp(jnp.where(row >= s, cum3 - c_s, -jnp.inf))
        att = jnp.dot(prod.reshape(tb, G_KW).astype(BF16), expand, preferred_element_type=F32)
        intra = intra + att.reshape(nb, G_BLOCK, G_WIDTH) * v_s
    o_ref[...] = intra.reshape(tb, G_WIDTH)

    last3 = jnp.broadcast_to(cum3[:, G_BLOCK - 1:G_BLOCK, :], (nb, G_BLOCK, G_KW))
    q_dec = (q3 * jnp.exp(cum3)).reshape(tb, G_KW).astype(BF16)
    k_dec = (k3 * jnp.exp(last3 - cum3)).reshape(tb, G_KW).astype(BF16)
    carry = jnp.exp(last3).reshape(tb, G_KW)
    vb = v.astype(BF16)
    bdmask = bdmask_ref[...]
    for j in range(nb):
        rows = slice(j * G_BLOCK, (j + 1) * G_BLOCK)
        state = state_ref[...]
        o_ref[rows, :] += lax.dot_general(q_dec[rows], state.astype(BF16), (((1,), (1,)), ((), ())),
                                          preferred_element_type=F32)
        upd = lax.dot_general(vb[rows], k_dec[rows], (((0,), (0,)), ((), ())), preferred_element_type=F32)
        state_ref[...] = carry[j * G_BLOCK:j * G_BLOCK + 1, :] * state + upd * bdmask

    @pl.when(pl.program_id(1) == pl.num_programs(1) - 1)
    def _():
        sfin_ref[0] = state_ref[...]


def _gla(qg, kg, vg, la, s0, batch, t, tile):
    nt = t // tile
    hsel = jnp.arange(G_KW) // G_DK
    vsel = jnp.arange(G_WIDTH) // G_DV
    same = hsel[:, None] == vsel[None, :]
    expand = same.astype(BF16)
    bdmask = same.T.astype(F32)
    state_in = []
    if s0 is not None:
        s0_t = jnp.swapaxes(s0, 2, 3).reshape(batch, G_WIDTH, G_DK)
        state_in = [jnp.where(same.T[None], jnp.tile(s0_t, (1, 1, G_HEADS)), 0.0)]
    rr = jnp.arange(tile)
    tri = ((rr[:, None] // G_BLOCK == rr[None, :] // G_BLOCK) & (rr[None, :] <= rr[:, None])).astype(F32)
    tok = lambda w: pl.BlockSpec((tile, w), lambda b, i: (b * nt + i, 0))
    st = pl.BlockSpec((1, G_WIDTH, G_KW), lambda b, i: (b, 0, 0))
    o, s_bd = pl.pallas_call(
        _gla_kernel,
        grid=(batch, nt),
        in_specs=[tok(G_KW), tok(G_KW), tok(G_WIDTH), tok(G_KW)] + [st] * len(state_in)
        + [_full((tile, tile)), _full((G_KW, G_WIDTH)), _full((G_WIDTH, G_KW))],
        out_specs=[tok(G_WIDTH), st],
        out_shape=[jax.ShapeDtypeStruct((batch * t, G_WIDTH), F32),
                   jax.ShapeDtypeStruct((batch, G_WIDTH, G_KW), F32)],
        scratch_shapes=[pltpu.VMEM((G_WIDTH, G_KW), F32)],
        compiler_params=_params("parallel", "arbitrary"),
        name="gla",
    )(qg, kg, vg, la, *state_in, tri, expand, bdmask)
    s_fin = jnp.stack([s_bd[:, h * G_DV:(h + 1) * G_DV, h * G_DK:(h + 1) * G_DK] for h in range(G_HEADS)], axis=1)
    return o, jnp.swapaxes(s_fin, 2, 3)


def _mixer_out_kernel(x_ref, oa_ref, og_ref, sr_ref, gg_ref, avg_ref, wa_ref, wg_ref, y_ref):
    og = og_ref[...]
    og = og * lax.rsqrt(_seg_mean_sq(og, avg_ref[...]) + EPS) * gg_ref[...] * sr_ref[...]
    y = jnp.dot(oa_ref[...], wa_ref[...], preferred_element_type=F32)
    y = y + jnp.dot(og.astype(BF16), wg_ref[...], preferred_element_type=F32)
    y_ref[...] = x_ref[...] + y


def _mixer_out(x, oa, og, sr, g_gla_out, w_out):
    n = x.shape[0]
    tm = TOKEN_TILE
    row = lambda w: pl.BlockSpec((tm, w), lambda i: (i, 0))
    w_out = w_out.astype(BF16)
    return pl.pallas_call(
        _mixer_out_kernel,
        grid=(n // tm,),
        in_specs=[row(D_MODEL), row(A_WIDTH), row(G_WIDTH), row(G_WIDTH), _full((1, G_WIDTH)),
                  _full((G_WIDTH, G_WIDTH)), _full((A_WIDTH, D_MODEL)), _full((G_WIDTH, D_MODEL))],
        out_specs=row(D_MODEL),
        out_shape=jax.ShapeDtypeStruct((n, D_MODEL), F32),
        compiler_params=_params("parallel"),
        name="mixer_out",
    )(x, oa, og, sr, jnp.tile(g_gla_out, G_HEADS).reshape(1, -1), _seg_avg_matrix(G_WIDTH, G_DV),
      w_out[:A_WIDTH], w_out[A_WIDTH:])


def _head_rms(z, g_ref, scale=1.0):
    outs = []
    for head in range(X_HEADS):
        sl = slice(head * X_HEAD_DIM, (head + 1) * X_HEAD_DIM)
        zh = z[:, sl]
        outs.append(zh * (lax.rsqrt(jnp.mean(zh * zh, axis=-1, keepdims=True) + EPS) * scale) * g_ref[:, sl])
    return outs


def _mem_kv_kernel(m_ref, gm_ref, wk_ref, wv_ref, gk_ref, k_ref, v_ref, kb_ref, vb_ref):
    m = _rms(m_ref[...], gm_ref[...]).astype(BF16)
    k = _head_rms(jnp.dot(m, wk_ref[...], preferred_element_type=F32), gk_ref)
    v = jnp.dot(m, wv_ref[...], preferred_element_type=F32)
    for head in range(X_HEADS):
        k_ref[:, head, :] = k[head]
        v_ref[:, head, :] = v[:, head * X_HEAD_DIM:(head + 1) * X_HEAD_DIM]
    kb_ref[...] = jnp.concatenate(k, axis=1).astype(BF16)
    vb_ref[...] = v.astype(BF16)


def _mem_kv(mem, g_mem, w_ck, w_cv, g_ck):
    batch = mem.shape[0]
    sq = _full((D_MODEL, D_MODEL))
    vec = _full((1, D_MODEL))
    out = pl.BlockSpec((None, N_MEM, X_HEADS, X_HEAD_DIM), lambda i: (i, 0, 0, 0))
    flat = pl.BlockSpec((None, N_MEM, D_MODEL), lambda i: (i, 0, 0))
    return pl.pallas_call(
        _mem_kv_kernel,
        grid=(batch,),
        in_specs=[flat, vec, sq, sq, vec],
        out_specs=[out, out, flat, flat],
        out_shape=[jax.ShapeDtypeStruct((batch, N_MEM, X_HEADS, X_HEAD_DIM), F32)] * 2
        + [jax.ShapeDtypeStruct((batch, N_MEM, D_MODEL), BF16)] * 2,
        compiler_params=_params("parallel"),
        name="mem_kv",
    )(mem, g_mem.reshape(1, -1), w_ck.astype(BF16), w_cv.astype(BF16), jnp.tile(g_ck, X_HEADS).reshape(1, -1))


def _mixer_out_cross_kernel(x_ref, oa_ref, og_ref, sr_ref, gg_ref, avg_ref, wa_ref, wg_ref,
                            gc_ref, wq_ref, gq_ref, mk_ref, mv_ref, wo_ref, y_ref):
    og = og_ref[...]
    og = og * lax.rsqrt(_seg_mean_sq(og, avg_ref[...]) + EPS) * gg_ref[...] * sr_ref[...]
    x = x_ref[...] + jnp.dot(oa_ref[...], wa_ref[...], preferred_element_type=F32)
    x = x + jnp.dot(og.astype(BF16), wg_ref[...], preferred_element_type=F32)
    h = _rms(x, gc_ref[...]).astype(BF16)
    q = _head_rms(jnp.dot(h, wq_ref[...], preferred_element_type=F32), gq_ref, X_HEAD_DIM ** -0.5)
    outs = []
    for head in range(X_HEADS):
        sl = slice(head * X_HEAD_DIM, (head + 1) * X_HEAD_DIM)
        s = lax.dot_general(q[head].astype(BF16), mk_ref[:, sl], (((1,), (1,)), ((), ())),
                            preferred_element_type=F32)
        e = jnp.exp(s - jnp.max(s, axis=-1, keepdims=True))
        p = (e * (1.0 / jnp.sum(e, axis=-1, keepdims=True))).astype(BF16)
        outs.append(jnp.dot(p, mv_ref[:, sl], preferred_element_type=F32))
    o = jnp.concatenate(outs, axis=1).astype(BF16)
    y_ref[...] = x + jnp.dot(o, wo_ref[...], preferred_element_type=F32)


def _cross_streams_kernel(x_ref, gc_ref, wq_ref, gq_ref, mk_ref, mv_ref, wo_ref, y_ref, q_scr, o_scr, *, t):
    b = pl.program_id(0)

    @pl.when(b == 0)
    def _():
        h = _rms(x_ref[...], gc_ref[...]).astype(BF16)
        q = _head_rms(jnp.dot(h, wq_ref[...], preferred_element_type=F32), gq_ref, X_HEAD_DIM ** -0.5)
        q_scr[...] = jnp.concatenate(q, axis=1).astype(BF16)

    rows = pl.ds(pl.multiple_of(b * t, t), t)
    q = q_scr[rows, :]
    outs = []
    for head in range(X_HEADS):
        sl = slice(head * X_HEAD_DIM, (head + 1) * X_HEAD_DIM)
        s = lax.dot_general(q[:, sl], mk_ref[0, :, sl].astype(BF16), (((1,), (1,)), ((), ())),
                            preferred_element_type=F32)
        e = jnp.exp(s - jnp.max(s, axis=-1, keepdims=True))
        p = (e * (1.0 / jnp.sum(e, axis=-1, keepdims=True))).astype(BF16)
        outs.append(jnp.dot(p, mv_ref[0, :, sl].astype(BF16), preferred_element_type=F32))
    o_scr[rows, :] = jnp.concatenate(outs, axis=1).astype(BF16)

    @pl.when(b == pl.num_programs(0) - 1)
    def _():
        y_ref[...] = x_ref[...] + jnp.dot(o_scr[...], wo_ref[...], preferred_element_type=F32)


def _cross_streams(x, mem_k, mem_v, g_cross, w_cq, g_cq, w_co, batch, t):
    n = batch * t
    whole = _full((n, D_MODEL))
    mem = pl.BlockSpec((1, N_MEM, D_MODEL), lambda b: (b, 0, 0))
    sq = _full((D_MODEL, D_MODEL))
    vec = _full((1, D_MODEL))
    return pl.pallas_call(
        functools.partial(_cross_streams_kernel, t=t),
        grid=(batch,),
        in_specs=[whole, vec, sq, vec, mem, mem, sq],
        out_specs=whole,
        out_shape=jax.ShapeDtypeStruct((n, D_MODEL), F32),
        scratch_shapes=[pltpu.VMEM((n, D_MODEL), BF16), pltpu.VMEM((n, D_MODEL), BF16)],
        compiler_params=_params("arbitrary"),
        name="cross_streams",
    )(x, g_cross.reshape(1, -1), w_cq.astype(BF16), jnp.tile(g_cq, X_HEADS).reshape(1, -1),
      mem_k, mem_v, w_co.astype(BF16))


def _mixer_out_cross(x, oa, og, sr, g_gla_out, w_out, mem_k, mem_v, g_cross, w_cq, g_cq, w_co, batch, t, tile):
    nt = t // tile
    row = lambda w: pl.BlockSpec((tile, w), lambda b, i: (b * nt + i, 0))
    mem = pl.BlockSpec((None, N_MEM, D_MODEL), lambda b, i: (b, 0, 0))
    sq = _full((D_MODEL, D_MODEL))
    vec = _full((1, D_MODEL))
    w_out = w_out.astype(BF16)
    return pl.pallas_call(
        _mixer_out_cross_kernel,
        grid=(batch, nt),
        in_specs=[row(D_MODEL), row(A_WIDTH), row(G_WIDTH), row(G_WIDTH), _full((1, G_WIDTH)),
                  _full((G_WIDTH, G_WIDTH)), _full((A_WIDTH, D_MODEL)), _full((G_WIDTH, D_MODEL)),
                  vec, sq, vec, mem, mem, sq],
        out_specs=row(D_MODEL),
        out_shape=jax.ShapeDtypeStruct((batch * t, D_MODEL), F32),
        compiler_params=_params("parallel", "parallel"),
        name="mixer_out_cross",
    )(x, oa, og, sr, jnp.tile(g_gla_out, G_HEADS).reshape(1, -1), _seg_avg_matrix(G_WIDTH, G_DV),
      w_out[:A_WIDTH], w_out[A_WIDTH:], g_cross.reshape(1, -1), w_cq.astype(BF16),
      jnp.tile(g_cq, X_HEADS).reshape(1, -1), mem_k, mem_v, w_co.astype(BF16))


CAND_ROWS = 64


def _sorting_network(n):
    pairs = []
    p = 1
    while p < n:
        k = p
        while k >= 1:
            for j in range(k % p, n - k, 2 * k):
                for i in range(min(k, n - j - k)):
                    if (i + j) // (2 * p) == (i + j + k) // (2 * p):
                        pairs.append((i + j, i + j + k))
            k //= 2
        p *= 2
    return pairs


def _top_values_tiled(s, count):
    nparts = s.shape[0] // SUBLANES
    parts = [s[SUBLANES * k:SUBLANES * (k + 1), :] for k in range(nparts)]
    for i, j in _sorting_network(nparts):
        parts[i], parts[j] = jnp.maximum(parts[i], parts[j]), jnp.minimum(parts[i], parts[j])
    which = lax.broadcasted_iota(jnp.int32, parts[0].shape, 0).astype(F32)
    exhausted = jnp.full(parts[0].shape, -jnp.inf, F32)
    out = []
    for r in range(count):
        mx = jnp.max(parts[0], axis=0, keepdims=True)
        first = jnp.min(jnp.where(parts[0] == mx, which, float(SUBLANES)), axis=0, keepdims=True)
        pop = which == first
        out.append(mx)
        for k in range(min(count - r - 1, nparts)):
            parts[k] = jnp.where(pop, parts[k + 1] if k + 1 < nparts else exhausted, parts[k])
    return jnp.concatenate(out, axis=0)


def _count_leading(desc, x, cmp):
    assert P_TOPK == 16
    row = lambda b: desc[b:b + 1, :]
    pick = lambda c, hi, lo: jnp.where(c, hi, lo)
    c8 = cmp(row(7), x)
    c4 = cmp(pick(c8, row(11), row(3)), x)
    c2 = cmp(pick(c8, pick(c4, row(13), row(9)), pick(c4, row(5), row(1))), x)
    c1 = cmp(pick(c8, pick(c4, pick(c2, row(14), row(12)), pick(c2, row(10), row(8))),
                  pick(c4, pick(c2, row(6), row(4)), pick(c2, row(2), row(0)))), x)
    c_last = cmp(row(15), x)
    count = pick(c8, 8.0, 0.0) + pick(c4, 4.0, 0.0) + pick(c2, 2.0, 0.0) + pick(c1, 1.0, 0.0)
    return count + pick(c_last, 1.0, 0.0)


def _route_kernel(x_ref, gf_ref, whi_ref, wlo_ref, sk1_ref, sk2_ref, ht_ref, r2_ref, e2_ref, n1_ref, c1_ref):
    hf = _rms(x_ref[...], gf_ref[...])
    ht_ref[...] = jnp.transpose(hf).astype(BF16)
    h_hi = hf.astype(BF16)
    h_lo = (hf - h_hi.astype(F32)).astype(BF16)
    qry = (jnp.dot(h_hi, whi_ref[...], preferred_element_type=F32)
           + jnp.dot(h_hi, wlo_ref[...], preferred_element_type=F32)
           + jnp.dot(h_lo, whi_ref[...], preferred_element_type=F32))
    nt = (((1,), (1,)), ((), ()))
    nk = P_TOPK + 1
    t = qry.shape[0]
    for head in range(P_HEADS):
        q1 = qry[:, head * P_QDIM:head * P_QDIM + P_HALF]
        q2 = qry[:, head * P_QDIM + P_HALF:(head + 1) * P_QDIM]
        s1 = lax.dot_general(sk1_ref[...], q1, nt, precision=HIGHEST, preferred_element_type=F32)
        s2 = lax.dot_general(sk2_ref[...], q2, nt, precision=HIGHEST, preferred_element_type=F32)
        v1 = _top_values_tiled(s1, nk)
        v2 = _top_values_tiled(s2, nk)
        slabs = [v1[a:a + 1, :] + v2[:nk // (a + 1), :] for a in range(nk)]
        pad = CAND_ROWS - sum(sl.shape[0] for sl in slabs)
        cand = jnp.concatenate(slabs + [jnp.full((pad, t), -jnp.inf, F32)], axis=0)
        top = _top_values_tiled(cand, nk)
        m1 = v1[0:1, :]
        m2 = v2[0:1, :]
        z = jnp.sum(jnp.exp(top[:P_TOPK, :] - (m1 + m2)), axis=0, keepdims=True)
        tau = 0.5 * (top[P_TOPK - 1:P_TOPK, :] + top[P_TOPK:nk, :])
        thr = tau - s1
        n1 = _count_leading(v2, thr, jnp.greater_equal)
        rank2 = _count_leading(v2, s2, jnp.greater)
        r2_ref[head] = rank2.astype(BF16)
        e2_ref[head] = jnp.exp(s2 - m2).astype(BF16)
        n1_ref[head] = n1
        c1_ref[head] = jnp.exp(s1 - m1) * (1.0 / z)


def _route(x, g_ffn, w_pq, sub_keys1, sub_keys2):
    n = x.shape[0]
    w_hi = w_pq.astype(BF16)
    w_lo = (w_pq - w_hi.astype(F32)).astype(BF16)
    tm = ROUTE_TILE
    keys = pl.BlockSpec((P_HEADS, P_NKEYS, tm), lambda i: (0, 0, i))
    tab = lambda dt: jax.ShapeDtypeStruct((P_HEADS, P_NKEYS, n), dt)
    return pl.pallas_call(
        _route_kernel,
        grid=(n // tm,),
        in_specs=[pl.BlockSpec((tm, D_MODEL), lambda i: (i, 0)), _full((1, D_MODEL)),
                  _full((D_MODEL, P_HEADS * P_QDIM)), _full((D_MODEL, P_HEADS * P_QDIM)),
                  _full((P_NKEYS, P_HALF)), _full((P_NKEYS, P_HALF))],
        out_specs=[pl.BlockSpec((D_MODEL, tm), lambda i: (0, i)), keys, keys, keys, keys],
        out_shape=[jax.ShapeDtypeStruct((D_MODEL, n), BF16), tab(BF16), tab(BF16), tab(F32), tab(F32)],
        compiler_params=_params("parallel"),
        name="peer_route",
    )(x, g_ffn.reshape(1, -1), w_hi, w_lo, sub_keys1, sub_keys2)


def _peer_kernel(ht_ref, x_ref, u_ref, vt_ref, r2_ref, e2_ref, n1_ref, c1_ref, y_ref, acc_ref):
    j = pl.program_id(1)

    @pl.when(j == 0)
    def _():
        acc_ref[...] = jnp.zeros_like(acc_ref)

    t = ht_ref.shape[1]
    n_i1 = PEER_EXPERT_TILE // P_NKEYS
    zero = jnp.zeros((), BF16)
    def token_row(row):
        return jnp.tile(jnp.broadcast_to(row, (2 * SUBLANES, t)).astype(BF16), (P_NKEYS // (2 * SUBLANES), 1))

    a = jnp.dot(u_ref[...], ht_ref[...], preferred_element_type=F32)
    ab = a.astype(BF16)
    act = ab * (1.0 + lax.erf(ab * (1.0 / math.sqrt(2.0))))
    gates = []
    for r in range(n_i1):
        g = jnp.zeros((P_NKEYS, t), BF16)
        for head in range(P_HEADS):
            n1 = token_row(n1_ref[head, pl.ds(j * n_i1 + r, 1), :])
            c1 = token_row(0.5 * c1_ref[head, pl.ds(j * n_i1 + r, 1), :])
            g = g + jnp.where(r2_ref[head] < n1, e2_ref[head], zero) * c1
        gates.append(g)
    ga = act * jnp.concatenate(gates, axis=0)
    acc_ref[...] += jnp.dot(vt_ref[0], ga, preferred_element_type=F32)

    @pl.when(j == pl.num_programs(1) - 1)
    def _():
        y_ref[...] = x_ref[...] + jnp.transpose(acc_ref[...])


def _peer(ht, x, r2, e2, n1, c1, u, vt, tt):
    n = x.shape[0]
    et = PEER_EXPERT_TILE
    ne = u.shape[0] // et
    once = pl.Buffered(1)
    keys = pl.BlockSpec((P_HEADS, P_NKEYS, tt), lambda i, j: (0, 0, i))
    return pl.pallas_call(
        _peer_kernel,
        grid=(n // tt, ne),
        in_specs=[pl.BlockSpec((D_MODEL, tt), lambda i, j: (0, i)),
                  pl.BlockSpec((tt, D_MODEL), lambda i, j: (i, 0), pipeline_mode=once),
                  pl.BlockSpec((et, D_MODEL), lambda i, j: (j, 0)),
                  pl.BlockSpec((1, D_MODEL, et), lambda i, j: (j, 0, 0)),
                  keys, keys, keys, keys],
        out_specs=pl.BlockSpec((tt, D_MODEL), lambda i, j: (i, 0), pipeline_mode=once),
        out_shape=jax.ShapeDtypeStruct((n, D_MODEL), F32),
        scratch_shapes=[pltpu.VMEM((D_MODEL, tt), F32)],
        compiler_params=_params("parallel", "arbitrary"),
        name="peer_experts",
    )(ht, x, u, vt, r2, e2, n1, c1)


def _layer(xp, xs, cache_k, cache_v, state, cmem_k, cmem_v, mem_prompt, past_len,
           g_mix, w_in, g_q, g_k, rel_bias, w_gate2, b_gate2, g_gla_out, w_out,
           g_cross, g_mem, w_cq, w_ck, w_cv, g_cq, g_ck, w_co, g_ffn, w_pq, sub_keys1, sub_keys2, expert_u, expert_v):
    bp, sp, _ = xp.shape
    bs, ts, _ = xs.shape
    xp = xp.reshape(bp * sp, D_MODEL)
    xs = xs.reshape(bs * ts, D_MODEL)
    u = expert_u.astype(BF16)
    ne = expert_v.shape[0] // PEER_EXPERT_TILE
    vt = jnp.swapaxes(expert_v.reshape(ne, PEER_EXPERT_TILE, D_MODEL), 1, 2).astype(BF16)
    heads = lambda a, b, t: a.reshape(b, t, A_HEADS, A_HEAD_DIM)

    def peer(x, peer_tile):
        ht, r2, e2, n1, c1 = _route(x, g_ffn, w_pq, sub_keys1, sub_keys2)
        return _peer(ht, x, r2, e2, n1, c1, u, vt, peer_tile)

    qa, ka_f, ka_b, va_f, va_b, qg, kg, vg, la, sr = _mixer_in(xp, g_mix, w_in, g_q, g_k, w_gate2, b_gate2)
    oa = _attn_prompt(qa, ka_b, va_b, rel_bias, bp, sp)
    og, gla_p = _gla(qg, kg, vg, la, None, bp, sp, GLA_TILE)
    mk, mv, mk_b, mv_b = _mem_kv(mem_prompt, g_mem, w_ck, w_cv, g_ck)
    xp = peer(_mixer_out_cross(xp, oa, og, sr, g_gla_out, w_out, mk_b, mv_b, g_cross, w_cq, g_cq, w_co,
                               bp, sp, TOKEN_TILE), PEER_TOKEN_TILE)
    keep = min(BAND, sp)
    outs_p = (heads(ka_f, bp, sp)[:, -keep:], heads(va_f, bp, sp)[:, -keep:], gla_p, mk, mv)

    qa, ka_f, ka_b, va_f, va_b, qg, kg, vg, la, sr = _mixer_in(xs, g_mix, w_in, g_q, g_k, w_gate2, b_gate2)
    oa = _attn_sample(qa, ka_b, va_b, cache_k, cache_v, rel_bias, bs, ts, past_len)
    og, gla_s = _gla(qg, kg, vg, la, state, bs, ts, ts)
    xs = _mixer_out(xs, oa, og, sr, g_gla_out, w_out)
    xs = _cross_streams(xs, cmem_k.reshape(bs, N_MEM, D_MODEL), cmem_v.reshape(bs, N_MEM, D_MODEL),
                        g_cross, w_cq, g_cq, w_co, bs, ts)
    xs = peer(xs, bs * ts)
    outs_s = (heads(ka_f, bs, ts), heads(va_f, bs, ts), gla_s)

    return (xp.reshape(bp, sp, D_MODEL), xs.reshape(bs, ts, D_MODEL)) + outs_p + outs_s


def kernel(x_prompt, x_sample, cache_att_k, cache_att_v, state_gla, cache_mem_k, cache_mem_v, mem_prompt, g_mix, w_in, g_q, g_k, rel_bias, w_gate2, b_gate2, g_gla_out, w_out, g_cross, g_mem, w_cq, w_ck, w_cv, g_cq, g_ck, w_co, g_ffn, w_pq, sub_keys1, sub_keys2, expert_u, expert_v):
    depth = w_in.shape[0]
    past_len = 4096
    xp, xs = x_prompt, x_sample
    per_layer = []
    for l in range(depth):
        outs = _layer(xp, xs, cache_att_k[l], cache_att_v[l], state_gla[l], cache_mem_k[l], cache_mem_v[l],
                      mem_prompt, past_len,
                      g_mix[l], w_in[l], g_q[l], g_k[l], rel_bias[l], w_gate2[l], b_gate2[l], g_gla_out[l], w_out[l],
                      g_cross[l], g_mem[l], w_cq[l], w_ck[l], w_cv[l], g_cq[l], g_ck[l], w_co[l],
                      g_ffn[l], w_pq[l], sub_keys1[l], sub_keys2[l], expert_u[l], expert_v[l])
        xp, xs = outs[0], outs[1]
        per_layer.append(outs[2:])
    stacked = [jnp.stack([p[i] for p in per_layer]) for i in range(8)]
    return (xp, xs) + tuple(stacked)
```

```python
import functools
import math

import jax
import jax.numpy as jnp
from jax import lax
from jax.experimental import pallas as pl
from jax.experimental.pallas import tpu as pltpu

F32 = jnp.float32
BF16 = jnp.bfloat16
HIGHEST = lax.Precision.HIGHEST

D_MODEL = 1024
CHUNK = 64
N_PAST_CHUNKS = 8
BAND = (N_PAST_CHUNKS + 1) * CHUNK
A_HEADS = 8
A_HEAD_DIM = 64
A_WIDTH = A_HEADS * A_HEAD_DIM
REL_CLIP = 128
G_HEADS = 4
G_DK = 64
G_DV = 128
G_KW = G_HEADS * G_DK
G_WIDTH = G_HEADS * G_DV
G_GATE_RANK = 16
G_GATE_NORM = 16.0
G_BLOCK = 16
N_MEM = 256
X_HEADS = 4
X_HEAD_DIM = D_MODEL // X_HEADS
P_HEADS = 8
P_NKEYS = 128
P_QDIM = 256
P_HALF = P_QDIM // 2
P_TOPK = 16
EPS = 1e-6
NEG = -1e30
LOG2E = math.log2(math.e)

LANES = 128
SUBLANES = 8
VMEM_LIMIT = 56 * 1024 * 1024

TOKEN_TILE = 512
ROUTE_TILE = 256
GLA_TILE = 256
PEER_TOKEN_TILE = 1024
PEER_EXPERT_TILE = 1024


def _params(*sem):
    return pltpu.CompilerParams(dimension_semantics=sem, vmem_limit_bytes=VMEM_LIMIT)


def _full(shape):
    return pl.BlockSpec(shape, lambda *_: (0,) * len(shape))


def _rms(x, g):
    return x * lax.rsqrt(jnp.mean(x * x, axis=-1, keepdims=True) + EPS) * g


def _seg_mean_sq(z, seg_avg):
    return jnp.dot((z * z).astype(BF16), seg_avg, preferred_element_type=F32)


def _seg_avg_matrix(width, seg):
    r = jnp.arange(width) // seg
    return jnp.where(r[:, None] == r[None, :], 1.0 / seg, 0.0).astype(BF16)


def _mixer_in_kernel(x_ref, gmix_ref, w_ref, gq_ref, gk_ref, avg_ref, wg2_ref, bg2_ref,
                     qa_ref, kaf_ref, kab_ref, vaf_ref, vab_ref, qg_ref, kg_ref, vg_ref, la_ref, sr_ref):
    h = _rms(x_ref[...], gmix_ref[...]).astype(BF16)
    z = jnp.dot(h, w_ref[...], preferred_element_type=F32)
    o = 0
    qa = z[:, o:o + A_WIDTH]; o += A_WIDTH
    ka = z[:, o:o + A_WIDTH]; o += A_WIDTH
    va = z[:, o:o + A_WIDTH]; o += A_WIDTH
    qg = z[:, o:o + G_KW]; o += G_KW
    kg = z[:, o:o + G_KW]; o += G_KW
    vg = z[:, o:o + G_WIDTH]; o += G_WIDTH
    r = z[:, o:o + G_WIDTH]; o += G_WIDTH
    gl = z[:, o:o + LANES]
    avg = avg_ref[...]
    qa = qa * lax.rsqrt(_seg_mean_sq(qa, avg) + EPS) * gq_ref[...]
    ka = ka * lax.rsqrt(_seg_mean_sq(ka, avg) + EPS) * gk_ref[...]
    qa_ref[...] = (qa * (A_HEAD_DIM ** -0.5 * LOG2E)).astype(BF16)
    kaf_ref[...] = ka
    kab_ref[...] = ka.astype(BF16)
    vaf_ref[...] = va
    vab_ref[...] = va.astype(BF16)
    qg_ref[...] = qg * (G_DK ** -0.5)
    kg_ref[...] = kg
    vg_ref[...] = vg
    pre = jnp.dot(gl, wg2_ref[...], precision=HIGHEST, preferred_element_type=F32) + bg2_ref[...]
    la_ref[...] = (jnp.minimum(pre, 0.0) - jnp.log1p(jnp.exp(-jnp.abs(pre)))) / G_GATE_NORM
    sr_ref[...] = r * jax.nn.sigmoid(r)


def _mixer_in(x, g_mix, w_in, g_q, g_k, w_gate2, b_gate2):
    n = x.shape[0]
    c_gl = 3 * A_WIDTH + 2 * G_KW + G_WIDTH
    w_cat = jnp.concatenate(
        [w_in[:, :c_gl], w_in[:, c_gl + G_GATE_RANK:],
         jnp.pad(w_in[:, c_gl:c_gl + G_GATE_RANK], ((0, 0), (0, LANES - G_GATE_RANK)))], axis=1).astype(BF16)
    wg2 = jnp.pad(w_gate2, ((0, LANES - G_GATE_RANK), (0, 0)))
    wcols = w_cat.shape[1]
    tm = TOKEN_TILE
    row = lambda w: pl.BlockSpec((tm, w), lambda i: (i, 0))
    outs = [(A_WIDTH, BF16), (A_WIDTH, F32), (A_WIDTH, BF16), (A_WIDTH, F32), (A_WIDTH, BF16),
            (G_KW, F32), (G_KW, F32), (G_WIDTH, F32), (G_KW, F32), (G_WIDTH, F32)]
    return pl.pallas_call(
        _mixer_in_kernel,
        grid=(n // tm,),
        in_specs=[row(D_MODEL), _full((1, D_MODEL)), _full((D_MODEL, wcols)), _full((1, A_WIDTH)),
                  _full((1, A_WIDTH)), _full((A_WIDTH, A_WIDTH)), _full((LANES, G_KW)), _full((1, G_KW))],
        out_specs=[row(w) for w, _ in outs],
        out_shape=[jax.ShapeDtypeStruct((n, w), dt) for w, dt in outs],
        compiler_params=_params("parallel"),
        name="mixer_in",
    )(x, g_mix.reshape(1, -1), w_cat, jnp.tile(g_q, A_HEADS).reshape(1, -1), jnp.tile(g_k, A_HEADS).reshape(1, -1),
      _seg_avg_matrix(A_WIDTH, A_HEAD_DIM), wg2, b_gate2.reshape(1, -1))


def _attend(q, segments):
    tq = q.shape[0]
    lane = lax.broadcasted_iota(jnp.int32, (tq, LANES), 1)
    outs = []
    for pair in range(A_HEADS // 2):
        sl = slice(pair * LANES, (pair + 1) * LANES)
        q2 = q[:, sl]
        halves = []
        for half in range(2):
            head = 2 * pair + half
            mine = (lane < A_HEAD_DIM) if half == 0 else (lane >= A_HEAD_DIM)
            qm = jnp.where(mine, q2, jnp.zeros_like(q2))
            scores = [lax.dot_general(qm, k[:, sl], (((1,), (1,)), ((), ())), preferred_element_type=F32) + bias(head)
                      for k, _, bias in segments]
            m = functools.reduce(jnp.maximum, [jnp.max(s, axis=-1, keepdims=True) for s in scores])
            es = [jnp.exp2(s - m) for s in scores]
            inv = 1.0 / functools.reduce(jnp.add, [jnp.sum(e, axis=-1, keepdims=True) for e in es])
            o = functools.reduce(jnp.add, [
                jnp.dot(e.astype(BF16), v[:, sl], preferred_element_type=F32)
                for e, (_, v, _) in zip(es, segments)])
            halves.append(o * inv)
        outs.append(jnp.where(lane < A_HEAD_DIM, halves[0], halves[1]))
    return jnp.concatenate(outs, axis=1)


ATT_GROUP = 4
ATT_ROWS = ATT_GROUP * CHUNK
ATT_WINDOW = (N_PAST_CHUNKS + ATT_GROUP) * CHUNK
ATT_CASES = -(-N_PAST_CHUNKS // ATT_GROUP) + 1
BIAS_WRAP = 1024


def _attn_prompt_kernel(q_ref, k_ref, v_ref, tb_ref, o_ref, bias_ref):
    g = pl.program_id(1)
    first = jnp.maximum(g * ATT_GROUP - N_PAST_CHUNKS, 0)
    start = pl.multiple_of(first * CHUNK, CHUNK)

    for case in range(ATT_CASES):
        @pl.when(g == case)
        def _(case=case):
            off = min(case * ATT_GROUP, N_PAST_CHUNKS) * CHUNK
            q_chunk = (off + lax.broadcasted_iota(jnp.int32, (ATT_ROWS, ATT_WINDOW), 0)) // CHUNK
            k_chunk = lax.broadcasted_iota(jnp.int32, (ATT_ROWS, ATT_WINDOW), 1) // CHUNK
            valid = (k_chunk <= q_chunk) & (k_chunk >= q_chunk - N_PAST_CHUNKS)
            for head in range(A_HEADS):
                table = jnp.broadcast_to(tb_ref[head:head + 1, :], (ATT_ROWS, BIAS_WRAP))
                rolled = pltpu.roll(table, off, 1, stride=1, stride_axis=0)
                bias_ref[head] = jnp.where(valid, rolled[:, :ATT_WINDOW], NEG)

    kb = k_ref[0, pl.ds(start, ATT_WINDOW), :]
    vb = v_ref[0, pl.ds(start, ATT_WINDOW), :]
    o_ref[...] = _attend(q_ref[...], [(kb, vb, lambda head: bias_ref[head])]).astype(o_ref.dtype)


def _wrapped_bias_table(rel_bias):
    u = jnp.arange(BIAS_WRAP)
    dist = jnp.where(u < CHUNK, -u, BIAS_WRAP - u)
    return rel_bias[:, jnp.clip(dist, -REL_CLIP, REL_CLIP) + REL_CLIP].astype(F32) * LOG2E


def _attn_prompt(qa, ka, va, rel_bias, batch, seq):
    ng = seq // ATT_ROWS
    assert seq % ATT_ROWS == 0 and ng >= ATT_CASES and seq >= ATT_WINDOW
    kv = pl.BlockSpec((1, seq, A_WIDTH), lambda b, g: (b, 0, 0))
    return pl.pallas_call(
        _attn_prompt_kernel,
        grid=(batch, ng),
        in_specs=[pl.BlockSpec((ATT_ROWS, A_WIDTH), lambda b, g: (b * ng + g, 0)), kv, kv,
                  _full((A_HEADS, BIAS_WRAP))],
        out_specs=pl.BlockSpec((ATT_ROWS, A_WIDTH), lambda b, g: (b * ng + g, 0)),
        out_shape=jax.ShapeDtypeStruct((batch * seq, A_WIDTH), BF16),
        scratch_shapes=[pltpu.VMEM((A_HEADS, ATT_ROWS, ATT_WINDOW), F32)],
        compiler_params=_params("arbitrary", "arbitrary"),
        name="attn_prompt",
    )(qa, ka.reshape(batch, seq, A_WIDTH), va.reshape(batch, seq, A_WIDTH), _wrapped_bias_table(rel_bias))


SAMPLE_STREAMS = 4


def _attn_sample_kernel(q_ref, kn_ref, vn_ref, kc_ref, vc_ref, tb_ref, o_ref, *, t, first_valid):
    r = kc_ref.shape[1]
    valid_c = lax.broadcasted_iota(jnp.int32, (t, r), 1) >= first_valid
    bias_c, bias_n = [], []
    for head in range(A_HEADS):
        table = jnp.broadcast_to(tb_ref[head:head + 1, :], (t, BIAS_WRAP))
        bias_c.append(jnp.where(valid_c, pltpu.roll(table, r % BIAS_WRAP, 1, stride=1, stride_axis=0)[:, :r], NEG))
        bias_n.append(pltpu.roll(table, 0, 1, stride=1, stride_axis=0)[:, :t])
    for stream in range(kc_ref.shape[0]):
        rows = slice(stream * t, (stream + 1) * t)
        segs = [(kc_ref[stream].astype(BF16), vc_ref[stream].astype(BF16), lambda head: bias_c[head]),
                (kn_ref[rows, :], vn_ref[rows, :], lambda head: bias_n[head])]
        o_ref[rows, :] = _attend(q_ref[rows, :], segs).astype(o_ref.dtype)


def _attn_sample(qa, ka, va, cache_k, cache_v, rel_bias, batch, t, past_len):
    r = cache_k.shape[1]
    q_chunk = past_len // CHUNK
    assert (past_len + t - 1) // CHUNK == q_chunk and r <= BAND
    first_valid = max(0, (q_chunk - N_PAST_CHUNKS) * CHUNK - (past_len - r))
    sb = math.gcd(batch, SAMPLE_STREAMS)
    tok = pl.BlockSpec((sb * t, A_WIDTH), lambda b: (b, 0))
    cache = pl.BlockSpec((sb, r, A_WIDTH), lambda b: (b, 0, 0))
    return pl.pallas_call(
        functools.partial(_attn_sample_kernel, t=t, first_valid=first_valid),
        grid=(batch // sb,),
        in_specs=[tok, tok, tok, cache, cache, _full((A_HEADS, BIAS_WRAP))],
        out_specs=tok,
        out_shape=jax.ShapeDtypeStruct((batch * t, A_WIDTH), BF16),
        compiler_params=_params("parallel"),
        name="attn_sample",
    )(qa, ka, va, cache_k.reshape(batch, r, A_WIDTH), cache_v.reshape(batch, r, A_WIDTH),
      _wrapped_bias_table(rel_bias))


def _gla_kernel(q_ref, k_ref, v_ref, la_ref, *rest):
    s0_ref = rest[0] if len(rest) == 7 else None
    tri_ref, expand_ref, bdmask_ref, o_ref, sfin_ref, state_ref = rest[-6:]
    tb = q_ref.shape[0]
    nb = tb // G_BLOCK

    @pl.when(pl.program_id(1) == 0)
    def _():
        state_ref[...] = jnp.zeros_like(state_ref) if s0_ref is None else s0_ref[0]

    q = q_ref[...]
    k = k_ref[...]
    v = v_ref[...]
    cum = jnp.dot(tri_ref[...], la_ref[...], precision=HIGHEST, preferred_element_type=F32)
    q3 = q.reshape(nb, G_BLOCK, G_KW)
    k3 = k.reshape(nb, G_BLOCK, G_KW)
    v3 = v.reshape(nb, G_BLOCK, G_WIDTH)
    cum3 = cum.reshape(nb, G_BLOCK, G_KW)
    row = lax.broadcasted_iota(jnp.int32, (nb, G_BLOCK, G_KW), 1)
    expand = expand_ref[...]

    intra = jnp.zeros((nb, G_BLOCK, G_WIDTH), F32)
    for s in range(G_BLOCK):
        k_s = jnp.broadcast_to(k3[:, s:s + 1, :], (nb, G_BLOCK, G_KW))
        c_s = jnp.broadcast_to(cum3[:, s:s + 1, :], (nb, G_BLOCK, G_KW))
        v_s = jnp.broadcast_to(v3[:, s:s + 1, :], (nb, G_BLOCK, G_WIDTH))
        prod = q3 * k_s * jnp.exp(jnp.where(row >= s, cum3 - c_s, -jnp.inf))
        att = jnp.dot(prod.reshape(tb, G_KW).astype(BF16), expand, preferred_element_type=F32)
        intra = intra + att.reshape(nb, G_BLOCK, G_WIDTH) * v_s
    o_ref[...] = intra.reshape(tb, G_WIDTH)

    last3 = jnp.broadcast_to(cum3[:, G_BLOCK - 1:G_BLOCK, :], (nb, G_BLOCK, G_KW))
    q_dec = (q3 * jnp.exp(cum3)).reshape(tb, G_KW).astype(BF16)
    k_dec = (k3 * jnp.exp(last3 - cum3)).reshape(tb, G_KW).astype(BF16)
    carry = jnp.exp(last3).reshape(tb, G_KW)
    vb = v.astype(BF16)
    bdmask = bdmask_ref[...]
    for j in range(nb):
        rows = slice(j * G_BLOCK, (j + 1) * G_BLOCK)
        state = state_ref[...]
        o_ref[rows, :] += lax.dot_general(q_dec[rows], state.astype(BF16), (((1,), (1,)), ((), ())),
                                          preferred_element_type=F32)
        upd = lax.dot_general(vb[rows], k_dec[rows], (((0,), (0,)), ((), ())), preferred_element_type=F32)
        state_ref[...] = carry[j * G_BLOCK:j * G_BLOCK + 1, :] * state + upd * bdmask

    @pl.when(pl.program_id(1) == pl.num_programs(1) - 1)
    def _():
        sfin_ref[0] = state_ref[...]


def _gla(qg, kg, vg, la, s0, batch, t, tile):
    nt = t // tile
    hsel = jnp.arange(G_KW) // G_DK
    vsel = jnp.arange(G_WIDTH) // G_DV
    same = hsel[:, None] == vsel[None, :]
    expand = same.astype(BF16)
    bdmask = same.T.astype(F32)
    state_in = []
    if s0 is not None:
        s0_t = jnp.swapaxes(s0, 2, 3).reshape(batch, G_WIDTH, G_DK)
        state_in = [jnp.where(same.T[None], jnp.tile(s0_t, (1, 1, G_HEADS)), 0.0)]
    rr = jnp.arange(tile)
    tri = ((rr[:, None] // G_BLOCK == rr[None, :] // G_BLOCK) & (rr[None, :] <= rr[:, None])).astype(F32)
    tok = lambda w: pl.BlockSpec((tile, w), lambda b, i: (b * nt + i, 0))
    st = pl.BlockSpec((1, G_WIDTH, G_KW), lambda b, i: (b, 0, 0))
    o, s_bd = pl.pallas_call(
        _gla_kernel,
        grid=(batch, nt),
        in_specs=[tok(G_KW), tok(G_KW), tok(G_WIDTH), tok(G_KW)] + [st] * len(state_in)
        + [_full((tile, tile)), _full((G_KW, G_WIDTH)), _full((G_WIDTH, G_KW))],
        out_specs=[tok(G_WIDTH), st],
        out_shape=[jax.ShapeDtypeStruct((batch * t, G_WIDTH), F32),
                   jax.ShapeDtypeStruct((batch, G_WIDTH, G_KW), F32)],
        scratch_shapes=[pltpu.VMEM((G_WIDTH, G_KW), F32)],
        compiler_params=_params("parallel", "arbitrary"),
        name="gla",
    )(qg, kg, vg, la, *state_in, tri, expand, bdmask)
    s_fin = jnp.stack([s_bd[:, h * G_DV:(h + 1) * G_DV, h * G_DK:(h + 1) * G_DK] for h in range(G_HEADS)], axis=1)
    return o, jnp.swapaxes(s_fin, 2, 3)


def _mixer_out_kernel(x_ref, oa_ref, og_ref, sr_ref, gg_ref, avg_ref, wa_ref, wg_ref, y_ref):
    og = og_ref[...]
    og = og * lax.rsqrt(_seg_mean_sq(og, avg_ref[...]) + EPS) * gg_ref[...] * sr_ref[...]
    y = jnp.dot(oa_ref[...], wa_ref[...], preferred_element_type=F32)
    y = y + jnp.dot(og.astype(BF16), wg_ref[...], preferred_element_type=F32)
    y_ref[...] = x_ref[...] + y


def _mixer_out(x, oa, og, sr, g_gla_out, w_out):
    n = x.shape[0]
    tm = TOKEN_TILE
    row = lambda w: pl.BlockSpec((tm, w), lambda i: (i, 0))
    w_out = w_out.astype(BF16)
    return pl.pallas_call(
        _mixer_out_kernel,
        grid=(n // tm,),
        in_specs=[row(D_MODEL), row(A_WIDTH), row(G_WIDTH), row(G_WIDTH), _full((1, G_WIDTH)),
                  _full((G_WIDTH, G_WIDTH)), _full((A_WIDTH, D_MODEL)), _full((G_WIDTH, D_MODEL))],
        out_specs=row(D_MODEL),
        out_shape=jax.ShapeDtypeStruct((n, D_MODEL), F32),
        compiler_params=_params("parallel"),
        name="mixer_out",
    )(x, oa, og, sr, jnp.tile(g_gla_out, G_HEADS).reshape(1, -1), _seg_avg_matrix(G_WIDTH, G_DV),
      w_out[:A_WIDTH], w_out[A_WIDTH:])


def _head_rms(z, g_ref, scale=1.0):
    outs = []
    for head in range(X_HEADS):
        sl = slice(head * X_HEAD_DIM, (head + 1) * X_HEAD_DIM)
        zh = z[:, sl]
        outs.append(zh * (lax.rsqrt(jnp.mean(zh * zh, axis=-1, keepdims=True) + EPS) * scale) * g_ref[:, sl])
    return outs


def _mem_kv_kernel(m_ref, gm_ref, wk_ref, wv_ref, gk_ref, k_ref, v_ref, kb_ref, vb_ref):
    m = _rms(m_ref[...], gm_ref[...]).astype(BF16)
    k = _head_rms(jnp.dot(m, wk_ref[...], preferred_element_type=F32), gk_ref)
    v = jnp.dot(m, wv_ref[...], preferred_element_type=F32)
    for head in range(X_HEADS):
        k_ref[:, head, :] = k[head]
        v_ref[:, head, :] = v[:, head * X_HEAD_DIM:(head + 1) * X_HEAD_DIM]
    kb_ref[...] = jnp.concatenate(k, axis=1).astype(BF16)
    vb_ref[...] = v.astype(BF16)


def _mem_kv(mem, g_mem, w_ck, w_cv, g_ck):
    batch = mem.shape[0]
    sq = _full((D_MODEL, D_MODEL))
    vec = _full((1, D_MODEL))
    out = pl.BlockSpec((None, N_MEM, X_HEADS, X_HEAD_DIM), lambda i: (i, 0, 0, 0))
    flat = pl.BlockSpec((None, N_MEM, D_MODEL), lambda i: (i, 0, 0))
    return pl.pallas_call(
        _mem_kv_kernel,
        grid=(batch,),
        in_specs=[flat, vec, sq, sq, vec],
        out_specs=[out, out, flat, flat],
        out_shape=[jax.ShapeDtypeStruct((batch, N_MEM, X_HEADS, X_HEAD_DIM), F32)] * 2
        + [jax.ShapeDtypeStruct((batch, N_MEM, D_MODEL), BF16)] * 2,
        compiler_params=_params("parallel"),
        name="mem_kv",
    )(mem, g_mem.reshape(1, -1), w_ck.astype(BF16), w_cv.astype(BF16), jnp.tile(g_ck, X_HEADS).reshape(1, -1))


def _mixer_out_cross_kernel(x_ref, oa_ref, og_ref, sr_ref, gg_ref, avg_ref, wa_ref, wg_ref,
                            gc_ref, wq_ref, gq_ref, mk_ref, mv_ref, wo_ref, y_ref):
    og = og_ref[...]
    og = og * lax.rsqrt(_seg_mean_sq(og, avg_ref[...]) + EPS) * gg_ref[...] * sr_ref[...]
    x = x_ref[...] + jnp.dot(oa_ref[...], wa_ref[...], preferred_element_type=F32)
    x = x + jnp.dot(og.astype(BF16), wg_ref[...], preferred_element_type=F32)
    h = _rms(x, gc_ref[...]).astype(BF16)
    q = _head_rms(jnp.dot(h, wq_ref[...], preferred_element_type=F32), gq_ref, X_HEAD_DIM ** -0.5)
    outs = []
    for head in range(X_HEADS):
        sl = slice(head * X_HEAD_DIM, (head + 1) * X_HEAD_DIM)
        s = lax.dot_general(q[head].astype(BF16), mk_ref[:, sl], (((1,), (1,)), ((), ())),
                            preferred_element_type=F32)
        e = jnp.exp(s - jnp.max(s, axis=-1, keepdims=True))
        p = (e * (1.0 / jnp.sum(e, axis=-1, keepdims=True))).astype(BF16)
        outs.append(jnp.dot(p, mv_ref[:, sl], preferred_element_type=F32))
    o = jnp.concatenate(outs, axis=1).astype(BF16)
    y_ref[...] = x + jnp.dot(o, wo_ref[...], preferred_element_type=F32)


def _cross_streams_kernel(x_ref, gc_ref, wq_ref, gq_ref, mk_ref, mv_ref, wo_ref, y_ref, q_scr, o_scr, *, t):
    b = pl.program_id(0)

    @pl.when(b == 0)
    def _():
        h = _rms(x_ref[...], gc_ref[...]).astype(BF16)
        q = _head_rms(jnp.dot(h, wq_ref[...], preferred_element_type=F32), gq_ref, X_HEAD_DIM ** -0.5)
        q_scr[...] = jnp.concatenate(q, axis=1).astype(BF16)

    rows = pl.ds(pl.multiple_of(b * t, t), t)
    q = q_scr[rows, :]
    outs = []
    for head in range(X_HEADS):
        sl = slice(head * X_HEAD_DIM, (head + 1) * X_HEAD_DIM)
        s = lax.dot_general(q[:, sl], mk_ref[0, :, sl].astype(BF16), (((1,), (1,)), ((), ())),
                            preferred_element_type=F32)
        e = jnp.exp(s - jnp.max(s, axis=-1, keepdims=True))
        p = (e * (1.0 / jnp.sum(e, axis=-1, keepdims=True))).astype(BF16)
        outs.append(jnp.dot(p, mv_ref[0, :, sl].astype(BF16), preferred_element_type=F32))
    o_scr[rows, :] = jnp.concatenate(outs, axis=1).astype(BF16)

    @pl.when(b == pl.num_programs(0) - 1)
    def _():
        y_ref[...] = x_ref[...] + jnp.dot(o_scr[...], wo_ref[...], preferred_element_type=F32)


def _cross_streams(x, mem_k, mem_v, g_cross, w_cq, g_cq, w_co, batch, t):
    n = batch * t
    whole = _full((n, D_MODEL))
    mem = pl.BlockSpec((1, N_MEM, D_MODEL), lambda b: (b, 0, 0))
    sq = _full((D_MODEL, D_MODEL))
    vec = _full((1, D_MODEL))
    return pl.pallas_call(
        functools.partial(_cross_streams_kernel, t=t),
        grid=(batch,),
        in_specs=[whole, vec, sq, vec, mem, mem, sq],
        out_specs=whole,
        out_shape=jax.ShapeDtypeStruct((n, D_MODEL), F32),
        scratch_shapes=[pltpu.VMEM((n, D_MODEL), BF16), pltpu.VMEM((n, D_MODEL), BF16)],
        compiler_params=_params("arbitrary"),
        name="cross_streams",
    )(x, g_cross.reshape(1, -1), w_cq.astype(BF16), jnp.tile(g_cq, X_HEADS).reshape(1, -1),
      mem_k, mem_v, w_co.astype(BF16))


def _mixer_out_cross(x, oa, og, sr, g_gla_out, w_out, mem_k, mem_v, g_cross, w_cq, g_cq, w_co, batch, t, tile):
    nt = t // tile
    row = lambda w: pl.BlockSpec((tile, w), lambda b, i: (b * nt + i, 0))
    mem = pl.BlockSpec((None, N_MEM, D_MODEL), lambda b, i: (b, 0, 0))
    sq = _full((D_MODEL, D_MODEL))
    vec = _full((1, D_MODEL))
    w_out = w_out.astype(BF16)
    return pl.pallas_call(
        _mixer_out_cross_kernel,
        grid=(batch, nt),
        in_specs=[row(D_MODEL), row(A_WIDTH), row(G_WIDTH), row(G_WIDTH), _full((1, G_WIDTH)),
                  _full((G_WIDTH, G_WIDTH)), _full((A_WIDTH, D_MODEL)), _full((G_WIDTH, D_MODEL)),
                  vec, sq, vec, mem, mem, sq],
        out_specs=row(D_MODEL),
        out_shape=jax.ShapeDtypeStruct((batch * t, D_MODEL), F32),
        compiler_params=_params("parallel", "parallel"),
        name="mixer_out_cross",
    )(x, oa, og, sr, jnp.tile(g_gla_out, G_HEADS).reshape(1, -1), _seg_avg_matrix(G_WIDTH, G_DV),
      w_out[:A_WIDTH], w_out[A_WIDTH:], g_cross.reshape(1, -1), w_cq.astype(BF16),
      jnp.tile(g_cq, X_HEADS).reshape(1, -1), mem_k, mem_v, w_co.astype(BF16))


CAND_ROWS = 64


def _sorting_network(n):
    pairs = []
    p = 1
    while p < n:
        k = p
        while k >= 1:
            for j in range(k % p, n - k, 2 * k):
                for i in range(min(k, n - j - k)):
                    if (i + j) // (2 * p) == (i + j + k) // (2 * p):
                        pairs.append((i + j, i + j + k))
            k //= 2
        p *= 2
    return pairs


def _top_values_tiled(s, count):
    nparts = s.shape[0] // SUBLANES
    parts = [s[SUBLANES * k:SUBLANES * (k + 1), :] for k in range(nparts)]
    for i, j in _sorting_network(nparts):
        parts[i], parts[j] = jnp.maximum(parts[i], parts[j]), jnp.minimum(parts[i], parts[j])
    which = lax.broadcasted_iota(jnp.int32, parts[0].shape, 0).astype(F32)
    exhausted = jnp.full(parts[0].shape, -jnp.inf, F32)
    out = []
    for r in range(count):
        mx = jnp.max(parts[0], axis=0, keepdims=True)
        first = jnp.min(jnp.where(parts[0] == mx, which, float(SUBLANES)), axis=0, keepdims=True)
        pop = which == first
        out.append(mx)
        for k in range(min(count - r - 1, nparts)):
            parts[k] = jnp.where(pop, parts[k + 1] if k + 1 < nparts else exhausted, parts[k])
    return jnp.concatenate(out, axis=0)


def _count_leading(desc, x, cmp):
    assert P_TOPK == 16
    row = lambda b: desc[b:b + 1, :]
    pick = lambda c, hi, lo: jnp.where(c, hi, lo)
    c8 = cmp(row(7), x)
    c4 = cmp(pick(c8, row(11), row(3)), x)
    c2 = cmp(pick(c8, pick(c4, row(13), row(9)), pick(c4, row(5), row(1))), x)
    c1 = cmp(pick(c8, pick(c4, pick(c2, row(14), row(12)), pick(c2, row(10), row(8))),
                  pick(c4, pick(c2, row(6), row(4)), pick(c2, row(2), row(0)))), x)
    c_last = cmp(row(15), x)
    count = pick(c8, 8.0, 0.0) + pick(c4, 4.0, 0.0) + pick(c2, 2.0, 0.0) + pick(c1, 1.0, 0.0)
    return count + pick(c_last, 1.0, 0.0)


def _route_kernel(x_ref, gf_ref, whi_ref, wlo_ref, sk_ref, ht_ref, r2_ref, e2_ref, n1_ref, c1_ref):
    hf = _rms(x_ref[...], gf_ref[...])
    ht_ref[...] = jnp.transpose(hf).astype(BF16)
    h_hi = hf.astype(BF16)
    h_lo = (hf - h_hi.astype(F32)).astype(BF16)
    qry = (jnp.dot(h_hi, whi_ref[...], preferred_element_type=F32)
           + jnp.dot(h_hi, wlo_ref[...], preferred_element_type=F32)
           + jnp.dot(h_lo, whi_ref[...], preferred_element_type=F32))
    nk = P_TOPK + 1
    t = qry.shape[0]
    q_hi = qry.astype(BF16)
    q_lo = (qry - q_hi.astype(F32)).astype(BF16)

    def scores(half, cols):
        nt = lambda a, b: lax.dot_general(a, b, (((1,), (1,)), ((), ())), preferred_element_type=F32)
        k_hi, k_lo = sk_ref[2 * half], sk_ref[2 * half + 1]
        return nt(k_hi, q_hi[:, cols]) + nt(k_hi, q_lo[:, cols]) + nt(k_lo, q_hi[:, cols])

    for head in range(P_HEADS):
        s1 = scores(0, slice(head * P_QDIM, head * P_QDIM + P_HALF))
        s2 = scores(1, slice(head * P_QDIM + P_HALF, (head + 1) * P_QDIM))
        v1 = _top_values_tiled(s1, nk)
        v2 = _top_values_tiled(s2, nk)
        slabs = [v1[a:a + 1, :] + v2[:nk // (a + 1), :] for a in range(nk)]
        pad = CAND_ROWS - sum(sl.shape[0] for sl in slabs)
        cand = jnp.concatenate(slabs + [jnp.full((pad, t), -jnp.inf, F32)], axis=0)
        top = _top_values_tiled(cand, nk)
        m1 = v1[0:1, :]
        m2 = v2[0:1, :]
        z = jnp.sum(jnp.exp(top[:P_TOPK, :] - (m1 + m2)), axis=0, keepdims=True)
        tau = 0.5 * (top[P_TOPK - 1:P_TOPK, :] + top[P_TOPK:nk, :])
        thr = tau - s1
        n1 = _count_leading(v2, thr, jnp.greater_equal)
        rank2 = _count_leading(v2, s2, jnp.greater)
        r2_ref[head] = rank2.astype(BF16)
        e2_ref[head] = jnp.exp(s2 - m2).astype(BF16)
        n1_ref[head] = n1
        c1_ref[head] = jnp.exp(s1 - m1) * (1.0 / z)


def _route(x, g_ffn, w_pq, sub_keys1, sub_keys2):
    n = x.shape[0]
    w_hi = w_pq.astype(BF16)
    w_lo = (w_pq - w_hi.astype(F32)).astype(BF16)
    sk = jnp.stack([sub_keys1, sub_keys2])
    sk_hi = sk.astype(BF16)
    sk_parts = jnp.stack([sk_hi, (sk - sk_hi.astype(F32)).astype(BF16)], axis=1).reshape(4, P_NKEYS, P_HALF)
    tm = ROUTE_TILE
    keys = pl.BlockSpec((P_HEADS, P_NKEYS, tm), lambda i: (0, 0, i))
    tab = lambda dt: jax.ShapeDtypeStruct((P_HEADS, P_NKEYS, n), dt)
    return pl.pallas_call(
        _route_kernel,
        grid=(n // tm,),
        in_specs=[pl.BlockSpec((tm, D_MODEL), lambda i: (i, 0)), _full((1, D_MODEL)),
                  _full((D_MODEL, P_HEADS * P_QDIM)), _full((D_MODEL, P_HEADS * P_QDIM)),
                  _full((4, P_NKEYS, P_HALF))],
        out_specs=[pl.BlockSpec((D_MODEL, tm), lambda i: (0, i)), keys, keys, keys, keys],
        out_shape=[jax.ShapeDtypeStruct((D_MODEL, n), BF16), tab(BF16), tab(BF16), tab(F32), tab(F32)],
        compiler_params=_params("parallel"),
        name="peer_route",
    )(x, g_ffn.reshape(1, -1), w_hi, w_lo, sk_parts)


def _peer_kernel(ht_ref, x_ref, u_ref, vt_ref, r2_ref, e2_ref, n1_ref, c1_ref, y_ref, acc_ref):
    j = pl.program_id(1)

    @pl.when(j == 0)
    def _():
        acc_ref[...] = jnp.zeros_like(acc_ref)

    t = ht_ref.shape[1]
    n_i1 = PEER_EXPERT_TILE // P_NKEYS
    zero = jnp.zeros((), BF16)
    def token_row(row):
        return jnp.tile(jnp.broadcast_to(row, (2 * SUBLANES, t)).astype(BF16), (P_NKEYS // (2 * SUBLANES), 1))

    a = jnp.dot(u_ref[...], ht_ref[...], preferred_element_type=F32)
    ab = a.astype(BF16)
    act = ab * (1.0 + lax.erf(ab * (1.0 / math.sqrt(2.0))))
    gates = []
    for r in range(n_i1):
        g = jnp.zeros((P_NKEYS, t), BF16)
        for head in range(P_HEADS):
            n1 = token_row(n1_ref[head, pl.ds(j * n_i1 + r, 1), :])
            c1 = token_row(0.5 * c1_ref[head, pl.ds(j * n_i1 + r, 1), :])
            g = g + jnp.where(r2_ref[head] < n1, e2_ref[head], zero) * c1
        gates.append(g)
    ga = act * jnp.concatenate(gates, axis=0)
    acc_ref[...] += jnp.dot(vt_ref[0], ga, preferred_element_type=F32)

    @pl.when(j == pl.num_programs(1) - 1)
    def _():
        y_ref[...] = x_ref[...] + jnp.transpose(acc_ref[...])


def _peer(ht, x, r2, e2, n1, c1, u, vt, tt):
    n = x.shape[0]
    et = PEER_EXPERT_TILE
    ne = u.shape[0] // et
    once = pl.Buffered(1)
    keys = pl.BlockSpec((P_HEADS, P_NKEYS, tt), lambda i, j: (0, 0, i))
    return pl.pallas_call(
        _peer_kernel,
        grid=(n // tt, ne),
        in_specs=[pl.BlockSpec((D_MODEL, tt), lambda i, j: (0, i)),
                  pl.BlockSpec((tt, D_MODEL), lambda i, j: (i, 0), pipeline_mode=once),
                  pl.BlockSpec((et, D_MODEL), lambda i, j: (j, 0)),
                  pl.BlockSpec((1, D_MODEL, et), lambda i, j: (j, 0, 0)),
                  keys, keys, keys, keys],
        out_specs=pl.BlockSpec((tt, D_MODEL), lambda i, j: (i, 0), pipeline_mode=once),
        out_shape=jax.ShapeDtypeStruct((n, D_MODEL), F32),
        scratch_shapes=[pltpu.VMEM((D_MODEL, tt), F32)],
        compiler_params=_params("parallel", "arbitrary"),
        name="peer_experts",
    )(ht, x, u, vt, r2, e2, n1, c1)


def _layer(xp, xs, cache_k, cache_v, state, cmem_k, cmem_v, mem_prompt, past_len,
           g_mix, w_in, g_q, g_k, rel_bias, w_gate2, b_gate2, g_gla_out, w_out,
           g_cross, g_mem, w_cq, w_ck, w_cv, g_cq, g_ck, w_co, g_ffn, w_pq, sub_keys1, sub_keys2, expert_u, expert_v):
    bp, sp, _ = xp.shape
    bs, ts, _ = xs.shape
    xp = xp.reshape(bp * sp, D_MODEL)
    xs = xs.reshape(bs * ts, D_MODEL)
    u = expert_u.astype(BF16)
    ne = expert_v.shape[0] // PEER_EXPERT_TILE
    vt = jnp.swapaxes(expert_v.reshape(ne, PEER_EXPERT_TILE, D_MODEL), 1, 2).astype(BF16)
    heads = lambda a, b, t: a.reshape(b, t, A_HEADS, A_HEAD_DIM)

    def peer(x, peer_tile):
        ht, r2, e2, n1, c1 = _route(x, g_ffn, w_pq, sub_keys1, sub_keys2)
        return _peer(ht, x, r2, e2, n1, c1, u, vt, peer_tile)

    qa, ka_f, ka_b, va_f, va_b, qg, kg, vg, la, sr = _mixer_in(xp, g_mix, w_in, g_q, g_k, w_gate2, b_gate2)
    oa = _attn_prompt(qa, ka_b, va_b, rel_bias, bp, sp)
    og, gla_p = _gla(qg, kg, vg, la, None, bp, sp, GLA_TILE)
    mk, mv, mk_b, mv_b = _mem_kv(mem_prompt, g_mem, w_ck, w_cv, g_ck)
    xp = peer(_mixer_out_cross(xp, oa, og, sr, g_gla_out, w_out, mk_b, mv_b, g_cross, w_cq, g_cq, w_co,
                               bp, sp, TOKEN_TILE), PEER_TOKEN_TILE)
    keep = min(BAND, sp)
    outs_p = (heads(ka_f, bp, sp)[:, -keep:], heads(va_f, bp, sp)[:, -keep:], gla_p, mk, mv)

    qa, ka_f, ka_b, va_f, va_b, qg, kg, vg, la, sr = _mixer_in(xs, g_mix, w_in, g_q, g_k, w_gate2, b_gate2)
    oa = _attn_sample(qa, ka_b, va_b, cache_k, cache_v, rel_bias, bs, ts, past_len)
    og, gla_s = _gla(qg, kg, vg, la, state, bs, ts, ts)
    xs = _mixer_out(xs, oa, og, sr, g_gla_out, w_out)
    xs = _cross_streams(xs, cmem_k.reshape(bs, N_MEM, D_MODEL), cmem_v.reshape(bs, N_MEM, D_MODEL),
                        g_cross, w_cq, g_cq, w_co, bs, ts)
    xs = peer(xs, bs * ts)
    outs_s = (heads(ka_f, bs, ts), heads(va_f, bs, ts), gla_s)

    return (xp.reshape(bp, sp, D_MODEL), xs.reshape(bs, ts, D_MODEL)) + outs_p + outs_s


def kernel(x_prompt, x_sample, cache_att_k, cache_att_v, state_gla, cache_mem_k, cache_mem_v, mem_prompt, g_mix, w_in, g_q, g_k, rel_bias, w_gate2, b_gate2, g_gla_out, w_out, g_cross, g_mem, w_cq, w_ck, w_cv, g_cq, g_ck, w_co, g_ffn, w_pq, sub_keys1, sub_keys2, expert_u, expert_v):
    depth = w_in.shape[0]
    past_len = 4096
    xp, xs = x_prompt, x_sample
    per_layer = []
    for l in range(depth):
        outs = _layer(xp, xs, cache_att_k[l], cache_att_v[l], state_gla[l], cache_mem_k[l], cache_mem_v[l],
                      mem_prompt, past_len,
                      g_mix[l], w_in[l], g_q[l], g_k[l], rel_bias[l], w_gate2[l], b_gate2[l], g_gla_out[l], w_out[l],
                      g_cross[l], g_mem[l], w_cq[l], w_ck[l], w_cv[l], g_cq[l], g_ck[l], w_co[l],
                      g_ffn[l], w_pq[l], sub_keys1[l], sub_keys2[l], expert_u[l], expert_v[l])
        xp, xs = outs[0], outs[1]
        per_layer.append(outs[2:])
    stacked = [jnp.stack([p[i] for p in per_layer]) for i in range(8)]
    return (xp, xs) + tuple(stacked)
```
